```python
import math
import jax
import jax.numpy as jnp
from jax import lax
import numpy as np

D_MODEL = 1024
BATCH = 8
SEQ = 4096
DEPTH = 4

N_MIXERS = 3
N_LAYERS_RET = (DEPTH + 2) // 3
N_LAYERS_GDN = (DEPTH + 1) // 3
N_LAYERS_RWKV = DEPTH // 3
NORM_EPS = 1e-6
F32 = jnp.float32

RET_HEADS = 4
RET_DK = D_MODEL // RET_HEADS
RET_DV = 2 * D_MODEL // RET_HEADS
RET_CHUNK = 128
RET_ROPE_BASE = 10000.0
RET_GN_EPS = 1e-5
RET_IN = 2 * RET_HEADS * RET_DK + 2 * RET_HEADS * RET_DV

GDN_HEADS = 8
GDN_DK = D_MODEL // GDN_HEADS
GDN_DV = D_MODEL // GDN_HEADS
GDN_CONV = 4
GDN_CHUNK = 64
GDN_QKV = GDN_HEADS * (2 * GDN_DK + GDN_DV)
GDN_IN = GDN_QKV + GDN_HEADS * GDN_DV + 2 * GDN_HEADS

RWKV_HEAD = 64
RWKV_HEADS = D_MODEL // RWKV_HEAD
RWKV_DECAY_LORA = 64
RWKV_A_LORA = 64
RWKV_GATE_LORA = 160
RWKV_GN_EPS = 64e-5

PEER_KEYS = 128
PEER_EXPERTS = PEER_KEYS * PEER_KEYS
PEER_HEADS = 8
PEER_DQ = 256
PEER_TOPK = 16
PEER_TOKEN_BLOCK = 128

kernel_name = 'hybrid_ret_gdn_rwkv7_peer_adaln'


def rms_norm(x, g, eps=NORM_EPS):
    xf = x.astype(F32)
    y = xf * lax.rsqrt(jnp.mean(xf * xf, axis=-1, keepdims=True) + eps)
    return (y * g).astype(x.dtype)


def head_layer_norm(x, g, b, eps):
    xf = x.astype(F32)
    mu = jnp.mean(xf, axis=-1, keepdims=True)
    var = jnp.mean(jnp.square(xf - mu), axis=-1, keepdims=True)
    return ((xf - mu) * lax.rsqrt(var + eps) * g + b).astype(x.dtype)


def l2_normalize(x, eps=1e-6):
    xf = x.astype(F32)
    return xf * lax.rsqrt(jnp.sum(xf * xf, axis=-1, keepdims=True) + eps)


def rotary(x):
    S, d = x.shape[1], x.shape[-1]
    half = d // 2
    inv_freq = RET_ROPE_BASE ** (-jnp.arange(half, dtype=F32) / half)
    ang = jnp.arange(S, dtype=F32)[:, None] * inv_freq[None, :]
    cos = jnp.cos(ang)[None, :, None, :]
    sin = jnp.sin(ang)[None, :, None, :]
    x1, x2 = x[..., :half], x[..., half:]
    return jnp.concatenate([x1 * cos - x2 * sin, x1 * sin + x2 * cos], axis=-1).astype(x.dtype)


def causal_depthwise_conv(x, w):
    ch = x.shape[-1]
    return lax.conv_general_dilated(
        x, w[:, None, :].astype(x.dtype), window_strides=(1,),
        padding=[(w.shape[0] - 1, 0)], dimension_numbers=('NWC', 'WIO', 'NWC'),
        feature_group_count=ch)


def retention_mixer(h, w_in, w_out, gn_g, gn_b):
    B, S, _ = h.shape
    H, dk, dv, C = RET_HEADS, RET_DK, RET_DV, RET_CHUNK
    N = S // C
    proj = h @ w_in
    q, k, v, gate = jnp.split(proj, [H * dk, 2 * H * dk, 2 * H * dk + H * dv], axis=-1)
    q = rotary(q.reshape(B, S, H, dk))
    k = rotary(k.reshape(B, S, H, dk)) * (dk ** -0.5)
    v = v.reshape(B, S, H, dv)
    log_gamma = jnp.log1p(-jnp.exp2(-5.0 - jnp.arange(H, dtype=F32)))
    idx = jnp.arange(C, dtype=F32)
    diff = idx[:, None] - idx[None, :]
    causal = diff >= 0
    decay_intra = jnp.where(causal[None], jnp.exp(jnp.where(causal, diff, 0.0)[None] * log_gamma[:, None, None]), 0.0)
    decay_q = jnp.exp((idx[:, None] + 1.0) * log_gamma[None, :])
    decay_k = jnp.exp((C - 1.0 - idx)[None, :] * log_gamma[:, None])
    decay_chunk = jnp.exp(C * log_gamma)

    def to_chunks(t):
        return jnp.moveaxis(t.reshape(B, N, C, H, t.shape[-1]), 1, 0)

    def step(state, inp):
        qc, kc, vc = inp
        scores = jnp.einsum('bihd,bjhd->bhij', qc, kc) * decay_intra[None]
        o = jnp.einsum('bhij,bjhv->bihv', scores, vc)
        o = o + jnp.einsum('bihd,bhdv->bihv', qc, state) * decay_q[None, :, :, None]
        state = state * decay_chunk[None, :, None, None] + jnp.einsum('bjhd,hj,bjhv->bhdv', kc, decay_k, vc)
        return state, o

    _, o = lax.scan(step, jnp.zeros((B, H, dk, dv), F32), (to_chunks(q), to_chunks(k), to_chunks(v)))
    o = jnp.moveaxis(o, 0, 1).reshape(B, S, H, dv)
    o = head_layer_norm(o, gn_g, gn_b, RET_GN_EPS).reshape(B, S, H * dv).astype(h.dtype)
    return (jax.nn.silu(gate) * o) @ w_out


def gated_deltanet_mixer(h, w_in, conv_w, a_log, dt_bias, norm_g, w_out):
    B, S, _ = h.shape
    H, dk, dv, C = GDN_HEADS, GDN_DK, GDN_DV, GDN_CHUNK
    N = S // C
    proj = h @ w_in
    qkv, gate, a_in, b_in = jnp.split(proj, [GDN_QKV, GDN_QKV + H * dv, GDN_QKV + H * dv + H], axis=-1)
    qkv = jax.nn.silu(causal_depthwise_conv(qkv, conv_w))
    q, k, v = jnp.split(qkv, [H * dk, 2 * H * dk], axis=-1)
    q = l2_normalize(q.reshape(B, S, H, dk)) * (dk ** -0.5)
    k = l2_normalize(k.reshape(B, S, H, dk))
    v = v.reshape(B, S, H, dv).astype(F32)
    beta = jax.nn.sigmoid(b_in.astype(F32))
    g = -jnp.exp(a_log.astype(F32)) * jax.nn.softplus(a_in.astype(F32) + dt_bias.astype(F32))

    def chunk(t):
        return jnp.swapaxes(t.reshape((B, N, C, H) + t.shape[3:]), 2, 3)

    qc, kc, vc = chunk(q), chunk(k), chunk(v)
    gc, bc = chunk(g), chunk(beta)
    cum = jnp.cumsum(gc, axis=-1)
    tri_incl = jnp.tril(jnp.ones((C, C), bool))
    tri_strict = jnp.tril(jnp.ones((C, C), bool), -1)
    decay_incl = jnp.exp(jnp.where(tri_incl, cum[..., :, None] - cum[..., None, :], -jnp.inf))
    L = jnp.where(tri_strict, jnp.einsum('bnhid,bnhjd->bnhij', kc, kc) * decay_incl, 0.0) * bc[..., :, None]
    rhs = jnp.concatenate([kc * (bc * jnp.exp(cum))[..., None], vc * bc[..., None]], axis=-1)
    sol = lax.linalg.triangular_solve(jnp.eye(C, dtype=F32) + L, rhs, left_side=True, lower=True, unit_diagonal=True)
    w_c, u_c = sol[..., :dk], sol[..., dk:]
    a_qk = jnp.einsum('bnhid,bnhjd->bnhij', qc, kc) * decay_incl
    q_dec = qc * jnp.exp(cum)[..., None]
    k_dec = kc * jnp.exp(cum[..., -1:] - cum)[..., None]
    end_dec = jnp.exp(cum[..., -1])

    def step(state, inp):
        wc, uc, aqk, qd, kd, ed = inp
        u = uc - jnp.einsum('bhcd,bhdv->bhcv', wc, state)
        o = jnp.einsum('bhcd,bhdv->bhcv', qd, state) + jnp.einsum('bhij,bhjv->bhiv', aqk, u)
        state = state * ed[..., None, None] + jnp.einsum('bhcd,bhcv->bhdv', kd, u)
        return state, o

    xs = tuple(jnp.moveaxis(t, 1, 0) for t in (w_c, u_c, a_qk, q_dec, k_dec, end_dec))
    _, o = lax.scan(step, jnp.zeros((B, H, dk, dv), F32), xs)
    o = jnp.transpose(o, (1, 0, 3, 2, 4)).reshape(B, S, H, dv)
    o = rms_norm(o, norm_g) * jax.nn.silu(gate.reshape(B, S, H, dv).astype(F32))
    return o.reshape(B, S, H * dv).astype(h.dtype) @ w_out


def rwkv7_mixer(h, mu, w_r, w_k, w_v, w_o, w0, w1, w2, a0, a1, a2, g1, g2, k_k, k_a, r_k, ln_g, ln_b):
    B, S, D = h.shape
    H, N = RWKV_HEADS, RWKV_HEAD
    xx = jnp.pad(h, ((0, 0), (1, 0), (0, 0)))[:, :-1] - h
    xr, xw, xk, xv, xa, xg = [h + xx * mu[i] for i in range(6)]
    r = xr @ w_r
    w = -jax.nn.softplus(-(w0 + jnp.tanh(xw @ w1) @ w2).astype(F32)) - 0.5
    k = xk @ w_k
    v = xv @ w_v
    a = jax.nn.sigmoid((a0 + (xa @ a1) @ a2).astype(F32))
    g = jax.nn.sigmoid(xg @ g1) @ g2

    def heads(t):
        return t.reshape(B, S, H, N)

    kk = l2_normalize(heads(k * k_k))
    k = k * (1.0 + (a - 1.0) * k_a)
    r_h, k_h, v_h, a_h = heads(r).astype(F32), heads(k).astype(F32), heads(v).astype(F32), heads(a)
    decay = jnp.exp(-jnp.exp(heads(w)))

    def step(state, inp):
        r_t, w_t, k_t, v_t, kk_t, a_t = inp
        sa = jnp.einsum('bhvk,bhk->bhv', state, -kk_t)
        state = (state * w_t[:, :, None, :] + sa[..., None] * (kk_t * a_t)[:, :, None, :]
                 + v_t[..., None] * k_t[:, :, None, :])
        return state, jnp.einsum('bhvk,bhk->bhv', state, r_t)

    xs = tuple(jnp.moveaxis(t, 1, 0) for t in (r_h, decay, k_h, v_h, kk, a_h))
    _, y = lax.scan(step, jnp.zeros((B, H, N, N), F32), xs)
    y = head_layer_norm(jnp.moveaxis(y, 0, 1), ln_g, ln_b, RWKV_GN_EPS)
    y = y + jnp.sum(r_h * k_h * r_k, axis=-1, keepdims=True) * v_h
    return (y.reshape(B, S, D).astype(h.dtype) * g) @ w_o


def peer_ffn(h, w_q, sub_keys, u, v):
    B, S, D = h.shape
    T = B * S
    H, K, half, TB = PEER_HEADS, PEER_TOPK, PEER_DQ // 2, PEER_TOKEN_BLOCK
    hf = h.reshape(T, D)
    q = (hf @ w_q).reshape(T, H, 2, half)
    sub_scores = jnp.einsum('thpd,hpnd->thpn', q, sub_keys).astype(F32)
    s_top, i_top = lax.top_k(sub_scores, K)
    cand = s_top[:, :, 0, :, None] + s_top[:, :, 1, None, :]
    best, pos = lax.top_k(cand.reshape(T, H, K * K), K)
    i_a = jnp.take_along_axis(i_top[:, :, 0], pos // K, axis=-1)
    i_b = jnp.take_along_axis(i_top[:, :, 1], pos % K, axis=-1)
    expert = (i_a * PEER_KEYS + i_b).reshape(T, H * K)
    gate = jax.nn.softmax(best, axis=-1).reshape(T, H * K)
    nb = T // TB

    def block(inp):
        xb, eb, gb = inp
        act = jax.nn.gelu(jnp.einsum('tkd,td->tk', u[eb], xb).astype(F32), approximate=False)
        return jnp.einsum('tk,tkd->td', (gb * act).astype(xb.dtype), v[eb])

    out = lax.map(block, (hf.reshape(nb, TB, D), expert.reshape(nb, TB, H * K), gate.reshape(nb, TB, H * K)))
    return out.reshape(B, S, D)


def setup_inputs(seed: int = 0) -> dict:
    key = jax.random.key(seed)
    ks = iter(jax.random.split(key, 64))
    D = D_MODEL
    NA, NB, NC = N_LAYERS_RET, N_LAYERS_GDN, N_LAYERS_RWKV

    def nrm(shape, scale):
        return jax.random.normal(next(ks), shape, F32) * scale

    def unif(shape, lo, hi):
        return jax.random.uniform(next(ks), shape, F32, lo, hi)

    dt = jnp.exp(unif((NB, GDN_HEADS), math.log(1e-3), math.log(1e-1)))
    return {
        'x': nrm((BATCH, SEQ, D), 1.0),
        'c': nrm((BATCH, D), 1.0),
        'ada_w': nrm((DEPTH, D, 6 * D), 0.5 * D ** -0.5),
        'ada_b': nrm((DEPTH, 6 * D), 0.02),
        'norm_mix_g': 1.0 + nrm((DEPTH, D), 0.02),
        'norm_ffn_g': 1.0 + nrm((DEPTH, D), 0.02),
        'final_norm_g': 1.0 + nrm((D,), 0.02),
        'ret_w_in': nrm((NA, D, RET_IN), D ** -0.5),
        'ret_w_out': nrm((NA, RET_HEADS * RET_DV, D), (RET_HEADS * RET_DV) ** -0.5),
        'ret_gn_g': 1.0 + nrm((NA, RET_HEADS, RET_DV), 0.02),
        'ret_gn_b': nrm((NA, RET_HEADS, RET_DV), 0.02),
        'gdn_w_in': nrm((NB, D, GDN_IN), D ** -0.5),
        'gdn_conv_w': nrm((NB, GDN_CONV, GDN_QKV), GDN_CONV ** -0.5),
        'gdn_a_log': jnp.log(unif((NB, GDN_HEADS), 1.0, 16.0)),
        'gdn_dt_bias': dt + jnp.log(-jnp.expm1(-dt)),
        'gdn_norm_g': 1.0 + nrm((NB, GDN_DV), 0.02),
        'gdn_w_out': nrm((NB, GDN_HEADS * GDN_DV, D), (GDN_HEADS * GDN_DV) ** -0.5),
        'rwkv_mu': unif((NC, 6, D), 0.0, 1.0),
        'rwkv_w_r': nrm((NC, D, D), D ** -0.5),
        'rwkv_w_k': nrm((NC, D, D), D ** -0.5),
        'rwkv_w_v': nrm((NC, D, D), D ** -0.5),
        'rwkv_w_o': nrm((NC, D, D), D ** -0.5),
        'rwkv_w0': unif((NC, D), -6.0, -1.0),
        'rwkv_w1': nrm((NC, D, RWKV_DECAY_LORA), D ** -0.5),
        'rwkv_w2': nrm((NC, RWKV_DECAY_LORA, D), 0.5 * RWKV_DECAY_LORA ** -0.5),
        'rwkv_a0': nrm((NC, D), 0.1),
        'rwkv_a1': nrm((NC, D, RWKV_A_LORA), D ** -0.5),
        'rwkv_a2': nrm((NC, RWKV_A_LORA, D), 0.5 * RWKV_A_LORA ** -0.5),
        'rwkv_g1': nrm((NC, D, RWKV_GATE_LORA), D ** -0.5),
        'rwkv_g2': nrm((NC, RWKV_GATE_LORA, D), RWKV_GATE_LORA ** -0.5),
        'rwkv_k_k': 0.85 + nrm((NC, D), 0.02),
        'rwkv_k_a': 1.0 + nrm((NC, D), 0.02),
        'rwkv_r_k': nrm((NC, RWKV_HEADS, RWKV_HEAD), 0.1),
        'rwkv_ln_g': 1.0 + nrm((NC, RWKV_HEADS, RWKV_HEAD), 0.02),
        'rwkv_ln_b': nrm((NC, RWKV_HEADS, RWKV_HEAD), 0.02),
        'peer_w_q': nrm((DEPTH, D, PEER_HEADS * PEER_DQ), D ** -0.5),
        'peer_sub_keys': nrm((DEPTH, PEER_HEADS, 2, PEER_KEYS, PEER_DQ // 2), (PEER_DQ // 2) ** -0.5),
        'peer_u': nrm((DEPTH, PEER_EXPERTS, D), D ** -0.5),
        'peer_v': nrm((DEPTH, PEER_EXPERTS, D), 0.5),
    }


def reference(x, c, ada_w, ada_b, norm_mix_g, norm_ffn_g, final_norm_g,
              ret_w_in, ret_w_out, ret_gn_g, ret_gn_b,
              gdn_w_in, gdn_conv_w, gdn_a_log, gdn_dt_bias, gdn_norm_g, gdn_w_out,
              rwkv_mu, rwkv_w_r, rwkv_w_k, rwkv_w_v, rwkv_w_o, rwkv_w0, rwkv_w1, rwkv_w2,
              rwkv_a0, rwkv_a1, rwkv_a2, rwkv_g1, rwkv_g2, rwkv_k_k, rwkv_k_a, rwkv_r_k,
              rwkv_ln_g, rwkv_ln_b,
              peer_w_q, peer_sub_keys, peer_u, peer_v):
    cond = jax.nn.silu(c)
    for layer in range(DEPTH):
        mod = cond @ ada_w[layer] + ada_b[layer]
        sh_m, sc_m, gt_m, sh_f, sc_f, gt_f = jnp.split(mod[:, None, :], 6, axis=-1)
        hm = rms_norm(x, norm_mix_g[layer]) * (1.0 + sc_m) + sh_m
        kind, j = layer % N_MIXERS, layer // N_MIXERS
        if kind == 0:
            y = retention_mixer(hm, ret_w_in[j], ret_w_out[j], ret_gn_g[j], ret_gn_b[j])
        elif kind == 1:
            y = gated_deltanet_mixer(hm, gdn_w_in[j], gdn_conv_w[j], gdn_a_log[j], gdn_dt_bias[j],
                                     gdn_norm_g[j], gdn_w_out[j])
        else:
            y = rwkv7_mixer(hm, rwkv_mu[j], rwkv_w_r[j], rwkv_w_k[j], rwkv_w_v[j], rwkv_w_o[j],
                            rwkv_w0[j], rwkv_w1[j], rwkv_w2[j], rwkv_a0[j], rwkv_a1[j], rwkv_a2[j],
                            rwkv_g1[j], rwkv_g2[j], rwkv_k_k[j], rwkv_k_a[j], rwkv_r_k[j],
                            rwkv_ln_g[j], rwkv_ln_b[j])
        x = x + gt_m * y
        hf = rms_norm(x, norm_ffn_g[layer]) * (1.0 + sc_f) + sh_f
        x = x + gt_f * peer_ffn(hf, peer_w_q[layer], peer_sub_keys[layer], peer_u[layer], peer_v[layer])
    return rms_norm(x, final_norm_g)
```

```python
import functools
import math

import jax
import jax.numpy as jnp
from jax import lax
from jax.experimental import pallas as pl
from jax.experimental.pallas import tpu as pltpu

F32 = jnp.float32
BF16 = jnp.bfloat16
HIGHEST = lax.Precision.HIGHEST

D_MODEL = 1024
NORM_EPS = 1e-6

RET_HEADS = 4
RET_DK = D_MODEL // RET_HEADS
RET_DV = 2 * D_MODEL // RET_HEADS
RET_CHUNK = 128
RET_ROPE_BASE = 10000.0
RET_GN_EPS = 1e-5

GDN_HEADS = 8
GDN_DK = D_MODEL // GDN_HEADS
GDN_DV = D_MODEL // GDN_HEADS
GDN_CONV = 4
GDN_CHUNK = 64
GDN_QKV = GDN_HEADS * (2 * GDN_DK + GDN_DV)

RWKV_HEAD = 64
RWKV_HEADS = D_MODEL // RWKV_HEAD
RWKV_GN_EPS = 64e-5

PEER_KEYS = 128
PEER_HEADS = 8
PEER_DQ = 256
PEER_TOPK = 16
PEER_SEL = PEER_HEADS * PEER_TOPK

LANES = 128
SUBLANES = 8
VMEM_LIMIT = 56 * 1024 * 1024


def _cparams(sem):
    return pltpu.CompilerParams(dimension_semantics=sem, vmem_limit_bytes=VMEM_LIMIT)


def _bdot(a, b):
    return jnp.dot(a.astype(BF16), b.astype(BF16), preferred_element_type=F32)


def _bdot_nt(a, b):
    return lax.dot_general(a.astype(BF16), b.astype(BF16), (((1,), (1,)), ((), ())),
                           preferred_element_type=F32)


def _bdot_tn(a, b):
    return lax.dot_general(a.astype(BF16), b.astype(BF16), (((0,), (0,)), ((), ())),
                           preferred_element_type=F32)


def _hdot(a, b):
    return jnp.dot(a, b, preferred_element_type=F32, precision=HIGHEST)


def _sigmoid(x):
    return 1.0 / (1.0 + jnp.exp(-x))


def _silu(x):
    return x * _sigmoid(x)


def _softplus(x):
    return jnp.maximum(x, 0.0) + jnp.log1p(jnp.exp(-jnp.abs(x)))


def _norm_mod(x, g, sc, sh):
    ms = jnp.mean(x * x, axis=-1, keepdims=True)
    return (x * lax.rsqrt(ms + NORM_EPS) * g) * (1.0 + sc) + sh


def _adaln_kernel(c_ref, w_ref, b_ref, o_ref):
    cond = _silu(c_ref[...])
    o_ref[0] = _bdot(cond, w_ref[0]) + b_ref[0]


def _adaln(c, ada_w, ada_b):
    depth, d, n = ada_w.shape
    b = c.shape[0]
    tn = 1024
    return pl.pallas_call(
        _adaln_kernel,
        out_shape=jax.ShapeDtypeStruct((depth, b, n), F32),
        grid=(depth, n // tn),
        in_specs=[
            pl.BlockSpec((b, d), lambda l, j: (0, 0)),
            pl.BlockSpec((1, d, tn), lambda l, j: (l, 0, j)),
            pl.BlockSpec((1, 1, tn), lambda l, j: (l, 0, j)),
        ],
        out_specs=pl.BlockSpec((1, b, tn), lambda l, j: (l, 0, j)),
        compiler_params=_cparams(("parallel", "parallel")),
        name="adaln",
    )(c, ada_w, ada_b.reshape(depth, 1, n))


def _nm_matmul_kernel(x_ref, g_ref, sc_ref, sh_ref, w_ref, o_ref, *h_ref):
    h = _norm_mod(x_ref[...], g_ref[...], sc_ref[0], sh_ref[0])
    o_ref[...] = jnp.dot(h.astype(BF16), w_ref[...], preferred_element_type=F32).astype(o_ref.dtype)
    if h_ref:
        h_ref[0][...] = h


def _nm_matmul(x2d, g, sc, sh, w, *, rows_per_batch, tm=512, tn=None, emit_h=False):
    m, d = x2d.shape
    n = w.shape[1]
    nb = sc.shape[0]
    if tn is None:
        tn = n
    tpb = rows_per_batch // tm
    out_shape = [jax.ShapeDtypeStruct((m, n), F32)]
    out_specs = [pl.BlockSpec((tm, tn), lambda j, i: (i, j))]
    if emit_h:
        assert tn == n
        out_shape.append(jax.ShapeDtypeStruct((m, d), F32))
        out_specs.append(pl.BlockSpec((tm, d), lambda j, i: (i, 0)))
    res = pl.pallas_call(
        _nm_matmul_kernel,
        out_shape=out_shape,
        grid=(n // tn, m // tm),
        in_specs=[
            pl.BlockSpec((tm, d), lambda j, i: (i, 0)),
            pl.BlockSpec((1, d), lambda j, i: (0, 0)),
            pl.BlockSpec((1, 1, d), lambda j, i: (i // tpb, 0, 0)),
            pl.BlockSpec((1, 1, d), lambda j, i: (i // tpb, 0, 0)),
            pl.BlockSpec((d, tn), lambda j, i: (0, j)),
        ],
        out_specs=out_specs,
        compiler_params=_cparams(("parallel", "parallel")),
        name="norm_mod_matmul",
    )(x2d, g.reshape(1, d), sc.reshape(nb, 1, d), sh.reshape(nb, 1, d), w)
    return res if emit_h else res[0]


def _mm_res_kernel(*refs, has_mul):
    if has_mul:
        a_ref, m_ref, w_ref, r_ref, gt_ref, o_ref = refs
        a = a_ref[...] * m_ref[...]
    else:
        a_ref, w_ref, r_ref, gt_ref, o_ref = refs
        a = a_ref[...]
    y = jnp.dot(a.astype(BF16), w_ref[...], preferred_element_type=F32)
    o_ref[...] = r_ref[...] + gt_ref[0] * y


def _mm_res(a, w, res, gt, *, rows_per_batch, mul=None, tm=512):
    m, k = a.shape
    n = w.shape[1]
    nb = gt.shape[0]
    tpb = rows_per_batch // tm
    ins = [a]
    specs = [pl.BlockSpec((tm, k), lambda i: (i, 0))]
    if mul is not None:
        ins.append(mul)
        specs.append(pl.BlockSpec((tm, k), lambda i: (i, 0)))
    ins += [w, res, gt.reshape(nb, 1, n)]
    specs += [
        pl.BlockSpec((k, n), lambda i: (0, 0)),
        pl.BlockSpec((tm, n), lambda i: (i, 0)),
        pl.BlockSpec((1, 1, n), lambda i: (i // tpb, 0, 0)),
    ]
    return pl.pallas_call(
        functools.partial(_mm_res_kernel, has_mul=mul is not None),
        out_shape=jax.ShapeDtypeStruct((m, n), F32),
        grid=(m // tm,),
        in_specs=specs,
        out_specs=pl.BlockSpec((tm, n), lambda i: (i, 0)),
        compiler_params=_cparams(("parallel",)),
        name="matmul_residual",
    )(*ins)


def _ret_kernel(q_ref, k_ref, v_ref, gate_ref, cos_ref, sin_ref, dintra_ref, dq_ref, dk_ref, dchunk_ref,
                gng_ref, gnb_ref, o_ref, state_ref):
    H, dk, dv = RET_HEADS, RET_DK, RET_DV
    half = dk // 2

    @pl.when(pl.program_id(1) == 0)
    def _():
        state_ref[...] = jnp.zeros_like(state_ref)

    cos = cos_ref[...]
    sin = sin_ref[...]

    def rot(ref, h):
        x1 = ref[0, :, h * dk:h * dk + half]
        x2 = ref[0, :, h * dk + half:(h + 1) * dk]
        return jnp.concatenate([x1 * cos - x2 * sin, x1 * sin + x2 * cos], axis=-1)

    for h in range(H):
        q = rot(q_ref, h)
        k = rot(k_ref, h) * (dk ** -0.5)
        v = v_ref[0, :, h * dv:(h + 1) * dv]
        scores = _bdot_nt(q, k) * dintra_ref[h]
        st = state_ref[h]
        o = _bdot(scores, v) + _bdot(q, st) * dq_ref[h]
        state_ref[h] = st * dchunk_ref[h] + _bdot_tn(k * dk_ref[h], v)
        mu = jnp.mean(o, axis=-1, keepdims=True)
        var = jnp.mean(jnp.square(o - mu), axis=-1, keepdims=True)
        on = (o - mu) * lax.rsqrt(var + RET_GN_EPS) * gng_ref[h] + gnb_ref[h]
        g = gate_ref[0, :, h * dv:(h + 1) * dv]
        o_ref[0, :, h * dv:(h + 1) * dv] = (_silu(g) * on).astype(o_ref.dtype)


def _retention_scan(proj, gn_g, gn_b):
    B, S, _ = proj.shape
    H, dk, dv, C = RET_HEADS, RET_DK, RET_DV, RET_CHUNK
    half = dk // 2
    N = S // C
    inv_freq = RET_ROPE_BASE ** (-jnp.arange(half, dtype=F32) / half)
    ang = jnp.arange(S, dtype=F32)[:, None] * inv_freq[None, :]
    cos, sin = jnp.cos(ang), jnp.sin(ang)
    log_gamma = jnp.log1p(-jnp.exp2(-5.0 - jnp.arange(H, dtype=F32)))
    idx = jnp.arange(C, dtype=F32)
    diff = idx[:, None] - idx[None, :]
    causal = diff >= 0
    d_intra = jnp.where(causal[None], jnp.exp(jnp.where(causal, diff, 0.0)[None] * log_gamma[:, None, None]), 0.0)
    d_q = jnp.exp((idx[None, :] + 1.0) * log_gamma[:, None])[:, :, None]
    d_k = jnp.exp((C - 1.0 - idx)[None, :] * log_gamma[:, None])[:, :, None]
    d_chunk = jnp.exp(C * log_gamma)[:, None, None]
    qw, vw = H * dk, H * dv
    return pl.pallas_call(
        _ret_kernel,
        out_shape=jax.ShapeDtypeStruct((B, S, vw), BF16),
        grid=(B, N),
        in_specs=[
            pl.BlockSpec((1, C, qw), lambda b, n: (b, n, 0)),
            pl.BlockSpec((1, C, qw), lambda b, n: (b, n, 1)),
            pl.BlockSpec((1, C, vw), lambda b, n: (b, n, 1)),
            pl.BlockSpec((1, C, vw), lambda b, n: (b, n, 2)),
            pl.BlockSpec((C, half), lambda b, n: (n, 0)),
            pl.BlockSpec((C, half), lambda b, n: (n, 0)),
            pl.BlockSpec((H, C, C), lambda b, n: (0, 0, 0)),
            pl.BlockSpec((H, C, 1), lambda b, n: (0, 0, 0)),
            pl.BlockSpec((H, C, 1), lambda b, n: (0, 0, 0)),
            pl.BlockSpec((H, 1, 1), lambda b, n: (0, 0, 0)),
            pl.BlockSpec((H, 1, dv), lambda b, n: (0, 0, 0)),
            pl.BlockSpec((H, 1, dv), lambda b, n: (0, 0, 0)),
        ],
        out_specs=pl.BlockSpec((1, C, vw), lambda b, n: (b, n, 0)),
        scratch_shapes=[pltpu.VMEM((H, dk, dv), F32)],
        compiler_params=_cparams(("parallel", "arbitrary")),
        name="retention_scan",
    )(proj, proj, proj, proj, cos, sin, d_intra, d_q, d_k, d_chunk,
      gn_g.reshape(H, 1, dv), gn_b.reshape(H, 1, dv))


def _shift_rows(cur, prev8, s):
    rows = lax.broadcasted_iota(jnp.int32, cur.shape, 0)
    rolled = pltpu.roll(cur, s, axis=0)
    head = pltpu.roll(prev8, s, axis=0)
    head = jnp.concatenate([head, jnp.zeros((cur.shape[0] - SUBLANES, cur.shape[1]), cur.dtype)], axis=0)
    return jnp.where(rows < s, head, rolled)


def _cumsum_rows(x):
    rows = lax.broadcasted_iota(jnp.int32, x.shape, 0)
    s = 1
    while s < x.shape[0]:
        x = x + jnp.where(rows >= s, pltpu.roll(x, s, axis=0), 0.0)
        s *= 2
    return x


def _unit_lower_inverse(L):
    C = L.shape[0]
    eye = (lax.broadcasted_iota(jnp.int32, (C, C), 0) == lax.broadcasted_iota(jnp.int32, (C, C), 1)).astype(F32)
    p = -L
    inv = eye + p
    s = 2
    while s < C:
        p = _hdot(p, p)
        inv = inv + _hdot(inv, p)
        s *= 2
    return inv


def _gdn_kernel(q_ref, k_ref, v_ref, gate_ref, ab_ref, cwq_ref, cwk_ref, cwv_ref, alog_ref, dtb_ref, ng_ref,
                o_ref, state_ref, prev_ref):
    C, dk = GDN_CHUNK, GDN_DK
    h = pl.program_id(1)

    @pl.when(pl.program_id(2) == 0)
    def _():
        state_ref[...] = jnp.zeros_like(state_ref)
        prev_ref[...] = jnp.zeros_like(prev_ref)

    def conv_silu(ref, cw_ref, slot):
        cur = ref[0]
        prev8 = prev_ref[slot]
        cw = cw_ref[...]
        acc = cur * cw[GDN_CONV - 1:GDN_CONV]
        for s in range(1, GDN_CONV):
            acc = acc + _shift_rows(cur, prev8, s) * cw[GDN_CONV - 1 - s:GDN_CONV - s]
        prev_ref[slot] = cur[C - SUBLANES:]
        return _silu(acc)

    q = conv_silu(q_ref, cwq_ref, 0)
    k = conv_silu(k_ref, cwk_ref, 1)
    v = conv_silu(v_ref, cwv_ref, 2)
    q = q * lax.rsqrt(jnp.sum(q * q, axis=-1, keepdims=True) + 1e-6) * (dk ** -0.5)
    k = k * lax.rsqrt(jnp.sum(k * k, axis=-1, keepdims=True) + 1e-6)

    ab = ab_ref[0]
    lane = lax.broadcasted_iota(jnp.int32, ab.shape, 1)
    g_all = -jnp.exp(alog_ref[...]) * _softplus(ab + dtb_ref[...])
    g = jnp.sum(jnp.where(lane == h, g_all, 0.0), axis=1, keepdims=True)
    beta = jnp.sum(jnp.where(lane == h + GDN_HEADS, _sigmoid(ab), 0.0), axis=1, keepdims=True)

    cum = _cumsum_rows(jnp.broadcast_to(g, (C, LANES)))
    cum_c = cum[:, :1]
    cum_last = cum[C - 1:C, :1]
    cum_r = cum.T[:C, :]
    ri = lax.broadcasted_iota(jnp.int32, (C, C), 0)
    ci = lax.broadcasted_iota(jnp.int32, (C, C), 1)
    incl = ri >= ci
    decay = jnp.where(incl, jnp.exp(jnp.where(incl, cum[:, :C] - cum_r, 0.0)), 0.0)
    L = jnp.where(ri > ci, _bdot_nt(k, k) * decay, 0.0) * beta
    rhs = jnp.concatenate([k * (beta * jnp.exp(cum_c)), v * beta], axis=-1)
    sol = _hdot(_unit_lower_inverse(L), rhs)
    w_c, u_c = sol[:, :dk], sol[:, dk:]
    a_qk = _bdot_nt(q, k) * decay
    q_dec = q * jnp.exp(cum_c)
    k_dec = k * jnp.exp(cum_last - cum_c)

    st = state_ref[...]
    u = u_c - _bdot(w_c, st)
    o = _bdot(q_dec, st) + _bdot(a_qk, u)
    state_ref[...] = st * jnp.exp(cum_last) + _bdot_tn(k_dec, u)

    ms = jnp.mean(o * o, axis=-1, keepdims=True)
    o = o * lax.rsqrt(ms + NORM_EPS) * ng_ref[...]
    o_ref[0] = (o * _silu(gate_ref[0])).astype(o_ref.dtype)


def _gdn_scan(proj, proj_ab, conv_w, a_log, dt_bias, norm_g):
    B, S, _ = proj.shape
    H, dk, C = GDN_HEADS, GDN_DK, GDN_CHUNK
    N = S // C
    pad = lambda t: jnp.pad(t.astype(F32), (0, LANES - H)).reshape(1, LANES)
    blk = lambda off: pl.BlockSpec((1, C, dk), lambda b, h, n: (b, n, off + h))
    cw = lambda off: pl.BlockSpec((GDN_CONV, dk), lambda b, h, n: (0, off + h))
    row = pl.BlockSpec((1, LANES), lambda b, h, n: (0, 0))
    return pl.pallas_call(
        _gdn_kernel,
        out_shape=jax.ShapeDtypeStruct((B, S, H * dk), BF16),
        grid=(B, H, N),
        in_specs=[blk(0), blk(H), blk(2 * H), blk(3 * H),
                  pl.BlockSpec((1, C, LANES), lambda b, h, n: (b, n, 0)),
                  cw(0), cw(H), cw(2 * H), row, row, row],
        out_specs=pl.BlockSpec((1, C, dk), lambda b, h, n: (b, n, h)),
        scratch_shapes=[pltpu.VMEM((dk, dk), F32), pltpu.VMEM((3, SUBLANES, dk), F32)],
        compiler_params=_cparams(("parallel", "parallel", "arbitrary")),
        name="gdn_scan",
    )(proj, proj, proj, proj, proj_ab, conv_w, conv_w, conv_w, pad(a_log), pad(dt_bias), norm_g.reshape(1, dk))


def _rwkv_proj_kernel(x_ref, xp_ref, g_ref, sc_ref, sh_ref, mu_ref, wr_ref, wk_ref, wv_ref, w1_ref, w2_ref,
                      a1_ref, a2_ref, g1_ref, g2_ref, w0_ref, a0_ref, kk_ref, ka_ref,
                      r_o, dec_o, k_o, v_o, kk_o, a_o, g_o, *, tiles_per_seq):
    h = _norm_mod(x_ref[...], g_ref[...], sc_ref[0], sh_ref[0])
    hp8 = _norm_mod(xp_ref[...], g_ref[...], sc_ref[0], sh_ref[0])
    seq_start = pl.program_id(0) % tiles_per_seq == 0
    first = jnp.where(seq_start, 0.0, hp8[SUBLANES - 1:SUBLANES, :])
    rows = lax.broadcasted_iota(jnp.int32, h.shape, 0)
    xx = jnp.where(rows == 0, first, pltpu.roll(h, 1, axis=0)) - h
    mix = lambda j: h + xx * mu_ref[j:j + 1, :]
    r = _bdot(mix(0), wr_ref[...])
    lw = w0_ref[...] + _bdot(jnp.tanh(_bdot(mix(1), w1_ref[...])), w2_ref[...])
    k = _bdot(mix(2), wk_ref[...])
    v = _bdot(mix(3), wv_ref[...])
    a = _sigmoid(a0_ref[...] + _bdot(_bdot(mix(4), a1_ref[...]), a2_ref[...]))
    g = _bdot(_sigmoid(_bdot(mix(5), g1_ref[...])), g2_ref[...])
    w = -_softplus(-lw) - 0.5
    r_o[...] = r
    dec_o[...] = jnp.exp(-jnp.exp(w))
    k_o[...] = k * (1.0 + (a - 1.0) * ka_ref[...])
    v_o[...] = v
    kk_o[...] = k * kk_ref[...]
    a_o[...] = a
    g_o[...] = g


def _pad_cols(w, n):
    return jnp.pad(w, ((0, 0), (0, n - w.shape[1])))


def _pad_rows(w, n):
    return jnp.pad(w, ((0, n - w.shape[0]), (0, 0)))


def _rwkv_proj(x2d, g, sc, sh, mu, w_r, w_k, w_v, w1, w2, a1, a2, g1, g2, w0, a0, k_k, k_a, *, rows_per_batch, tm=256):
    m, d = x2d.shape
    nb = sc.shape[0]
    tpb = rows_per_batch // tm
    lora_w = LANES * pl.cdiv(w1.shape[1], LANES)
    lora_g = LANES * pl.cdiv(g1.shape[1], LANES)
    bf = lambda t: t.astype(BF16)
    full = lambda a: pl.BlockSpec(a.shape, lambda i: (0,) * a.ndim)
    row = lambda t: t.reshape(1, d)
    ws = [bf(w_r), bf(w_k), bf(w_v), bf(_pad_cols(w1, lora_w)), bf(_pad_rows(w2, lora_w)),
          bf(_pad_cols(a1, lora_w)), bf(_pad_rows(a2, lora_w)), bf(_pad_cols(g1, lora_g)), bf(_pad_rows(g2, lora_g)),
          row(w0), row(a0), row(k_k), row(k_a)]
    tile = pl.BlockSpec((tm, d), lambda i: (i, 0))
    return pl.pallas_call(
        functools.partial(_rwkv_proj_kernel, tiles_per_seq=tpb),
        out_shape=[jax.ShapeDtypeStruct((m, d), F32)] * 7,
        grid=(m // tm,),
        in_specs=[
            tile,
            pl.BlockSpec((SUBLANES, d), lambda i: (jnp.maximum(i * (tm // SUBLANES) - 1, 0), 0)),
            pl.BlockSpec((1, d), lambda i: (0, 0)),
            pl.BlockSpec((1, 1, d), lambda i: (i // tpb, 0, 0)),
            pl.BlockSpec((1, 1, d), lambda i: (i // tpb, 0, 0)),
            full(mu),
        ] + [full(w) for w in ws],
        out_specs=[tile] * 7,
        compiler_params=_cparams(("parallel",)),
        name="rwkv_proj",
    )(x2d, x2d, g.reshape(1, d), sc.reshape(nb, 1, d), sh.reshape(nb, 1, d), mu, *ws)


def _rwkv_scan_kernel(r_ref, w_ref, k_ref, v_ref, kk_ref, a_ref, rk_ref, lng_ref, lnb_ref, y_ref, state_ref, yrow_ref):
    n = RWKV_HEAD

    @pl.when(pl.program_id(0) == 0)
    def _():
        state_ref[...] = jnp.zeros_like(state_ref)

    def step(t, carry):
        r, w, k, kkr, a = r_ref[t], w_ref[t], k_ref[t], kk_ref[t], a_ref[t]
        kk = kkr * lax.rsqrt(jnp.sum(kkr * kkr, axis=0, keepdims=True) + 1e-6)
        nkk = -kk
        kka = kk * a

        def vloop(vi, c):
            sv = state_ref[vi]
            sa = jnp.sum(sv * nkk, axis=0, keepdims=True)
            vrow = v_ref[t, pl.ds(vi, 1), :]
            sn = sv * w + sa * kka + vrow * k
            state_ref[vi] = sn
            yrow_ref[pl.ds(vi, 1), :] = jnp.sum(sn * r, axis=0, keepdims=True)
            return c

        lax.fori_loop(0, n, vloop, 0)
        y = yrow_ref[...]
        mu = jnp.mean(y, axis=0, keepdims=True)
        var = jnp.mean(jnp.square(y - mu), axis=0, keepdims=True)
        yn = (y - mu) * lax.rsqrt(var + RWKV_GN_EPS) * lng_ref[...] + lnb_ref[...]
        bonus = jnp.sum(r * k * rk_ref[...], axis=0, keepdims=True)
        y_ref[t] = yn + bonus * v_ref[t]
        return carry

    lax.fori_loop(0, r_ref.shape[0], step, 0)


def _rwkv_scan(r, dec, k, v, kk, a, r_k, ln_g, ln_b, *, tc=32):
    B, S, D = r.shape
    H, n = RWKV_HEADS, RWKV_HEAD
    lanes = B * H
    to_scan = lambda t: jnp.transpose(t.reshape(B, S, H, n), (1, 3, 0, 2)).reshape(S, n, lanes)
    per_head = lambda p: jnp.tile(p.T, (1, B))
    blk = pl.BlockSpec((tc, n, lanes), lambda i: (i, 0, 0))
    cst = pl.BlockSpec((n, lanes), lambda i: (0, 0))
    y = pl.pallas_call(
        _rwkv_scan_kernel,
        out_shape=jax.ShapeDtypeStruct((S, n, lanes), F32),
        grid=(S // tc,),
        in_specs=[blk] * 6 + [cst] * 3,
        out_specs=blk,
        scratch_shapes=[pltpu.VMEM((n, n, lanes), F32), pltpu.VMEM((n, lanes), F32)],
        compiler_params=_cparams(("arbitrary",)),
        name="rwkv_scan",
    )(to_scan(r), to_scan(dec), to_scan(k), to_scan(v), to_scan(kk), to_scan(a),
      per_head(r_k), per_head(ln_g), per_head(ln_b))
    return jnp.transpose(y.reshape(S, n, B, H), (2, 0, 3, 1)).reshape(B, S, D)


def _rwkv_mixer(x, g, sc, sh, gt, mu, w_r, w_k, w_v, w_o, w0, w1, w2, a0, a1, a2, g1, g2, k_k, k_a, r_k, ln_g, ln_b):
    B, S, D = x.shape
    x2d = x.reshape(B * S, D)
    tm = min(256, S)
    r, dec, k, v, kk, a, gg = _rwkv_proj(x2d, g, sc, sh, mu, w_r, w_k, w_v, w1, w2, a1, a2, g1, g2, w0, a0, k_k, k_a,
                                         rows_per_batch=S, tm=tm)
    sh3 = lambda t: t.reshape(B, S, D)
    y = _rwkv_scan(sh3(r), sh3(dec), sh3(k), sh3(v), sh3(kk), sh3(a), r_k, ln_g, ln_b, tc=min(32, S))
    out = _mm_res(y.reshape(B * S, D), w_o.astype(BF16), x2d, gt, rows_per_batch=S, mul=gg, tm=min(512, S))
    return out.reshape(B, S, D)


def _rwkv_mixer_test(x, g, sc, sh, p):
    gt = jnp.ones_like(sc)
    return _rwkv_mixer(x, g, sc, sh, gt, p['mu'], p['w_r'], p['w_k'], p['w_v'], p['w_o'], p['w0'], p['w1'], p['w2'],
                       p['a0'], p['a1'], p['a2'], p['g1'], p['g2'], p['k_k'], p['k_a'], p['r_k'], p['ln_g'],
                       p['ln_b']) - x


def _topk_rows(s, k):
    n = s.shape[0]
    rows = lax.broadcasted_iota(jnp.int32, s.shape, 0)
    vals, ids = [], []
    for _ in range(k):
        m = jnp.max(s, axis=0, keepdims=True)
        idx = jnp.min(jnp.where(s == m, rows, n), axis=0, keepdims=True)
        vals.append(m)
        ids.append(idx)
        s = jnp.where(rows == idx, -jnp.inf, s)
    return jnp.concatenate(vals, axis=0), jnp.concatenate(ids, axis=0)


def _take_rows(table, pos):
    out = jnp.zeros(pos.shape, table.dtype)
    for m in range(table.shape[0]):
        out = jnp.where(pos == m, table[m:m + 1, :], out)
    return out


def _peer_route_kernel(q_ref, keys_ref, idx_o, gate_o):
    K, half = PEER_TOPK, PEER_DQ // 2

    def head(h, carry):
        tops = []
        for p in range(2):
            c = pl.multiple_of((2 * h + p) * half, half)
            s = _bdot_nt(keys_ref[h, p], q_ref[:, pl.ds(c, half)])
            tops.append(_topk_rows(s, K))
        (va, ia), (vb, ib) = tops
        cand = jnp.concatenate([va[m:m + 1, :] + vb for m in range(K)], axis=0)
        best, pos = _topk_rows(cand, K)
        expert = _take_rows(ia, pos // K) * PEER_KEYS + _take_rows(ib, pos % K)
        e = jnp.exp(best - best[0:1, :])
        r0 = pl.multiple_of(h * K, K)
        idx_o[pl.ds(r0, K), :] = expert
        gate_o[pl.ds(r0, K), :] = e / jnp.sum(e, axis=0, keepdims=True)
        return carry

    lax.fori_loop(0, PEER_HEADS, head, 0)


def _peer_route(q, sub_keys, *, tm=256):
    t, n = q.shape
    blk = pl.BlockSpec((PEER_SEL, tm), lambda i: (0, i))
    return pl.pallas_call(
        _peer_route_kernel,
        out_shape=[jax.ShapeDtypeStruct((PEER_SEL, t), jnp.int32), jax.ShapeDtypeStruct((PEER_SEL, t), F32)],
        grid=(t // tm,),
        in_specs=[pl.BlockSpec((tm, n), lambda i: (i, 0)),
                  pl.BlockSpec(sub_keys.shape, lambda i: (0, 0, 0, 0))],
        out_specs=[blk, blk],
        compiler_params=_cparams(("parallel",)),
        name="peer_route",
    )(q, sub_keys.astype(BF16))


def _pack_uv(u, v):
    e, d = u.shape
    ub = lax.bitcast_convert_type(u.astype(BF16), jnp.uint16).astype(jnp.uint32)
    vb = lax.bitcast_convert_type(v.astype(BF16), jnp.uint16).astype(jnp.uint32)
    return lax.bitcast_convert_type((vb << 16) | ub, jnp.int32).reshape(e, d // LANES, LANES)


PEER_NBUF = 4


def _gelu(x):
    return 0.5 * x * (1.0 + lax.erf(x * (2.0 ** -0.5)))


def _peer_expert_kernel(idx_hbm, gate_ref, h_ref, xres_ref, gt_ref, uv_hbm, o_ref, idx_smem, buf, sem_idx, sem):
    tb = h_ref.shape[0]
    nsel = PEER_SEL
    base = pl.program_id(0) * (tb * nsel)
    cp = pltpu.make_async_copy(idx_hbm.at[pl.ds(base, tb * nsel)], idx_smem, sem_idx)
    cp.start()
    cp.wait()

    def issue(t, slot):
        for k in range(nsel):
            e = idx_smem[t * nsel + k]
            pltpu.make_async_copy(uv_hbm.at[e], buf.at[slot, k], sem.at[slot]).start()

    def wait(slot):
        pltpu.make_async_copy(uv_hbm.at[pl.ds(0, nsel)], buf.at[slot], sem.at[slot]).wait()

    for t0 in range(PEER_NBUF - 1):
        issue(t0, t0)

    lane = lax.broadcasted_iota(jnp.int32, gate_ref.shape, 1)

    def body(t, carry):
        slot = t % PEER_NBUF
        nxt = t + PEER_NBUF - 1

        @pl.when(nxt < tb)
        def _():
            issue(nxt, nxt % PEER_NBUF)

        wait(slot)
        packed = buf[slot]
        u = lax.bitcast_convert_type(packed << 16, F32)
        v = lax.bitcast_convert_type(packed & jnp.int32(-65536), F32)
        x = h_ref[t]
        act = jnp.sum(jnp.sum(u * x[None], axis=1), axis=1, keepdims=True)
        gate = jnp.sum(jnp.where(lane == t, gate_ref[...], 0.0), axis=1, keepdims=True)
        wgt = gate * _gelu(act)
        out = jnp.sum(v * wgt[:, :, None], axis=0)
        o_ref[t] = xres_ref[t] + gt_ref[0] * out
        return carry

    lax.fori_loop(0, tb, body, 0)


def _peer_expert(idx_flat, gate_t, h, xres, gt, uv, *, rows_per_batch, tb=128):
    t, d = h.shape
    nb = gt.shape[0]
    c = d // LANES
    tpb = rows_per_batch // tb
    tok = pl.BlockSpec((tb, c, LANES), lambda i: (i, 0, 0))
    out = pl.pallas_call(
        _peer_expert_kernel,
        out_shape=jax.ShapeDtypeStruct((t, c, LANES), F32),
        grid=(t // tb,),
        in_specs=[
            pl.BlockSpec(memory_space=pl.ANY),
            pl.BlockSpec((PEER_SEL, tb), lambda i: (0, i)),
            tok, tok,
            pl.BlockSpec((1, c, LANES), lambda i: (i // tpb, 0, 0)),
            pl.BlockSpec(memory_space=pl.ANY),
        ],
        out_specs=tok,
        scratch_shapes=[
            pltpu.SMEM((tb * PEER_SEL,), jnp.int32),
            pltpu.VMEM((PEER_NBUF, PEER_SEL, c, LANES), jnp.int32),
            pltpu.SemaphoreType.DMA,
            pltpu.SemaphoreType.DMA((PEER_NBUF,)),
        ],
        compiler_params=_cparams(("arbitrary",)),
        name="peer_expert",
    )(idx_flat, gate_t, h.reshape(t, c, LANES), xres.reshape(t, c, LANES), gt.reshape(nb, c, LANES), uv)
    return out.reshape(t, d)


def _peer_ffn(x2d, g, sc, sh, gt, w_q, sub_keys, uv, *, rows_per_batch):
    tm = min(512, rows_per_batch)
    q, h = _nm_matmul(x2d, g, sc, sh, w_q, rows_per_batch=rows_per_batch, tm=tm, emit_h=True)
    idx_t, gate_t = _peer_route(q, sub_keys, tm=min(256, rows_per_batch))
    return _peer_expert(idx_t.T.reshape(-1), gate_t, h, x2d, gt, uv, rows_per_batch=rows_per_batch,
                        tb=min(128, rows_per_batch))


def _peer_test(x, g, sc, sh, w_q, sub_keys, u, v):
    B, S, D = x.shape
    x2d = x.reshape(B * S, D)
    return (_peer_ffn(x2d, g, sc, sh, jnp.ones_like(sc), w_q.astype(BF16), sub_keys, _pack_uv(u, v),
                      rows_per_batch=S) - x2d).reshape(B, S, D)


def _final_norm_kernel(x_ref, g_ref, o_ref):
    x = x_ref[...]
    ms = jnp.mean(x * x, axis=-1, keepdims=True)
    o_ref[...] = x * lax.rsqrt(ms + NORM_EPS) * g_ref[...]


def _final_norm(x2d, g, *, tm=512):
    m, d = x2d.shape
    tile = pl.BlockSpec((tm, d), lambda i: (i, 0))
    return pl.pallas_call(
        _final_norm_kernel,
        out_shape=jax.ShapeDtypeStruct((m, d), F32),
        grid=(m // tm,),
        in_specs=[tile, pl.BlockSpec((1, d), lambda i: (0, 0))],
        out_specs=tile,
        compiler_params=_cparams(("parallel",)),
        name="final_norm",
    )(x2d, g.reshape(1, d))


def kernel(x, c, ada_w, ada_b, norm_mix_g, norm_ffn_g, final_norm_g, ret_w_in, ret_w_out, ret_gn_g, ret_gn_b, gdn_w_in, gdn_conv_w, gdn_a_log, gdn_dt_bias, gdn_norm_g, gdn_w_out, rwkv_mu, rwkv_w_r, rwkv_w_k, rwkv_w_v, rwkv_w_o, rwkv_w0, rwkv_w1, rwkv_w2, rwkv_a0, rwkv_a1, rwkv_a2, rwkv_g1, rwkv_g2, rwkv_k_k, rwkv_k_a, rwkv_r_k, rwkv_ln_g, rwkv_ln_b, peer_w_q, peer_sub_keys, peer_u, peer_v):
    B, S, D = x.shape
    T = B * S
    depth = ada_w.shape[0]
    bf = lambda t: t.astype(BF16)
    mod = _adaln(c, ada_w, ada_b)
    x2d = x.reshape(T, D)
    for layer in range(depth):
        sh_m, sc_m, gt_m, sh_f, sc_f, gt_f = [mod[layer, :, i * D:(i + 1) * D] for i in range(6)]
        g_mix = norm_mix_g[layer]
        kind, j = layer % 3, layer // 3
        if kind == 0:
            proj = _nm_matmul(x2d, g_mix, sc_m, sh_m, bf(ret_w_in[j]), rows_per_batch=S, tn=2048)
            o = _retention_scan(proj.reshape(B, S, -1), ret_gn_g[j], ret_gn_b[j])
            x2d = _mm_res(o.reshape(T, -1), bf(ret_w_out[j]), x2d, gt_m, rows_per_batch=S)
        elif kind == 1:
            w = gdn_w_in[j]
            wide = GDN_QKV + GDN_HEADS * GDN_DV
            proj = _nm_matmul(x2d, g_mix, sc_m, sh_m, bf(w[:, :wide]), rows_per_batch=S, tn=2048)
            proj_ab = _nm_matmul(x2d, g_mix, sc_m, sh_m, bf(_pad_cols(w[:, wide:], LANES)), rows_per_batch=S)
            o = _gdn_scan(proj.reshape(B, S, -1), proj_ab.reshape(B, S, -1), gdn_conv_w[j], gdn_a_log[j],
                          gdn_dt_bias[j], gdn_norm_g[j])
            x2d = _mm_res(o.reshape(T, -1), bf(gdn_w_out[j]), x2d, gt_m, rows_per_batch=S)
        else:
            x2d = _rwkv_mixer(x2d.reshape(B, S, D), g_mix, sc_m, sh_m, gt_m, rwkv_mu[j], rwkv_w_r[j], rwkv_w_k[j],
                              rwkv_w_v[j], rwkv_w_o[j], rwkv_w0[j], rwkv_w1[j], rwkv_w2[j], rwkv_a0[j], rwkv_a1[j],
                              rwkv_a2[j], rwkv_g1[j], rwkv_g2[j], rwkv_k_k[j], rwkv_k_a[j], rwkv_r_k[j],
                              rwkv_ln_g[j], rwkv_ln_b[j]).reshape(T, D)
        x2d = _peer_ffn(x2d, norm_ffn_g[layer], sc_f, sh_f, gt_f, bf(peer_w_q[layer]), peer_sub_keys[layer],
                        _pack_uv(peer_u[layer], peer_v[layer]), rows_per_batch=S)
    return _final_norm(x2d, final_norm_g).reshape(B, S, D)
```

```python
import functools
import math

import jax
import jax.numpy as jnp
from jax import lax
from jax.experimental import pallas as pl
from jax.experimental.pallas import tpu as pltpu

F32 = jnp.float32
BF16 = jnp.bfloat16
HIGHEST = lax.Precision.HIGHEST

D_MODEL = 1024
NORM_EPS = 1e-6

RET_HEADS = 4
RET_DK = D_MODEL // RET_HEADS
RET_DV = 2 * D_MODEL // RET_HEADS
RET_CHUNK = 128
RET_ROPE_BASE = 10000.0
RET_GN_EPS = 1e-5

GDN_HEADS = 8
GDN_DK = D_MODEL // GDN_HEADS
GDN_DV = D_MODEL // GDN_HEADS
GDN_CONV = 4
GDN_CHUNK = 64
GDN_QKV = GDN_HEADS * (2 * GDN_DK + GDN_DV)

RWKV_HEAD = 64
RWKV_HEADS = D_MODEL // RWKV_HEAD
RWKV_GN_EPS = 64e-5

PEER_KEYS = 128
PEER_HEADS = 8
PEER_DQ = 256
PEER_TOPK = 16
PEER_SEL = PEER_HEADS * PEER_TOPK

LANES = 128
SUBLANES = 8
VMEM_LIMIT = 56 * 1024 * 1024


def _cparams(sem):
    return pltpu.CompilerParams(dimension_semantics=sem, vmem_limit_bytes=VMEM_LIMIT)


def _bdot(a, b):
    return jnp.dot(a.astype(BF16), b.astype(BF16), preferred_element_type=F32)


def _bdot_nt(a, b):
    return lax.dot_general(a.astype(BF16), b.astype(BF16), (((1,), (1,)), ((), ())),
                           preferred_element_type=F32)


def _bdot_tn(a, b):
    return lax.dot_general(a.astype(BF16), b.astype(BF16), (((0,), (0,)), ((), ())),
                           preferred_element_type=F32)


def _hdot(a, b):
    return jnp.dot(a, b, preferred_element_type=F32, precision=HIGHEST)


def _sigmoid(x):
    return 1.0 / (1.0 + jnp.exp(-x))


def _silu(x):
    return x * _sigmoid(x)


def _softplus(x):
    return jnp.maximum(x, 0.0) + jnp.log1p(jnp.exp(-jnp.abs(x)))


def _norm_mod(x, g, sc, sh):
    ms = jnp.mean(x * x, axis=-1, keepdims=True)
    return (x * lax.rsqrt(ms + NORM_EPS) * g) * (1.0 + sc) + sh


def _adaln_kernel(c_ref, w_ref, b_ref, o_ref):
    cond = _silu(c_ref[...])
    o_ref[0] = _bdot(cond, w_ref[0]) + b_ref[0]


def _adaln(c, ada_w, ada_b):
    depth, d, n = ada_w.shape
    b = c.shape[0]
    tn = 1024
    return pl.pallas_call(
        _adaln_kernel,
        out_shape=jax.ShapeDtypeStruct((depth, b, n), F32),
        grid=(depth, n // tn),
        in_specs=[
            pl.BlockSpec((b, d), lambda l, j: (0, 0)),
            pl.BlockSpec((1, d, tn), lambda l, j: (l, 0, j)),
            pl.BlockSpec((1, 1, tn), lambda l, j: (l, 0, j)),
        ],
        out_specs=pl.BlockSpec((1, b, tn), lambda l, j: (l, 0, j)),
        compiler_params=_cparams(("parallel", "parallel")),
        name="adaln",
    )(c, ada_w, ada_b.reshape(depth, 1, n))


def _nm_matmul_kernel(x_ref, g_ref, sc_ref, sh_ref, w_ref, o_ref, *h_ref):
    h = _norm_mod(x_ref[...], g_ref[...], sc_ref[0], sh_ref[0])
    o_ref[...] = jnp.dot(h.astype(BF16), w_ref[...], preferred_element_type=F32).astype(o_ref.dtype)
    if h_ref:
        h_ref[0][...] = h


def _nm_matmul(x2d, g, sc, sh, w, *, rows_per_batch, tm=512, tn=None, emit_h=False):
    m, d = x2d.shape
    n = w.shape[1]
    nb = sc.shape[0]
    if tn is None:
        tn = n
    tpb = rows_per_batch // tm
    out_shape = [jax.ShapeDtypeStruct((m, n), F32)]
    out_specs = [pl.BlockSpec((tm, tn), lambda j, i: (i, j))]
    if emit_h:
        assert tn == n
        out_shape.append(jax.ShapeDtypeStruct((m, d), F32))
        out_specs.append(pl.BlockSpec((tm, d), lambda j, i: (i, 0)))
    res = pl.pallas_call(
        _nm_matmul_kernel,
        out_shape=out_shape,
        grid=(n // tn, m // tm),
        in_specs=[
            pl.BlockSpec((tm, d), lambda j, i: (i, 0)),
            pl.BlockSpec((1, d), lambda j, i: (0, 0)),
            pl.BlockSpec((1, 1, d), lambda j, i: (i // tpb, 0, 0)),
            pl.BlockSpec((1, 1, d), lambda j, i: (i // tpb, 0, 0)),
            pl.BlockSpec((d, tn), lambda j, i: (0, j)),
        ],
        out_specs=out_specs,
        compiler_params=_cparams(("parallel", "parallel")),
        name="norm_mod_matmul",
    )(x2d, g.reshape(1, d), sc.reshape(nb, 1, d), sh.reshape(nb, 1, d), w)
    return res if emit_h else res[0]


def _mm_res_kernel(*refs, has_mul):
    if has_mul:
        a_ref, m_ref, w_ref, r_ref, gt_ref, o_ref = refs
        a = a_ref[...] * m_ref[...]
    else:
        a_ref, w_ref, r_ref, gt_ref, o_ref = refs
        a = a_ref[...]
    y = jnp.dot(a.astype(BF16), w_ref[...], preferred_element_type=F32)
    o_ref[...] = r_ref[...] + gt_ref[0] * y


def _mm_res(a, w, res, gt, *, rows_per_batch, mul=None, tm=512):
    m, k = a.shape
    n = w.shape[1]
    nb = gt.shape[0]
    tpb = rows_per_batch // tm
    ins = [a]
    specs = [pl.BlockSpec((tm, k), lambda i: (i, 0))]
    if mul is not None:
        ins.append(mul)
        specs.append(pl.BlockSpec((tm, k), lambda i: (i, 0)))
    ins += [w, res, gt.reshape(nb, 1, n)]
    specs += [
        pl.BlockSpec((k, n), lambda i: (0, 0)),
        pl.BlockSpec((tm, n), lambda i: (i, 0)),
        pl.BlockSpec((1, 1, n), lambda i: (i // tpb, 0, 0)),
    ]
    return pl.pallas_call(
        functools.partial(_mm_res_kernel, has_mul=mul is not None),
        out_shape=jax.ShapeDtypeStruct((m, n), F32),
        grid=(m // tm,),
        in_specs=specs,
        out_specs=pl.BlockSpec((tm, n), lambda i: (i, 0)),
        compiler_params=_cparams(("parallel",)),
        name="matmul_residual",
    )(*ins)


def _ret_kernel(q_ref, k_ref, v_ref, gate_ref, cos_ref, sin_ref, dintra_ref, dq_ref, dk_ref, dchunk_ref,
                gng_ref, gnb_ref, o_ref, state_ref):
    H, dk, dv = RET_HEADS, RET_DK, RET_DV
    half = dk // 2

    @pl.when(pl.program_id(1) == 0)
    def _():
        state_ref[...] = jnp.zeros_like(state_ref)

    cos = cos_ref[...]
    sin = sin_ref[...]

    def rot(ref, h):
        x1 = ref[0, :, h * dk:h * dk + half]
        x2 = ref[0, :, h * dk + half:(h + 1) * dk]
        return jnp.concatenate([x1 * cos - x2 * sin, x1 * sin + x2 * cos], axis=-1)

    for h in range(H):
        q = rot(q_ref, h)
        k = rot(k_ref, h) * (dk ** -0.5)
        v = v_ref[0, :, h * dv:(h + 1) * dv]
        scores = _bdot_nt(q, k) * dintra_ref[h]
        st = state_ref[h]
        o = _bdot(scores, v) + _bdot(q, st) * dq_ref[h]
        state_ref[h] = st * dchunk_ref[h] + _bdot_tn(k * dk_ref[h], v)
        mu = jnp.mean(o, axis=-1, keepdims=True)
        var = jnp.mean(jnp.square(o - mu), axis=-1, keepdims=True)
        on = (o - mu) * lax.rsqrt(var + RET_GN_EPS) * gng_ref[h] + gnb_ref[h]
        g = gate_ref[0, :, h * dv:(h + 1) * dv]
        o_ref[0, :, h * dv:(h + 1) * dv] = (_silu(g) * on).astype(o_ref.dtype)


def _retention_scan(proj, gn_g, gn_b):
    B, S, _ = proj.shape
    H, dk, dv, C = RET_HEADS, RET_DK, RET_DV, RET_CHUNK
    half = dk // 2
    N = S // C
    inv_freq = RET_ROPE_BASE ** (-jnp.arange(half, dtype=F32) / half)
    ang = jnp.arange(S, dtype=F32)[:, None] * inv_freq[None, :]
    cos, sin = jnp.cos(ang), jnp.sin(ang)
    log_gamma = jnp.log1p(-jnp.exp2(-5.0 - jnp.arange(H, dtype=F32)))
    idx = jnp.arange(C, dtype=F32)
    diff = idx[:, None] - idx[None, :]
    causal = diff >= 0
    d_intra = jnp.where(causal[None], jnp.exp(jnp.where(causal, diff, 0.0)[None] * log_gamma[:, None, None]), 0.0)
    d_q = jnp.exp((idx[None, :] + 1.0) * log_gamma[:, None])[:, :, None]
    d_k = jnp.exp((C - 1.0 - idx)[None, :] * log_gamma[:, None])[:, :, None]
    d_chunk = jnp.exp(C * log_gamma)[:, None, None]
    qw, vw = H * dk, H * dv
    return pl.pallas_call(
        _ret_kernel,
        out_shape=jax.ShapeDtypeStruct((B, S, vw), BF16),
        grid=(B, N),
        in_specs=[
            pl.BlockSpec((1, C, qw), lambda b, n: (b, n, 0)),
            pl.BlockSpec((1, C, qw), lambda b, n: (b, n, 1)),
            pl.BlockSpec((1, C, vw), lambda b, n: (b, n, 1)),
            pl.BlockSpec((1, C, vw), lambda b, n: (b, n, 2)),
            pl.BlockSpec((C, half), lambda b, n: (n, 0)),
            pl.BlockSpec((C, half), lambda b, n: (n, 0)),
            pl.BlockSpec((H, C, C), lambda b, n: (0, 0, 0)),
            pl.BlockSpec((H, C, 1), lambda b, n: (0, 0, 0)),
            pl.BlockSpec((H, C, 1), lambda b, n: (0, 0, 0)),
            pl.BlockSpec((H, 1, 1), lambda b, n: (0, 0, 0)),
            pl.BlockSpec((H, 1, dv), lambda b, n: (0, 0, 0)),
            pl.BlockSpec((H, 1, dv), lambda b, n: (0, 0, 0)),
        ],
        out_specs=pl.BlockSpec((1, C, vw), lambda b, n: (b, n, 0)),
        scratch_shapes=[pltpu.VMEM((H, dk, dv), F32)],
        compiler_params=_cparams(("parallel", "arbitrary")),
        name="retention_scan",
    )(proj, proj, proj, proj, cos, sin, d_intra, d_q, d_k, d_chunk,
      gn_g.reshape(H, 1, dv), gn_b.reshape(H, 1, dv))


def _shift_rows(cur, prev8, s):
    rows = lax.broadcasted_iota(jnp.int32, cur.shape, 0)
    rolled = pltpu.roll(cur, s, axis=0)
    head = pltpu.roll(prev8, s, axis=0)
    head = jnp.concatenate([head, jnp.zeros((cur.shape[0] - SUBLANES, cur.shape[1]), cur.dtype)], axis=0)
    return jnp.where(rows < s, head, rolled)


def _cumsum_rows(x):
    rows = lax.broadcasted_iota(jnp.int32, x.shape, 0)
    s = 1
    while s < x.shape[0]:
        x = x + jnp.where(rows >= s, pltpu.roll(x, s, axis=0), 0.0)
        s *= 2
    return x


def _unit_lower_inverse(L):
    C = L.shape[0]
    eye = (lax.broadcasted_iota(jnp.int32, (C, C), 0) == lax.broadcasted_iota(jnp.int32, (C, C), 1)).astype(F32)
    p = -L
    inv = eye + p
    s = 2
    while s < C:
        p = _hdot(p, p)
        inv = inv + _hdot(inv, p)
        s *= 2
    return inv


def _gdn_kernel(qkv_ref, gate_ref, ab_ref, cw_ref, alog_ref, dtb_ref, ng_ref, o_ref, state_ref, prev_ref):
    C, dk, H = GDN_CHUNK, GDN_DK, GDN_HEADS

    @pl.when(pl.program_id(1) == 0)
    def _():
        state_ref[...] = jnp.zeros_like(state_ref)
        prev_ref[...] = jnp.zeros_like(prev_ref)

    def conv_silu(col):
        cur = qkv_ref[0, :, col * dk:(col + 1) * dk]
        prev8 = prev_ref[:, col * dk:(col + 1) * dk]
        cw = cw_ref[:, col * dk:(col + 1) * dk]
        acc = cur * cw[GDN_CONV - 1:GDN_CONV]
        for s in range(1, GDN_CONV):
            acc = acc + _shift_rows(cur, prev8, s) * cw[GDN_CONV - 1 - s:GDN_CONV - s]
        prev_ref[:, col * dk:(col + 1) * dk] = cur[C - SUBLANES:]
        return _silu(acc)

    ab = ab_ref[0]
    g_all = -jnp.exp(alog_ref[...]) * _softplus(ab + dtb_ref[...])
    beta_all = _sigmoid(ab)
    ri = lax.broadcasted_iota(jnp.int32, (C, C), 0)
    ci = lax.broadcasted_iota(jnp.int32, (C, C), 1)
    incl = ri >= ci

    for h in range(H):
        q = conv_silu(h)
        k = conv_silu(H + h)
        v = conv_silu(2 * H + h)
        q = q * lax.rsqrt(jnp.sum(q * q, axis=-1, keepdims=True) + 1e-6) * (dk ** -0.5)
        k = k * lax.rsqrt(jnp.sum(k * k, axis=-1, keepdims=True) + 1e-6)
        g = g_all[:, h:h + 1]
        beta = beta_all[:, H + h:H + h + 1]

        cum = _cumsum_rows(jnp.broadcast_to(g, (C, LANES)))
        cum_c = cum[:, :1]
        cum_last = cum[C - 1:C, :1]
        cum_r = cum.T[:C, :]
        decay = jnp.where(incl, jnp.exp(jnp.where(incl, cum[:, :C] - cum_r, 0.0)), 0.0)
        L = jnp.where(ri > ci, _bdot_nt(k, k) * decay, 0.0) * beta
        rhs = jnp.concatenate([k * (beta * jnp.exp(cum_c)), v * beta], axis=-1)
        sol = _hdot(_unit_lower_inverse(L), rhs)
        w_c, u_c = sol[:, :dk], sol[:, dk:]
        a_qk = _bdot_nt(q, k) * decay
        q_dec = q * jnp.exp(cum_c)
        k_dec = k * jnp.exp(cum_last - cum_c)

        st = state_ref[h]
        u = u_c - _bdot(w_c, st)
        o = _bdot(q_dec, st) + _bdot(a_qk, u)
        state_ref[h] = st * jnp.exp(cum_last) + _bdot_tn(k_dec, u)

        ms = jnp.mean(o * o, axis=-1, keepdims=True)
        o = o * lax.rsqrt(ms + NORM_EPS) * ng_ref[...]
        o_ref[0, :, h * dk:(h + 1) * dk] = (o * _silu(gate_ref[0, :, h * dk:(h + 1) * dk])).astype(o_ref.dtype)


def _gdn_scan(proj, proj_ab, conv_w, a_log, dt_bias, norm_g):
    B, S, _ = proj.shape
    H, dk, C = GDN_HEADS, GDN_DK, GDN_CHUNK
    N = S // C
    pad = lambda t: jnp.pad(t.astype(F32), (0, LANES - H)).reshape(1, LANES)
    row = pl.BlockSpec((1, LANES), lambda b, n: (0, 0))
    return pl.pallas_call(
        _gdn_kernel,
        out_shape=jax.ShapeDtypeStruct((B, S, H * dk), BF16),
        grid=(B, N),
        in_specs=[pl.BlockSpec((1, C, 3 * H * dk), lambda b, n: (b, n, 0)),
                  pl.BlockSpec((1, C, H * dk), lambda b, n: (b, n, 3)),
                  pl.BlockSpec((1, C, LANES), lambda b, n: (b, n, 0)),
                  pl.BlockSpec((GDN_CONV, 3 * H * dk), lambda b, n: (0, 0)),
                  row, row, row],
        out_specs=pl.BlockSpec((1, C, H * dk), lambda b, n: (b, n, 0)),
        scratch_shapes=[pltpu.VMEM((H, dk, dk), F32), pltpu.VMEM((SUBLANES, 3 * H * dk), F32)],
        compiler_params=_cparams(("parallel", "arbitrary")),
        name="gdn_scan",
    )(proj, proj, proj_ab, conv_w, pad(a_log), pad(dt_bias), norm_g.reshape(1, dk))


def _rwkv_proj_kernel(x_ref, xp_ref, g_ref, sc_ref, sh_ref, mu_ref, wr_ref, wk_ref, wv_ref, w1_ref, w2_ref,
                      a1_ref, a2_ref, g1_ref, g2_ref, w0_ref, a0_ref, kk_ref, ka_ref,
                      r_o, dec_o, k_o, v_o, kk_o, a_o, g_o, *, tiles_per_seq):
    h = _norm_mod(x_ref[...], g_ref[...], sc_ref[0], sh_ref[0])
    hp8 = _norm_mod(xp_ref[...], g_ref[...], sc_ref[0], sh_ref[0])
    seq_start = pl.program_id(0) % tiles_per_seq == 0
    first = jnp.where(seq_start, 0.0, hp8[SUBLANES - 1:SUBLANES, :])
    rows = lax.broadcasted_iota(jnp.int32, h.shape, 0)
    xx = jnp.where(rows == 0, first, pltpu.roll(h, 1, axis=0)) - h
    mix = lambda j: h + xx * mu_ref[j:j + 1, :]
    r = _bdot(mix(0), wr_ref[...])
    lw = w0_ref[...] + _bdot(jnp.tanh(_bdot(mix(1), w1_ref[...])), w2_ref[...])
    k = _bdot(mix(2), wk_ref[...])
    v = _bdot(mix(3), wv_ref[...])
    a = _sigmoid(a0_ref[...] + _bdot(_bdot(mix(4), a1_ref[...]), a2_ref[...]))
    g = _bdot(_sigmoid(_bdot(mix(5), g1_ref[...])), g2_ref[...])
    w = -_softplus(-lw) - 0.5
    r_o[...] = r
    dec_o[...] = jnp.exp(-jnp.exp(w))
    k_o[...] = k * (1.0 + (a - 1.0) * ka_ref[...])
    v_o[...] = v
    kk_o[...] = k * kk_ref[...]
    a_o[...] = a
    g_o[...] = g


def _pad_cols(w, n):
    return jnp.pad(w, ((0, 0), (0, n - w.shape[1])))


def _pad_rows(w, n):
    return jnp.pad(w, ((0, n - w.shape[0]), (0, 0)))


def _rwkv_proj(x2d, g, sc, sh, mu, w_r, w_k, w_v, w1, w2, a1, a2, g1, g2, w0, a0, k_k, k_a, *, rows_per_batch, tm=256):
    m, d = x2d.shape
    nb = sc.shape[0]
    tpb = rows_per_batch // tm
    lora_w = LANES * pl.cdiv(w1.shape[1], LANES)
    lora_g = LANES * pl.cdiv(g1.shape[1], LANES)
    bf = lambda t: t.astype(BF16)
    full = lambda a: pl.BlockSpec(a.shape, lambda i: (0,) * a.ndim)
    row = lambda t: t.reshape(1, d)
    ws = [bf(w_r), bf(w_k), bf(w_v), bf(_pad_cols(w1, lora_w)), bf(_pad_rows(w2, lora_w)),
          bf(_pad_cols(a1, lora_w)), bf(_pad_rows(a2, lora_w)), bf(_pad_cols(g1, lora_g)), bf(_pad_rows(g2, lora_g)),
          row(w0), row(a0), row(k_k), row(k_a)]
    tile = pl.BlockSpec((tm, d), lambda i: (i, 0))
    return pl.pallas_call(
        functools.partial(_rwkv_proj_kernel, tiles_per_seq=tpb),
        out_shape=[jax.ShapeDtypeStruct((m, d), F32)] * 7,
        grid=(m // tm,),
        in_specs=[
            tile,
            pl.BlockSpec((SUBLANES, d), lambda i: (jnp.maximum(i * (tm // SUBLANES) - 1, 0), 0)),
            pl.BlockSpec((1, d), lambda i: (0, 0)),
            pl.BlockSpec((1, 1, d), lambda i: (i // tpb, 0, 0)),
            pl.BlockSpec((1, 1, d), lambda i: (i // tpb, 0, 0)),
            full(mu),
        ] + [full(w) for w in ws],
        out_specs=[tile] * 7,
        compiler_params=_cparams(("parallel",)),
        name="rwkv_proj",
    )(x2d, x2d, g.reshape(1, d), sc.reshape(nb, 1, d), sh.reshape(nb, 1, d), mu, *ws)


RWKV_VUNROLL = 4


def _rwkv_scan_kernel(r_ref, w_ref, k_ref, v_ref, kk_ref, a_ref, rk_ref, lng_ref, lnb_ref, y_ref, state_ref, yrow_ref):
    n = RWKV_HEAD

    @pl.when(pl.program_id(0) == 0)
    def _():
        state_ref[...] = jnp.zeros_like(state_ref)

    def step(t, carry):
        r, w, k, kkr, a = r_ref[t], w_ref[t], k_ref[t], kk_ref[t], a_ref[t]
        kk = kkr * lax.rsqrt(jnp.sum(kkr * kkr, axis=0, keepdims=True) + 1e-6)
        nkk = -kk
        kka = kk * a

        def vloop(vb, c):
            for j in range(RWKV_VUNROLL):
                vi = vb * RWKV_VUNROLL + j
                sv = state_ref[vi]
                sa = jnp.sum(sv * nkk, axis=0, keepdims=True)
                vrow = v_ref[t, pl.ds(vi, 1), :]
                sn = sv * w + sa * kka + vrow * k
                state_ref[vi] = sn
                yrow_ref[pl.ds(vi, 1), :] = jnp.sum(sn * r, axis=0, keepdims=True)
            return c

        lax.fori_loop(0, n // RWKV_VUNROLL, vloop, 0)
        y = yrow_ref[...]
        mu = jnp.mean(y, axis=0, keepdims=True)
        var = jnp.mean(jnp.square(y - mu), axis=0, keepdims=True)
        yn = (y - mu) * lax.rsqrt(var + RWKV_GN_EPS) * lng_ref[...] + lnb_ref[...]
        bonus = jnp.sum(r * k * rk_ref[...], axis=0, keepdims=True)
        y_ref[t] = yn + bonus * v_ref[t]
        return carry

    lax.fori_loop(0, r_ref.shape[0], step, 0)


def _rwkv_scan(r, dec, k, v, kk, a, r_k, ln_g, ln_b, *, tc=32):
    B, S, D = r.shape
    H, n = RWKV_HEADS, RWKV_HEAD
    lanes = B * H
    to_scan = lambda t: jnp.transpose(t.reshape(B, S, H, n), (1, 3, 0, 2)).reshape(S, n, lanes)
    per_head = lambda p: jnp.tile(p.T, (1, B))
    blk = pl.BlockSpec((tc, n, lanes), lambda i: (i, 0, 0))
    cst = pl.BlockSpec((n, lanes), lambda i: (0, 0))
    y = pl.pallas_call(
        _rwkv_scan_kernel,
        out_shape=jax.ShapeDtypeStruct((S, n, lanes), F32),
        grid=(S // tc,),
        in_specs=[blk] * 6 + [cst] * 3,
        out_specs=blk,
        scratch_shapes=[pltpu.VMEM((n, n, lanes), F32), pltpu.VMEM((n, lanes), F32)],
        compiler_params=_cparams(("arbitrary",)),
        name="rwkv_scan",
    )(to_scan(r), to_scan(dec), to_scan(k), to_scan(v), to_scan(kk), to_scan(a),
      per_head(r_k), per_head(ln_g), per_head(ln_b))
    return jnp.transpose(y.reshape(S, n, B, H), (2, 0, 3, 1)).reshape(B, S, D)


def _rwkv_mixer(x, g, sc, sh, gt, mu, w_r, w_k, w_v, w_o, w0, w1, w2, a0, a1, a2, g1, g2, k_k, k_a, r_k, ln_g, ln_b):
    B, S, D = x.shape
    x2d = x.reshape(B * S, D)
    tm = min(256, S)
    r, dec, k, v, kk, a, gg = _rwkv_proj(x2d, g, sc, sh, mu, w_r, w_k, w_v, w1, w2, a1, a2, g1, g2, w0, a0, k_k, k_a,
                                         rows_per_batch=S, tm=tm)
    sh3 = lambda t: t.reshape(B, S, D)
    y = _rwkv_scan(sh3(r), sh3(dec), sh3(k), sh3(v), sh3(kk), sh3(a), r_k, ln_g, ln_b, tc=min(32, S))
    out = _mm_res(y.reshape(B * S, D), w_o.astype(BF16), x2d, gt, rows_per_batch=S, mul=gg, tm=min(512, S))
    return out.reshape(B, S, D)


def _rwkv_mixer_test(x, g, sc, sh, p):
    gt = jnp.ones_like(sc)
    return _rwkv_mixer(x, g, sc, sh, gt, p['mu'], p['w_r'], p['w_k'], p['w_v'], p['w_o'], p['w0'], p['w1'], p['w2'],
                       p['a0'], p['a1'], p['a2'], p['g1'], p['g2'], p['k_k'], p['k_a'], p['r_k'], p['ln_g'],
                       p['ln_b']) - x


def _topk_rows(s, k, rows=None):
    if rows is None:
        rows = lax.broadcasted_iota(jnp.int32, s.shape, 0)
    n = jnp.iinfo(jnp.int32).max
    vals, ids = [], []
    for _ in range(k):
        m = jnp.max(s, axis=0, keepdims=True)
        idx = jnp.min(jnp.where(s == m, rows, n), axis=0, keepdims=True)
        vals.append(m)
        ids.append(idx)
        s = jnp.where(rows == idx, -jnp.inf, s)
    return jnp.concatenate(vals, axis=0), jnp.concatenate(ids, axis=0)


def _take_rows(table, pos):
    out = jnp.zeros(pos.shape, table.dtype)
    for m in range(table.shape[0]):
        out = jnp.where(pos == m, table[m:m + 1, :], out)
    return out


_PEER_CAND = [(i, PEER_TOPK // (i + 1)) for i in range(PEER_TOPK)]
_PEER_NCAND = sum(n for _, n in _PEER_CAND)
_PEER_NCAND_PAD = SUBLANES * pl.cdiv(_PEER_NCAND, SUBLANES)


def _peer_cand_codes():
    codes = [i * PEER_TOPK + j for i, n in _PEER_CAND for j in range(n)]
    codes += [PEER_TOPK * PEER_TOPK + p for p in range(_PEER_NCAND_PAD - _PEER_NCAND)]
    return jnp.broadcast_to(jnp.asarray(codes, jnp.int32)[:, None], (_PEER_NCAND_PAD, LANES))


def _peer_route_kernel(q_ref, keys_ref, codes_ref, idx_o, gate_o):
    K, half = PEER_TOPK, PEER_DQ // 2

    tm = q_ref.shape[0]
    G = 2
    codes = jnp.concatenate([codes_ref[...]] * (G * tm // LANES), axis=1)
    pad = jnp.full((_PEER_NCAND_PAD - _PEER_NCAND, tm), -jnp.inf, F32)

    def group(hg, carry):
        ss = []
        for dh in range(G):
            for p in range(2):
                c = pl.multiple_of((2 * (G * hg + dh) + p) * half, half)
                ss.append(_bdot_nt(keys_ref[G * hg + dh, p], q_ref[:, pl.ds(c, half)]))
        vals, ids = _topk_rows(jnp.concatenate(ss, axis=1), K)
        part = lambda t, j: t[:, j * tm:(j + 1) * tm]
        cands = []
        for dh in range(G):
            va, vb = part(vals, 2 * dh), part(vals, 2 * dh + 1)
            cands.append(jnp.concatenate([va[i:i + 1, :] + vb[:n, :] for i, n in _PEER_CAND] + [pad], axis=0))
        best, pos = _topk_rows(jnp.concatenate(cands, axis=1), K, codes)
        for dh in range(G):
            ia, ib = part(ids, 2 * dh), part(ids, 2 * dh + 1)
            ps, bs = part(pos, dh), part(best, dh)
            expert = _take_rows(ia, ps // K) * PEER_KEYS + _take_rows(ib, ps % K)
            e = jnp.exp(bs - bs[0:1, :])
            r0 = pl.multiple_of((G * hg + dh) * K, K)
            idx_o[pl.ds(r0, K), :] = expert
            gate_o[pl.ds(r0, K), :] = e / jnp.sum(e, axis=0, keepdims=True)
        return carry

    lax.fori_loop(0, PEER_HEADS // G, group, 0)


def _peer_route(q, sub_keys, *, tm=256):
    t, n = q.shape
    blk = pl.BlockSpec((PEER_SEL, tm), lambda i: (0, i))
    return pl.pallas_call(
        _peer_route_kernel,
        out_shape=[jax.ShapeDtypeStruct((PEER_SEL, t), jnp.int32), jax.ShapeDtypeStruct((PEER_SEL, t), F32)],
        grid=(t // tm,),
        in_specs=[pl.BlockSpec((tm, n), lambda i: (i, 0)),
                  pl.BlockSpec(sub_keys.shape, lambda i: (0, 0, 0, 0)),
                  pl.BlockSpec((_PEER_NCAND_PAD, LANES), lambda i: (0, 0))],
        out_specs=[blk, blk],
        compiler_params=_cparams(("parallel",)),
        name="peer_route",
    )(q, sub_keys.astype(BF16), _peer_cand_codes())


def _pack_uv(u, v):
    e, d = u.shape
    ub = lax.bitcast_convert_type(u.astype(BF16), jnp.uint16).astype(jnp.uint32)
    vb = lax.bitcast_convert_type(v.astype(BF16), jnp.uint16).astype(jnp.uint32)
    return lax.bitcast_convert_type((vb << 16) | ub, jnp.int32).reshape(e, d // LANES, LANES)


PEER_NBUF = 4


def _gelu(x):
    return 0.5 * x * (1.0 + lax.erf(x * (2.0 ** -0.5)))


def _peer_expert_kernel(idx_hbm, gate_ref, h_ref, xres_ref, gt_ref, uv_hbm, o_ref, idx_smem, buf, sem_idx, sem):
    tb = h_ref.shape[0]
    nsel = PEER_SEL
    base = pl.program_id(0) * (tb * nsel)
    cp = pltpu.make_async_copy(idx_hbm.at[pl.ds(base, tb * nsel)], idx_smem, sem_idx)
    cp.start()
    cp.wait()

    def issue(t, slot):
        for k in range(nsel):
            e = idx_smem[t * nsel + k]
            pltpu.make_async_copy(uv_hbm.at[e], buf.at[slot, k], sem.at[slot]).start()

    def wait(slot):
        pltpu.make_async_copy(uv_hbm.at[pl.ds(0, nsel)], buf.at[slot], sem.at[slot]).wait()

    for t0 in range(PEER_NBUF - 1):
        issue(t0, t0)

    lane = lax.broadcasted_iota(jnp.int32, gate_ref.shape, 1)

    def body(t, carry):
        slot = t % PEER_NBUF
        nxt = t + PEER_NBUF - 1

        @pl.when(nxt < tb)
        def _():
            issue(nxt, nxt % PEER_NBUF)

        wait(slot)
        packed = buf[slot]
        u = lax.bitcast_convert_type(packed << 16, F32)
        v = lax.bitcast_convert_type(packed & jnp.int32(-65536), F32)
        x = h_ref[t]
        act = jnp.sum(jnp.sum(u * x[None], axis=1), axis=1, keepdims=True)
        gate = jnp.sum(jnp.where(lane == t, gate_ref[...], 0.0), axis=1, keepdims=True)
        wgt = gate * _gelu(act)
        out = jnp.sum(v * wgt[:, :, None], axis=0)
        o_ref[t] = xres_ref[t] + gt_ref[0] * out
        return carry

    lax.fori_loop(0, tb, body, 0)


def _peer_expert(idx_flat, gate_t, h, xres, gt, uv, *, rows_per_batch, tb=128):
    t, d = h.shape
    nb = gt.shape[0]
    c = d // LANES
    tpb = rows_per_batch // tb
    tok = pl.BlockSpec((tb, c, LANES), lambda i: (i, 0, 0))
    out = pl.pallas_call(
        _peer_expert_kernel,
        out_shape=jax.ShapeDtypeStruct((t, c, LANES), F32),
        grid=(t // tb,),
        in_specs=[
            pl.BlockSpec(memory_space=pl.ANY),
            pl.BlockSpec((PEER_SEL, tb), lambda i: (0, i)),
            tok, tok,
            pl.BlockSpec((1, c, LANES), lambda i: (i // tpb, 0, 0)),
            pl.BlockSpec(memory_space=pl.ANY),
        ],
        out_specs=tok,
        scratch_shapes=[
            pltpu.SMEM((tb * PEER_SEL,), jnp.int32),
            pltpu.VMEM((PEER_NBUF, PEER_SEL, c, LANES), jnp.int32),
            pltpu.SemaphoreType.DMA,
            pltpu.SemaphoreType.DMA((PEER_NBUF,)),
        ],
        compiler_params=_cparams(("arbitrary",)),
        name="peer_expert",
    )(idx_flat, gate_t, h.reshape(t, c, LANES), xres.reshape(t, c, LANES), gt.reshape(nb, c, LANES), uv)
    return out.reshape(t, d)


def _peer_ffn(x2d, g, sc, sh, gt, w_q, sub_keys, uv, *, rows_per_batch):
    tm = min(512, rows_per_batch)
    q, h = _nm_matmul(x2d, g, sc, sh, w_q, rows_per_batch=rows_per_batch, tm=tm, emit_h=True)
    idx_t, gate_t = _peer_route(q, sub_keys, tm=min(256, rows_per_batch))
    return _peer_expert(idx_t.T.reshape(-1), gate_t, h, x2d, gt, uv, rows_per_batch=rows_per_batch,
                        tb=min(128, rows_per_batch))


def _peer_test(x, g, sc, sh, w_q, sub_keys, u, v):
    B, S, D = x.shape
    x2d = x.reshape(B * S, D)
    return (_peer_ffn(x2d, g, sc, sh, jnp.ones_like(sc), w_q.astype(BF16), sub_keys, _pack_uv(u, v),
                      rows_per_batch=S) - x2d).reshape(B, S, D)


def _final_norm_kernel(x_ref, g_ref, o_ref):
    x = x_ref[...]
    ms = jnp.mean(x * x, axis=-1, keepdims=True)
    o_ref[...] = x * lax.rsqrt(ms + NORM_EPS) * g_ref[...]


def _final_norm(x2d, g, *, tm=512):
    m, d = x2d.shape
    tile = pl.BlockSpec((tm, d), lambda i: (i, 0))
    return pl.pallas_call(
        _final_norm_kernel,
        out_shape=jax.ShapeDtypeStruct((m, d), F32),
        grid=(m // tm,),
        in_specs=[tile, pl.BlockSpec((1, d), lambda i: (0, 0))],
        out_specs=tile,
        compiler_params=_cparams(("parallel",)),
        name="final_norm",
    )(x2d, g.reshape(1, d))


def kernel(x, c, ada_w, ada_b, norm_mix_g, norm_ffn_g, final_norm_g, ret_w_in, ret_w_out, ret_gn_g, ret_gn_b, gdn_w_in, gdn_conv_w, gdn_a_log, gdn_dt_bias, gdn_norm_g, gdn_w_out, rwkv_mu, rwkv_w_r, rwkv_w_k, rwkv_w_v, rwkv_w_o, rwkv_w0, rwkv_w1, rwkv_w2, rwkv_a0, rwkv_a1, rwkv_a2, rwkv_g1, rwkv_g2, rwkv_k_k, rwkv_k_a, rwkv_r_k, rwkv_ln_g, rwkv_ln_b, peer_w_q, peer_sub_keys, peer_u, peer_v):
    B, S, D = x.shape
    T = B * S
    depth = ada_w.shape[0]
    bf = lambda t: t.astype(BF16)
    mod = _adaln(c, ada_w, ada_b)
    x2d = x.reshape(T, D)
    for layer in range(depth):
        sh_m, sc_m, gt_m, sh_f, sc_f, gt_f = [mod[layer, :, i * D:(i + 1) * D] for i in range(6)]
        g_mix = norm_mix_g[layer]
        kind, j = layer % 3, layer // 3
        if kind == 0:
            proj = _nm_matmul(x2d, g_mix, sc_m, sh_m, bf(ret_w_in[j]), rows_per_batch=S, tn=2048)
            o = _retention_scan(proj.reshape(B, S, -1), ret_gn_g[j], ret_gn_b[j])
            x2d = _mm_res(o.reshape(T, -1), bf(ret_w_out[j]), x2d, gt_m, rows_per_batch=S)
        elif kind == 1:
            w = gdn_w_in[j]
            wide = GDN_QKV + GDN_HEADS * GDN_DV
            proj = _nm_matmul(x2d, g_mix, sc_m, sh_m, bf(w[:, :wide]), rows_per_batch=S, tn=2048)
            proj_ab = _nm_matmul(x2d, g_mix, sc_m, sh_m, bf(_pad_cols(w[:, wide:], LANES)), rows_per_batch=S)
            o = _gdn_scan(proj.reshape(B, S, -1), proj_ab.reshape(B, S, -1), gdn_conv_w[j], gdn_a_log[j],
                          gdn_dt_bias[j], gdn_norm_g[j])
            x2d = _mm_res(o.reshape(T, -1), bf(gdn_w_out[j]), x2d, gt_m, rows_per_batch=S)
        else:
            x2d = _rwkv_mixer(x2d.reshape(B, S, D), g_mix, sc_m, sh_m, gt_m, rwkv_mu[j], rwkv_w_r[j], rwkv_w_k[j],
                              rwkv_w_v[j], rwkv_w_o[j], rwkv_w0[j], rwkv_w1[j], rwkv_w2[j], rwkv_a0[j], rwkv_a1[j],
                              rwkv_a2[j], rwkv_g1[j], rwkv_g2[j], rwkv_k_k[j], rwkv_k_a[j], rwkv_r_k[j],
                              rwkv_ln_g[j], rwkv_ln_b[j]).reshape(T, D)
        x2d = _peer_ffn(x2d, norm_ffn_g[layer], sc_f, sh_f, gt_f, bf(peer_w_q[layer]), peer_sub_keys[layer],
                        _pack_uv(peer_u[layer], peer_v[layer]), rows_per_batch=S)
    return _final_norm(x2d, final_norm_g).reshape(B, S, D)
```

```python
import functools
import math

import jax
import jax.numpy as jnp
from jax import lax
from jax.experimental import pallas as pl
from jax.experimental.pallas import tpu as pltpu
from jax.experimental.pallas import tpu_sc as plsc

F32 = jnp.float32
BF16 = jnp.bfloat16
HIGHEST = lax.Precision.HIGHEST

D_MODEL = 1024
NORM_EPS = 1e-6

RET_HEADS = 4
RET_DK = D_MODEL // RET_HEADS
RET_DV = 2 * D_MODEL // RET_HEADS
RET_CHUNK = 128
RET_ROPE_BASE = 10000.0
RET_GN_EPS = 1e-5

GDN_HEADS = 8
GDN_DK = D_MODEL // GDN_HEADS
GDN_DV = D_MODEL // GDN_HEADS
GDN_CONV = 4
GDN_CHUNK = 64
GDN_QKV = GDN_HEADS * (2 * GDN_DK + GDN_DV)

RWKV_HEAD = 64
RWKV_HEADS = D_MODEL // RWKV_HEAD
RWKV_GN_EPS = 64e-5

PEER_KEYS = 128
PEER_HEADS = 8
PEER_DQ = 256
PEER_TOPK = 16
PEER_SEL = PEER_HEADS * PEER_TOPK

LANES = 128
SUBLANES = 8
VMEM_LIMIT = 56 * 1024 * 1024


def _cparams(sem):
    return pltpu.CompilerParams(dimension_semantics=sem, vmem_limit_bytes=VMEM_LIMIT)


def _bdot(a, b):
    return jnp.dot(a.astype(BF16), b.astype(BF16), preferred_element_type=F32)


def _bdot_nt(a, b):
    return lax.dot_general(a.astype(BF16), b.astype(BF16), (((1,), (1,)), ((), ())),
                           preferred_element_type=F32)


def _bdot_tn(a, b):
    return lax.dot_general(a.astype(BF16), b.astype(BF16), (((0,), (0,)), ((), ())),
                           preferred_element_type=F32)


def _hdot(a, b):
    return jnp.dot(a, b, preferred_element_type=F32, precision=HIGHEST)


def _sigmoid(x):
    return 1.0 / (1.0 + jnp.exp(-x))


def _silu(x):
    return x * _sigmoid(x)


def _softplus(x):
    return jnp.maximum(x, 0.0) + jnp.log1p(jnp.exp(-jnp.abs(x)))


def _norm_mod(x, g, sc, sh):
    ms = jnp.mean(x * x, axis=-1, keepdims=True)
    return (x * lax.rsqrt(ms + NORM_EPS) * g) * (1.0 + sc) + sh


def _adaln_kernel(c_ref, w_ref, b_ref, o_ref):
    cond = _silu(c_ref[...])
    o_ref[0] = _bdot(cond, w_ref[0]) + b_ref[0]


def _adaln(c, ada_w, ada_b):
    depth, d, n = ada_w.shape
    b = c.shape[0]
    tn = 1024
    return pl.pallas_call(
        _adaln_kernel,
        out_shape=jax.ShapeDtypeStruct((depth, b, n), F32),
        grid=(depth, n // tn),
        in_specs=[
            pl.BlockSpec((b, d), lambda l, j: (0, 0)),
            pl.BlockSpec((1, d, tn), lambda l, j: (l, 0, j)),
            pl.BlockSpec((1, 1, tn), lambda l, j: (l, 0, j)),
        ],
        out_specs=pl.BlockSpec((1, b, tn), lambda l, j: (l, 0, j)),
        compiler_params=_cparams(("parallel", "parallel")),
        name="adaln",
    )(c, ada_w, ada_b.reshape(depth, 1, n))


def _nm_matmul_kernel(x_ref, g_ref, sc_ref, sh_ref, w_ref, o_ref, *h_ref):
    h = _norm_mod(x_ref[...], g_ref[...], sc_ref[0], sh_ref[0])
    o_ref[...] = jnp.dot(h.astype(BF16), w_ref[...], preferred_element_type=F32).astype(o_ref.dtype)
    if h_ref:
        h_ref[0][...] = h


def _nm_matmul(x2d, g, sc, sh, w, *, rows_per_batch, tm=512, tn=None, emit_h=False):
    m, d = x2d.shape
    n = w.shape[1]
    nb = sc.shape[0]
    if tn is None:
        tn = n
    tpb = rows_per_batch // tm
    out_shape = [jax.ShapeDtypeStruct((m, n), F32)]
    out_specs = [pl.BlockSpec((tm, tn), lambda j, i: (i, j))]
    if emit_h:
        assert tn == n
        out_shape.append(jax.ShapeDtypeStruct((m, d), F32))
        out_specs.append(pl.BlockSpec((tm, d), lambda j, i: (i, 0)))
    res = pl.pallas_call(
        _nm_matmul_kernel,
        out_shape=out_shape,
        grid=(n // tn, m // tm),
        in_specs=[
            pl.BlockSpec((tm, d), lambda j, i: (i, 0)),
            pl.BlockSpec((1, d), lambda j, i: (0, 0)),
            pl.BlockSpec((1, 1, d), lambda j, i: (i // tpb, 0, 0)),
            pl.BlockSpec((1, 1, d), lambda j, i: (i // tpb, 0, 0)),
            pl.BlockSpec((d, tn), lambda j, i: (0, j)),
        ],
        out_specs=out_specs,
        compiler_params=_cparams(("parallel", "parallel")),
        name="norm_mod_matmul",
    )(x2d, g.reshape(1, d), sc.reshape(nb, 1, d), sh.reshape(nb, 1, d), w)
    return res if emit_h else res[0]


def _mm_res_kernel(*refs, has_mul):
    if has_mul:
        a_ref, m_ref, w_ref, r_ref, gt_ref, o_ref = refs
        a = a_ref[...] * m_ref[...]
    else:
        a_ref, w_ref, r_ref, gt_ref, o_ref = refs
        a = a_ref[...]
    y = jnp.dot(a.astype(BF16), w_ref[...], preferred_element_type=F32)
    o_ref[...] = r_ref[...] + gt_ref[0] * y


def _mm_res(a, w, res, gt, *, rows_per_batch, mul=None, tm=512):
    m, k = a.shape
    n = w.shape[1]
    nb = gt.shape[0]
    tpb = rows_per_batch // tm
    ins = [a]
    specs = [pl.BlockSpec((tm, k), lambda i: (i, 0))]
    if mul is not None:
        ins.append(mul)
        specs.append(pl.BlockSpec((tm, k), lambda i: (i, 0)))
    ins += [w, res, gt.reshape(nb, 1, n)]
    specs += [
        pl.BlockSpec((k, n), lambda i: (0, 0)),
        pl.BlockSpec((tm, n), lambda i: (i, 0)),
        pl.BlockSpec((1, 1, n), lambda i: (i // tpb, 0, 0)),
    ]
    return pl.pallas_call(
        functools.partial(_mm_res_kernel, has_mul=mul is not None),
        out_shape=jax.ShapeDtypeStruct((m, n), F32),
        grid=(m // tm,),
        in_specs=specs,
        out_specs=pl.BlockSpec((tm, n), lambda i: (i, 0)),
        compiler_params=_cparams(("parallel",)),
        name="matmul_residual",
    )(*ins)


def _ret_kernel(q_ref, k_ref, v_ref, gate_ref, cos_ref, sin_ref, dintra_ref, dq_ref, dk_ref, dchunk_ref,
                gng_ref, gnb_ref, o_ref, state_ref):
    H, dk, dv = RET_HEADS, RET_DK, RET_DV
    half = dk // 2

    @pl.when(pl.program_id(1) == 0)
    def _():
        state_ref[...] = jnp.zeros_like(state_ref)

    cos = cos_ref[...]
    sin = sin_ref[...]

    def rot(ref, h):
        x1 = ref[0, :, h * dk:h * dk + half]
        x2 = ref[0, :, h * dk + half:(h + 1) * dk]
        return jnp.concatenate([x1 * cos - x2 * sin, x1 * sin + x2 * cos], axis=-1)

    for h in range(H):
        q = rot(q_ref, h)
        k = rot(k_ref, h) * (dk ** -0.5)
        v = v_ref[0, :, h * dv:(h + 1) * dv]
        scores = _bdot_nt(q, k) * dintra_ref[h]
        st = state_ref[h]
        o = _bdot(scores, v) + _bdot(q, st) * dq_ref[h]
        state_ref[h] = st * dchunk_ref[h] + _bdot_tn(k * dk_ref[h], v)
        mu = jnp.mean(o, axis=-1, keepdims=True)
        var = jnp.mean(jnp.square(o - mu), axis=-1, keepdims=True)
        on = (o - mu) * lax.rsqrt(var + RET_GN_EPS) * gng_ref[h] + gnb_ref[h]
        g = gate_ref[0, :, h * dv:(h + 1) * dv]
        o_ref[0, :, h * dv:(h + 1) * dv] = (_silu(g) * on).astype(o_ref.dtype)


def _retention_scan(proj, gn_g, gn_b):
    B, S, _ = proj.shape
    H, dk, dv, C = RET_HEADS, RET_DK, RET_DV, RET_CHUNK
    half = dk // 2
    N = S // C
    inv_freq = RET_ROPE_BASE ** (-jnp.arange(half, dtype=F32) / half)
    ang = jnp.arange(S, dtype=F32)[:, None] * inv_freq[None, :]
    cos, sin = jnp.cos(ang), jnp.sin(ang)
    log_gamma = jnp.log1p(-jnp.exp2(-5.0 - jnp.arange(H, dtype=F32)))
    idx = jnp.arange(C, dtype=F32)
    diff = idx[:, None] - idx[None, :]
    causal = diff >= 0
    d_intra = jnp.where(causal[None], jnp.exp(jnp.where(causal, diff, 0.0)[None] * log_gamma[:, None, None]), 0.0)
    d_q = jnp.exp((idx[None, :] + 1.0) * log_gamma[:, None])[:, :, None]
    d_k = jnp.exp((C - 1.0 - idx)[None, :] * log_gamma[:, None])[:, :, None]
    d_chunk = jnp.exp(C * log_gamma)[:, None, None]
    qw, vw = H * dk, H * dv
    return pl.pallas_call(
        _ret_kernel,
        out_shape=jax.ShapeDtypeStruct((B, S, vw), BF16),
        grid=(B, N),
        in_specs=[
            pl.BlockSpec((1, C, qw), lambda b, n: (b, n, 0)),
            pl.BlockSpec((1, C, qw), lambda b, n: (b, n, 1)),
            pl.BlockSpec((1, C, vw), lambda b, n: (b, n, 1)),
            pl.BlockSpec((1, C, vw), lambda b, n: (b, n, 2)),
            pl.BlockSpec((C, half), lambda b, n: (n, 0)),
            pl.BlockSpec((C, half), lambda b, n: (n, 0)),
            pl.BlockSpec((H, C, C), lambda b, n: (0, 0, 0)),
            pl.BlockSpec((H, C, 1), lambda b, n: (0, 0, 0)),
            pl.BlockSpec((H, C, 1), lambda b, n: (0, 0, 0)),
            pl.BlockSpec((H, 1, 1), lambda b, n: (0, 0, 0)),
            pl.BlockSpec((H, 1, dv), lambda b, n: (0, 0, 0)),
            pl.BlockSpec((H, 1, dv), lambda b, n: (0, 0, 0)),
        ],
        out_specs=pl.BlockSpec((1, C, vw), lambda b, n: (b, n, 0)),
        scratch_shapes=[pltpu.VMEM((H, dk, dv), F32)],
        compiler_params=_cparams(("parallel", "arbitrary")),
        name="retention_scan",
    )(proj, proj, proj, proj, cos, sin, d_intra, d_q, d_k, d_chunk,
      gn_g.reshape(H, 1, dv), gn_b.reshape(H, 1, dv))


def _shift_rows(cur, prev8, s):
    rows = lax.broadcasted_iota(jnp.int32, cur.shape, 0)
    rolled = pltpu.roll(cur, s, axis=0)
    head = pltpu.roll(prev8, s, axis=0)
    head = jnp.concatenate([head, jnp.zeros((cur.shape[0] - SUBLANES, cur.shape[1]), cur.dtype)], axis=0)
    return jnp.where(rows < s, head, rolled)


def _cumsum_rows(x):
    rows = lax.broadcasted_iota(jnp.int32, x.shape, 0)
    s = 1
    while s < x.shape[0]:
        x = x + jnp.where(rows >= s, pltpu.roll(x, s, axis=0), 0.0)
        s *= 2
    return x


def _unit_lower_inverse(L):
    C = L.shape[0]
    eye = (lax.broadcasted_iota(jnp.int32, (C, C), 0) == lax.broadcasted_iota(jnp.int32, (C, C), 1)).astype(F32)
    p = -L
    inv = eye + p
    s = 2
    while s < C:
        p = _hdot(p, p)
        inv = inv + _hdot(inv, p)
        s *= 2
    return inv


def _gdn_kernel(qkv_ref, gate_ref, ab_ref, cw_ref, alog_ref, dtb_ref, ng_ref, o_ref, state_ref, prev_ref):
    C, dk, H = GDN_CHUNK, GDN_DK, GDN_HEADS

    @pl.when(pl.program_id(1) == 0)
    def _():
        state_ref[...] = jnp.zeros_like(state_ref)
        prev_ref[...] = jnp.zeros_like(prev_ref)

    def conv_silu(col):
        cur = qkv_ref[0, :, col * dk:(col + 1) * dk]
        prev8 = prev_ref[:, col * dk:(col + 1) * dk]
        cw = cw_ref[:, col * dk:(col + 1) * dk]
        acc = cur * cw[GDN_CONV - 1:GDN_CONV]
        for s in range(1, GDN_CONV):
            acc = acc + _shift_rows(cur, prev8, s) * cw[GDN_CONV - 1 - s:GDN_CONV - s]
        prev_ref[:, col * dk:(col + 1) * dk] = cur[C - SUBLANES:]
        return _silu(acc)

    ab = ab_ref[0]
    g_all = -jnp.exp(alog_ref[...]) * _softplus(ab + dtb_ref[...])
    beta_all = _sigmoid(ab)
    ri = lax.broadcasted_iota(jnp.int32, (C, C), 0)
    ci = lax.broadcasted_iota(jnp.int32, (C, C), 1)
    incl = ri >= ci

    for h in range(H):
        q = conv_silu(h)
        k = conv_silu(H + h)
        v = conv_silu(2 * H + h)
        q = q * lax.rsqrt(jnp.sum(q * q, axis=-1, keepdims=True) + 1e-6) * (dk ** -0.5)
        k = k * lax.rsqrt(jnp.sum(k * k, axis=-1, keepdims=True) + 1e-6)
        g = g_all[:, h:h + 1]
        beta = beta_all[:, H + h:H + h + 1]

        cum = _cumsum_rows(jnp.broadcast_to(g, (C, LANES)))
        cum_c = cum[:, :1]
        cum_last = cum[C - 1:C, :1]
        cum_r = cum.T[:C, :]
        decay = jnp.where(incl, jnp.exp(jnp.where(incl, cum[:, :C] - cum_r, 0.0)), 0.0)
        L = jnp.where(ri > ci, _bdot_nt(k, k) * decay, 0.0) * beta
        rhs = jnp.concatenate([k * (beta * jnp.exp(cum_c)), v * beta], axis=-1)
        sol = _hdot(_unit_lower_inverse(L), rhs)
        w_c, u_c = sol[:, :dk], sol[:, dk:]
        a_qk = _bdot_nt(q, k) * decay
        q_dec = q * jnp.exp(cum_c)
        k_dec = k * jnp.exp(cum_last - cum_c)

        st = state_ref[h]
        u = u_c - _bdot(w_c, st)
        o = _bdot(q_dec, st) + _bdot(a_qk, u)
        state_ref[h] = st * jnp.exp(cum_last) + _bdot_tn(k_dec, u)

        ms = jnp.mean(o * o, axis=-1, keepdims=True)
        o = o * lax.rsqrt(ms + NORM_EPS) * ng_ref[...]
        o_ref[0, :, h * dk:(h + 1) * dk] = (o * _silu(gate_ref[0, :, h * dk:(h + 1) * dk])).astype(o_ref.dtype)


def _gdn_scan(proj, proj_ab, conv_w, a_log, dt_bias, norm_g):
    B, S, _ = proj.shape
    H, dk, C = GDN_HEADS, GDN_DK, GDN_CHUNK
    N = S // C
    pad = lambda t: jnp.pad(t.astype(F32), (0, LANES - H)).reshape(1, LANES)
    row = pl.BlockSpec((1, LANES), lambda b, n: (0, 0))
    return pl.pallas_call(
        _gdn_kernel,
        out_shape=jax.ShapeDtypeStruct((B, S, H * dk), BF16),
        grid=(B, N),
        in_specs=[pl.BlockSpec((1, C, 3 * H * dk), lambda b, n: (b, n, 0)),
                  pl.BlockSpec((1, C, H * dk), lambda b, n: (b, n, 3)),
                  pl.BlockSpec((1, C, LANES), lambda b, n: (b, n, 0)),
                  pl.BlockSpec((GDN_CONV, 3 * H * dk), lambda b, n: (0, 0)),
                  row, row, row],
        out_specs=pl.BlockSpec((1, C, H * dk), lambda b, n: (b, n, 0)),
        scratch_shapes=[pltpu.VMEM((H, dk, dk), F32), pltpu.VMEM((SUBLANES, 3 * H * dk), F32)],
        compiler_params=_cparams(("parallel", "arbitrary")),
        name="gdn_scan",
    )(proj, proj, proj_ab, conv_w, pad(a_log), pad(dt_bias), norm_g.reshape(1, dk))


def _rwkv_proj_kernel(x_ref, xp_ref, g_ref, sc_ref, sh_ref, mu_ref, wr_ref, wk_ref, wv_ref, w1_ref, w2_ref,
                      a1_ref, a2_ref, g1_ref, g2_ref, w0_ref, a0_ref, kk_ref, ka_ref,
                      r_o, dec_o, k_o, v_o, kk_o, a_o, g_o, *, tiles_per_seq):
    h = _norm_mod(x_ref[...], g_ref[...], sc_ref[0], sh_ref[0])
    hp8 = _norm_mod(xp_ref[...], g_ref[...], sc_ref[0], sh_ref[0])
    seq_start = pl.program_id(0) % tiles_per_seq == 0
    first = jnp.where(seq_start, 0.0, hp8[SUBLANES - 1:SUBLANES, :])
    rows = lax.broadcasted_iota(jnp.int32, h.shape, 0)
    xx = jnp.where(rows == 0, first, pltpu.roll(h, 1, axis=0)) - h
    mix = lambda j: h + xx * mu_ref[j:j + 1, :]
    r = _bdot(mix(0), wr_ref[...])
    lw = w0_ref[...] + _bdot(jnp.tanh(_bdot(mix(1), w1_ref[...])), w2_ref[...])
    k = _bdot(mix(2), wk_ref[...])
    v = _bdot(mix(3), wv_ref[...])
    a = _sigmoid(a0_ref[...] + _bdot(_bdot(mix(4), a1_ref[...]), a2_ref[...]))
    g = _bdot(_sigmoid(_bdot(mix(5), g1_ref[...])), g2_ref[...])
    w = -_softplus(-lw) - 0.5
    r_o[...] = r
    dec_o[...] = jnp.exp(-jnp.exp(w))
    k_o[...] = k * (1.0 + (a - 1.0) * ka_ref[...])
    v_o[...] = v
    kk_o[...] = k * kk_ref[...]
    a_o[...] = a
    g_o[...] = g


def _pad_cols(w, n):
    return jnp.pad(w, ((0, 0), (0, n - w.shape[1])))


def _pad_rows(w, n):
    return jnp.pad(w, ((0, n - w.shape[0]), (0, 0)))


def _rwkv_proj(x2d, g, sc, sh, mu, w_r, w_k, w_v, w1, w2, a1, a2, g1, g2, w0, a0, k_k, k_a, *, rows_per_batch, tm=256):
    m, d = x2d.shape
    nb = sc.shape[0]
    tpb = rows_per_batch // tm
    lora_w = LANES * pl.cdiv(w1.shape[1], LANES)
    lora_g = LANES * pl.cdiv(g1.shape[1], LANES)
    bf = lambda t: t.astype(BF16)
    full = lambda a: pl.BlockSpec(a.shape, lambda i: (0,) * a.ndim)
    row = lambda t: t.reshape(1, d)
    ws = [bf(w_r), bf(w_k), bf(w_v), bf(_pad_cols(w1, lora_w)), bf(_pad_rows(w2, lora_w)),
          bf(_pad_cols(a1, lora_w)), bf(_pad_rows(a2, lora_w)), bf(_pad_cols(g1, lora_g)), bf(_pad_rows(g2, lora_g)),
          row(w0), row(a0), row(k_k), row(k_a)]
    tile = pl.BlockSpec((tm, d), lambda i: (i, 0))
    return pl.pallas_call(
        functools.partial(_rwkv_proj_kernel, tiles_per_seq=tpb),
        out_shape=[jax.ShapeDtypeStruct((m, d), F32)] * 7,
        grid=(m // tm,),
        in_specs=[
            tile,
            pl.BlockSpec((SUBLANES, d), lambda i: (jnp.maximum(i * (tm // SUBLANES) - 1, 0), 0)),
            pl.BlockSpec((1, d), lambda i: (0, 0)),
            pl.BlockSpec((1, 1, d), lambda i: (i // tpb, 0, 0)),
            pl.BlockSpec((1, 1, d), lambda i: (i // tpb, 0, 0)),
            full(mu),
        ] + [full(w) for w in ws],
        out_specs=[tile] * 7,
        compiler_params=_cparams(("parallel",)),
        name="rwkv_proj",
    )(x2d, x2d, g.reshape(1, d), sc.reshape(nb, 1, d), sh.reshape(nb, 1, d), mu, *ws)


RWKV_VUNROLL = 4


def _rwkv_scan_kernel(r_ref, w_ref, k_ref, v_ref, kk_ref, a_ref, rk_ref, lng_ref, lnb_ref, y_ref, state_ref, yrow_ref):
    n = RWKV_HEAD

    @pl.when(pl.program_id(0) == 0)
    def _():
        state_ref[...] = jnp.zeros_like(state_ref)

    def step(t, carry):
        r, w, k, kkr, a = r_ref[t], w_ref[t], k_ref[t], kk_ref[t], a_ref[t]
        kk = kkr * lax.rsqrt(jnp.sum(kkr * kkr, axis=0, keepdims=True) + 1e-6)
        nkk = -kk
        kka = kk * a

        def vloop(vb, c):
            for j in range(RWKV_VUNROLL):
                vi = vb * RWKV_VUNROLL + j
                sv = state_ref[vi]
                sa = jnp.sum(sv * nkk, axis=0, keepdims=True)
                vrow = v_ref[t, pl.ds(vi, 1), :]
                sn = sv * w + sa * kka + vrow * k
                state_ref[vi] = sn
                yrow_ref[pl.ds(vi, 1), :] = jnp.sum(sn * r, axis=0, keepdims=True)
            return c

        lax.fori_loop(0, n // RWKV_VUNROLL, vloop, 0)
        y = yrow_ref[...]
        mu = jnp.mean(y, axis=0, keepdims=True)
        var = jnp.mean(jnp.square(y - mu), axis=0, keepdims=True)
        yn = (y - mu) * lax.rsqrt(var + RWKV_GN_EPS) * lng_ref[...] + lnb_ref[...]
        bonus = jnp.sum(r * k * rk_ref[...], axis=0, keepdims=True)
        y_ref[t] = yn + bonus * v_ref[t]
        return carry

    lax.fori_loop(0, r_ref.shape[0], step, 0)


def _rwkv_scan(r, dec, k, v, kk, a, r_k, ln_g, ln_b, *, tc=32):
    B, S, D = r.shape
    H, n = RWKV_HEADS, RWKV_HEAD
    lanes = B * H
    to_scan = lambda t: jnp.transpose(t.reshape(B, S, H, n), (1, 3, 0, 2)).reshape(S, n, lanes)
    per_head = lambda p: jnp.tile(p.T, (1, B))
    blk = pl.BlockSpec((tc, n, lanes), lambda i: (i, 0, 0))
    cst = pl.BlockSpec((n, lanes), lambda i: (0, 0))
    y = pl.pallas_call(
        _rwkv_scan_kernel,
        out_shape=jax.ShapeDtypeStruct((S, n, lanes), F32),
        grid=(S // tc,),
        in_specs=[blk] * 6 + [cst] * 3,
        out_specs=blk,
        scratch_shapes=[pltpu.VMEM((n, n, lanes), F32), pltpu.VMEM((n, lanes), F32)],
        compiler_params=_cparams(("arbitrary",)),
        name="rwkv_scan",
    )(to_scan(r), to_scan(dec), to_scan(k), to_scan(v), to_scan(kk), to_scan(a),
      per_head(r_k), per_head(ln_g), per_head(ln_b))
    return jnp.transpose(y.reshape(S, n, B, H), (2, 0, 3, 1)).reshape(B, S, D)


def _rwkv_mixer(x, g, sc, sh, gt, mu, w_r, w_k, w_v, w_o, w0, w1, w2, a0, a1, a2, g1, g2, k_k, k_a, r_k, ln_g, ln_b):
    B, S, D = x.shape
    x2d = x.reshape(B * S, D)
    tm = min(256, S)
    r, dec, k, v, kk, a, gg = _rwkv_proj(x2d, g, sc, sh, mu, w_r, w_k, w_v, w1, w2, a1, a2, g1, g2, w0, a0, k_k, k_a,
                                         rows_per_batch=S, tm=tm)
    sh3 = lambda t: t.reshape(B, S, D)
    y = _rwkv_scan(sh3(r), sh3(dec), sh3(k), sh3(v), sh3(kk), sh3(a), r_k, ln_g, ln_b, tc=min(32, S))
    out = _mm_res(y.reshape(B * S, D), w_o.astype(BF16), x2d, gt, rows_per_batch=S, mul=gg, tm=min(512, S))
    return out.reshape(B, S, D)


def _rwkv_mixer_test(x, g, sc, sh, p):
    gt = jnp.ones_like(sc)
    return _rwkv_mixer(x, g, sc, sh, gt, p['mu'], p['w_r'], p['w_k'], p['w_v'], p['w_o'], p['w0'], p['w1'], p['w2'],
                       p['a0'], p['a1'], p['a2'], p['g1'], p['g2'], p['k_k'], p['k_a'], p['r_k'], p['ln_g'],
                       p['ln_b']) - x


def _topk_rows(s, k, rows=None):
    if rows is None:
        rows = lax.broadcasted_iota(jnp.int32, s.shape, 0)
    n = jnp.iinfo(jnp.int32).max
    vals, ids = [], []
    for _ in range(k):
        m = jnp.max(s, axis=0, keepdims=True)
        idx = jnp.min(jnp.where(s == m, rows, n), axis=0, keepdims=True)
        vals.append(m)
        ids.append(idx)
        s = jnp.where(rows == idx, -jnp.inf, s)
    return jnp.concatenate(vals, axis=0), jnp.concatenate(ids, axis=0)


def _take_rows(table, pos):
    out = jnp.zeros(pos.shape, table.dtype)
    for m in range(table.shape[0]):
        out = jnp.where(pos == m, table[m:m + 1, :], out)
    return out


_PEER_CAND = [(i, PEER_TOPK // (i + 1)) for i in range(PEER_TOPK)]
_PEER_NCAND = sum(n for _, n in _PEER_CAND)
_PEER_NCAND_PAD = SUBLANES * pl.cdiv(_PEER_NCAND, SUBLANES)


def _peer_cand_codes():
    codes = [i * PEER_TOPK + j for i, n in _PEER_CAND for j in range(n)]
    codes += [PEER_TOPK * PEER_TOPK + p for p in range(_PEER_NCAND_PAD - _PEER_NCAND)]
    return jnp.broadcast_to(jnp.asarray(codes, jnp.int32)[:, None], (_PEER_NCAND_PAD, LANES))


def _peer_route_kernel(q_ref, keys_ref, codes_ref, idx_o, gate_o):
    K, half = PEER_TOPK, PEER_DQ // 2

    tm = q_ref.shape[0]
    G = 2
    codes = jnp.concatenate([codes_ref[...]] * (G * tm // LANES), axis=1)
    pad = jnp.full((_PEER_NCAND_PAD - _PEER_NCAND, tm), -jnp.inf, F32)

    def group(hg, carry):
        ss = []
        for dh in range(G):
            for p in range(2):
                c = pl.multiple_of((2 * (G * hg + dh) + p) * half, half)
                ss.append(_bdot_nt(keys_ref[G * hg + dh, p], q_ref[:, pl.ds(c, half)]))
        vals, ids = _topk_rows(jnp.concatenate(ss, axis=1), K)
        part = lambda t, j: t[:, j * tm:(j + 1) * tm]
        cands = []
        for dh in range(G):
            va, vb = part(vals, 2 * dh), part(vals, 2 * dh + 1)
            cands.append(jnp.concatenate([va[i:i + 1, :] + vb[:n, :] for i, n in _PEER_CAND] + [pad], axis=0))
        best, pos = _topk_rows(jnp.concatenate(cands, axis=1), K, codes)
        for dh in range(G):
            ia, ib = part(ids, 2 * dh), part(ids, 2 * dh + 1)
            ps, bs = part(pos, dh), part(best, dh)
            expert = _take_rows(ia, ps // K) * PEER_KEYS + _take_rows(ib, ps % K)
            e = jnp.exp(bs - bs[0:1, :])
            r0 = pl.multiple_of((G * hg + dh) * K, K)
            idx_o[pl.ds(r0, K), :] = expert
            gate_o[pl.ds(r0, K), :] = e / jnp.sum(e, axis=0, keepdims=True)
        return carry

    lax.fori_loop(0, PEER_HEADS // G, group, 0)


def _peer_route(q, sub_keys, *, tm=256):
    t, n = q.shape
    blk = pl.BlockSpec((PEER_SEL, tm), lambda i: (0, i))
    return pl.pallas_call(
        _peer_route_kernel,
        out_shape=[jax.ShapeDtypeStruct((PEER_SEL, t), jnp.int32), jax.ShapeDtypeStruct((PEER_SEL, t), F32)],
        grid=(t // tm,),
        in_specs=[pl.BlockSpec((tm, n), lambda i: (i, 0)),
                  pl.BlockSpec(sub_keys.shape, lambda i: (0, 0, 0, 0)),
                  pl.BlockSpec((_PEER_NCAND_PAD, LANES), lambda i: (0, 0))],
        out_specs=[blk, blk],
        compiler_params=_cparams(("parallel",)),
        name="peer_route",
    )(q, sub_keys.astype(BF16), _peer_cand_codes())


def _pack_uv(u, v):
    e, d = u.shape
    ub = lax.bitcast_convert_type(u.astype(BF16), jnp.uint16).astype(jnp.uint32)
    vb = lax.bitcast_convert_type(v.astype(BF16), jnp.uint16).astype(jnp.uint32)
    return lax.bitcast_convert_type((vb << 16) | ub, jnp.int32).reshape(e, d // LANES, LANES)


PEER_NBUF = 4


def _gelu(x):
    return 0.5 * x * (1.0 + lax.erf(x * (2.0 ** -0.5)))


def _peer_eval(packed, x, gate):
    u = lax.bitcast_convert_type(packed << 16, F32)
    v = lax.bitcast_convert_type(packed & jnp.int32(-65536), F32)
    act = jnp.sum(jnp.sum(u * x[None], axis=1), axis=1, keepdims=True)
    wgt = gate * _gelu(act)
    return jnp.sum(v * wgt[:, :, None], axis=0)


def _gate_column(gate_ref, t):
    lane = lax.broadcasted_iota(jnp.int32, gate_ref.shape, 1)
    return jnp.sum(jnp.where(lane == t, gate_ref[...], 0.0), axis=1, keepdims=True)


def _peer_expert_kernel(idx_hbm, gate_ref, h_ref, xres_ref, gt_ref, uv_hbm, o_ref, idx_smem, buf, sem_idx, sem, *,
                        tok0):
    tb = h_ref.shape[0]
    nsel = PEER_SEL
    base = (pl.program_id(0) * tb + tok0) * nsel
    cp = pltpu.make_async_copy(idx_hbm.at[pl.ds(base, tb * nsel)], idx_smem, sem_idx)
    cp.start()
    cp.wait()

    def issue(t, slot):
        for k in range(nsel):
            e = idx_smem[t * nsel + k]
            pltpu.make_async_copy(uv_hbm.at[e], buf.at[slot, k], sem.at[slot]).start()

    def wait(slot):
        pltpu.make_async_copy(uv_hbm.at[pl.ds(0, nsel)], buf.at[slot], sem.at[slot]).wait()

    for t0 in range(PEER_NBUF - 1):
        issue(t0, t0)

    def body(t, carry):
        slot = t % PEER_NBUF
        nxt = t + PEER_NBUF - 1

        @pl.when(nxt < tb)
        def _():
            issue(nxt, nxt % PEER_NBUF)

        wait(slot)
        out = _peer_eval(buf[slot], h_ref[t], _gate_column(gate_ref, t))
        o_ref[t] = xres_ref[t] + gt_ref[0] * out
        return carry

    lax.fori_loop(0, tb, body, 0)


def _peer_expert(idx_flat, gate_t, h3, xres3, gt3, uv, *, rows_per_batch, tok0, ntok, tb=128):
    _, c, _ = h3.shape
    tpb = rows_per_batch // tb
    b0 = tok0 // tb
    tok = pl.BlockSpec((tb, c, LANES), lambda i: (i + b0, 0, 0))
    return pl.pallas_call(
        functools.partial(_peer_expert_kernel, tok0=tok0),
        out_shape=jax.ShapeDtypeStruct((ntok, c, LANES), F32),
        grid=(ntok // tb,),
        in_specs=[
            pl.BlockSpec(memory_space=pl.ANY),
            pl.BlockSpec((PEER_SEL, tb), lambda i: (0, i + b0)),
            tok, tok,
            pl.BlockSpec((1, c, LANES), lambda i: ((i + b0) // tpb, 0, 0)),
            pl.BlockSpec(memory_space=pl.ANY),
        ],
        out_specs=pl.BlockSpec((tb, c, LANES), lambda i: (i, 0, 0)),
        scratch_shapes=[
            pltpu.SMEM((tb * PEER_SEL,), jnp.int32),
            pltpu.VMEM((PEER_NBUF, PEER_SEL, c, LANES), jnp.int32),
            pltpu.SemaphoreType.DMA,
            pltpu.SemaphoreType.DMA((PEER_NBUF,)),
        ],
        compiler_params=_cparams(("arbitrary",)),
        name="peer_expert",
    )(idx_flat, gate_t, h3, xres3, gt3, uv)


SC_CORES = 2
SC_SUBCORES = 16
SC_WORKERS = SC_CORES * SC_SUBCORES
SC_GATHER_ROWS = 32
SC_NBUF = 3
SC_IDX_WINDOW = 1024


def _sc_gather(uv, idx):
    n = idx.shape[0]
    rpw = n // SC_WORKERS
    iw = min(SC_IDX_WINDOW, rpw)
    nwin = rpw // iw
    w = SC_GATHER_ROWS
    nq = iw // w
    mesh = plsc.VectorSubcoreMesh(core_axis_name="core", subcore_axis_name="subcore")

    @functools.partial(
        pl.kernel, out_type=jax.ShapeDtypeStruct((n,) + uv.shape[1:], uv.dtype), mesh=mesh,
        scratch_types=[pltpu.VMEM((iw,), jnp.int32), pltpu.VMEM((SC_NBUF, w) + uv.shape[1:], uv.dtype),
                       pltpu.SemaphoreType.DMA((SC_NBUF,)), pltpu.SemaphoreType.DMA((SC_NBUF,))],
        name="peer_sc_gather")
    def k(t_hbm, i_hbm, o_hbm, idx_v, buf, sem_g, sem_w):
        wid = lax.axis_index("core") * SC_SUBCORES + lax.axis_index("subcore")
        base = wid * rpw

        @pl.loop(0, nwin)
        def _(wi):
            row0 = base + wi * iw
            pltpu.sync_copy(i_hbm.at[pl.ds(row0, iw)], idx_v)

            def gather(q):
                b = q % SC_NBUF
                return pltpu.make_async_copy(t_hbm.at[idx_v.at[pl.ds(q * w, w)]], buf.at[b], sem_g.at[b])

            def write(q):
                b = q % SC_NBUF
                return pltpu.make_async_copy(buf.at[b], o_hbm.at[pl.ds(row0 + q * w, w)], sem_w.at[b])

            for q in range(nq + SC_NBUF - 1):
                if q < nq:
                    if q >= SC_NBUF:
                        write(q - SC_NBUF).wait()
                    gather(q).start()
                p = q - (SC_NBUF - 1)
                if 0 <= p < nq:
                    gather(p).wait()
                    write(p).start()
            for p in range(max(nq - SC_NBUF, 0), nq):
                write(p).wait()

    return k(uv, idx)


PEER_STAGED_TOKENS = 16


def _peer_staged_kernel(rows_ref, gate_ref, h_ref, xres_ref, gt_ref, o_ref):
    ts = h_ref.shape[0]
    t_base = pl.program_id(1) * ts

    def body(t, carry):
        packed = rows_ref[pl.ds(pl.multiple_of(t * PEER_SEL, PEER_SEL), PEER_SEL)]
        out = _peer_eval(packed, h_ref[t], _gate_column(gate_ref, t_base + t))
        o_ref[t] = xres_ref[t] + gt_ref[0] * out
        return carry

    lax.fori_loop(0, ts, body, 0)


def _peer_staged(rows, gate_t, h3, xres3, gt3, *, rows_per_batch, tok0, ntok, tb=128):
    _, c, _ = h3.shape
    ts = min(PEER_STAGED_TOKENS, tb)
    per = tb // ts
    tpb = rows_per_batch // tb
    b0 = tok0 // tb
    tok = pl.BlockSpec((ts, c, LANES), lambda i, j: ((i + b0) * per + j, 0, 0))
    return pl.pallas_call(
        _peer_staged_kernel,
        out_shape=jax.ShapeDtypeStruct((ntok, c, LANES), F32),
        grid=(ntok // tb, per),
        in_specs=[
            pl.BlockSpec((ts * PEER_SEL, c, LANES), lambda i, j: (i * per + j, 0, 0)),
            pl.BlockSpec((PEER_SEL, tb), lambda i, j: (0, i + b0)),
            tok, tok,
            pl.BlockSpec((1, c, LANES), lambda i, j: ((i + b0) // tpb, 0, 0)),
        ],
        out_specs=pl.BlockSpec((ts, c, LANES), lambda i, j: (i * per + j, 0, 0)),
        compiler_params=_cparams(("parallel", "arbitrary")),
        name="peer_staged",
    )(rows, gate_t, h3, xres3, gt3)


PEER_SC_SHARE = (11, 16)
PEER_SC_CHUNKS = 4


def _peer_experts(idx_flat, gate_t, h, xres, gt, uv, *, rows_per_batch):
    t, d = h.shape
    c = d // LANES
    tb = min(128, rows_per_batch)
    h3, xres3, gt3 = h.reshape(t, c, LANES), xres.reshape(t, c, LANES), gt.reshape(-1, c, LANES)
    blocks = t // tb
    nch = min(PEER_SC_CHUNKS, blocks * PEER_SC_SHARE[0] // PEER_SC_SHARE[1])
    per = blocks * PEER_SC_SHARE[0] // PEER_SC_SHARE[1] // max(nch, 1)
    ch = per * tb
    outs = []
    for i in range(nch):
        rows = _sc_gather(uv, lax.slice(idx_flat, (i * ch * PEER_SEL,), ((i + 1) * ch * PEER_SEL,)))
        outs.append(_peer_staged(rows, gate_t, h3, xres3, gt3, rows_per_batch=rows_per_batch, tok0=i * ch, ntok=ch,
                                 tb=tb))
    ta = nch * ch
    if ta < t:
        outs.append(_peer_expert(idx_flat, gate_t, h3, xres3, gt3, uv, rows_per_batch=rows_per_batch, tok0=ta,
                                 ntok=t - ta, tb=tb))
    return jnp.concatenate(outs, axis=0).reshape(t, d)


def _peer_ffn(x2d, g, sc, sh, gt, w_q, sub_keys, uv, *, rows_per_batch):
    tm = min(512, rows_per_batch)
    q, h = _nm_matmul(x2d, g, sc, sh, w_q, rows_per_batch=rows_per_batch, tm=tm, emit_h=True)
    idx_t, gate_t = _peer_route(q, sub_keys, tm=min(256, rows_per_batch))
    return _peer_experts(idx_t.T.reshape(-1), gate_t, h, x2d, gt, uv, rows_per_batch=rows_per_batch)


def _peer_test(x, g, sc, sh, w_q, sub_keys, u, v):
    B, S, D = x.shape
    x2d = x.reshape(B * S, D)
    return (_peer_ffn(x2d, g, sc, sh, jnp.ones_like(sc), w_q.astype(BF16), sub_keys, _pack_uv(u, v),
                      rows_per_batch=S) - x2d).reshape(B, S, D)


def _final_norm_kernel(x_ref, g_ref, o_ref):
    x = x_ref[...]
    ms = jnp.mean(x * x, axis=-1, keepdims=True)
    o_ref[...] = x * lax.rsqrt(ms + NORM_EPS) * g_ref[...]


def _final_norm(x2d, g, *, tm=512):
    m, d = x2d.shape
    tile = pl.BlockSpec((tm, d), lambda i: (i, 0))
    return pl.pallas_call(
        _final_norm_kernel,
        out_shape=jax.ShapeDtypeStruct((m, d), F32),
        grid=(m // tm,),
        in_specs=[tile, pl.BlockSpec((1, d), lambda i: (0, 0))],
        out_specs=tile,
        compiler_params=_cparams(("parallel",)),
        name="final_norm",
    )(x2d, g.reshape(1, d))


def kernel(x, c, ada_w, ada_b, norm_mix_g, norm_ffn_g, final_norm_g, ret_w_in, ret_w_out, ret_gn_g, ret_gn_b, gdn_w_in, gdn_conv_w, gdn_a_log, gdn_dt_bias, gdn_norm_g, gdn_w_out, rwkv_mu, rwkv_w_r, rwkv_w_k, rwkv_w_v, rwkv_w_o, rwkv_w0, rwkv_w1, rwkv_w2, rwkv_a0, rwkv_a1, rwkv_a2, rwkv_g1, rwkv_g2, rwkv_k_k, rwkv_k_a, rwkv_r_k, rwkv_ln_g, rwkv_ln_b, peer_w_q, peer_sub_keys, peer_u, peer_v):
    B, S, D = x.shape
    T = B * S
    depth = ada_w.shape[0]
    bf = lambda t: t.astype(BF16)
    mod = _adaln(c, ada_w, ada_b)
    x2d = x.reshape(T, D)
    for layer in range(depth):
        sh_m, sc_m, gt_m, sh_f, sc_f, gt_f = [mod[layer, :, i * D:(i + 1) * D] for i in range(6)]
        g_mix = norm_mix_g[layer]
        kind, j = layer % 3, layer // 3
        if kind == 0:
            proj = _nm_matmul(x2d, g_mix, sc_m, sh_m, bf(ret_w_in[j]), rows_per_batch=S, tn=2048)
            o = _retention_scan(proj.reshape(B, S, -1), ret_gn_g[j], ret_gn_b[j])
            x2d = _mm_res(o.reshape(T, -1), bf(ret_w_out[j]), x2d, gt_m, rows_per_batch=S)
        elif kind == 1:
            w = gdn_w_in[j]
            wide = GDN_QKV + GDN_HEADS * GDN_DV
            proj = _nm_matmul(x2d, g_mix, sc_m, sh_m, bf(w[:, :wide]), rows_per_batch=S, tn=2048)
            proj_ab = _nm_matmul(x2d, g_mix, sc_m, sh_m, bf(_pad_cols(w[:, wide:], LANES)), rows_per_batch=S)
            o = _gdn_scan(proj.reshape(B, S, -1), proj_ab.reshape(B, S, -1), gdn_conv_w[j], gdn_a_log[j],
                          gdn_dt_bias[j], gdn_norm_g[j])
            x2d = _mm_res(o.reshape(T, -1), bf(gdn_w_out[j]), x2d, gt_m, rows_per_batch=S)
        else:
            x2d = _rwkv_mixer(x2d.reshape(B, S, D), g_mix, sc_m, sh_m, gt_m, rwkv_mu[j], rwkv_w_r[j], rwkv_w_k[j],
                              rwkv_w_v[j], rwkv_w_o[j], rwkv_w0[j], rwkv_w1[j], rwkv_w2[j], rwkv_a0[j], rwkv_a1[j],
                              rwkv_a2[j], rwkv_g1[j], rwkv_g2[j], rwkv_k_k[j], rwkv_k_a[j], rwkv_r_k[j],
                              rwkv_ln_g[j], rwkv_ln_b[j]).reshape(T, D)
        x2d = _peer_ffn(x2d, norm_ffn_g[layer], sc_f, sh_f, gt_f, bf(peer_w_q[layer]), peer_sub_keys[layer],
                        _pack_uv(peer_u[layer], peer_v[layer]), rows_per_batch=S)
    return _final_norm(x2d, final_norm_g).reshape(B, S, D)
```

```python
import functools
import math

import jax
import jax.numpy as jnp
from jax import lax
from jax.experimental import pallas as pl
from jax.experimental.pallas import tpu as pltpu
from jax.experimental.pallas import tpu_sc as plsc

F32 = jnp.float32
BF16 = jnp.bfloat16
HIGHEST = lax.Precision.HIGHEST

D_MODEL = 1024
NORM_EPS = 1e-6

RET_HEADS = 4
RET_DK = D_MODEL // RET_HEADS
RET_DV = 2 * D_MODEL // RET_HEADS
RET_CHUNK = 128
RET_ROPE_BASE = 10000.0
RET_GN_EPS = 1e-5

GDN_HEADS = 8
GDN_DK = D_MODEL // GDN_HEADS
GDN_DV = D_MODEL // GDN_HEADS
GDN_CONV = 4
GDN_CHUNK = 64
GDN_QKV = GDN_HEADS * (2 * GDN_DK + GDN_DV)

RWKV_HEAD = 64
RWKV_HEADS = D_MODEL // RWKV_HEAD
RWKV_GN_EPS = 64e-5

PEER_KEYS = 128
PEER_HEADS = 8
PEER_DQ = 256
PEER_TOPK = 16
PEER_SEL = PEER_HEADS * PEER_TOPK

LANES = 128
SUBLANES = 8
VMEM_LIMIT = 56 * 1024 * 1024


def _cparams(sem):
    return pltpu.CompilerParams(dimension_semantics=sem, vmem_limit_bytes=VMEM_LIMIT)


def _bdot(a, b):
    return jnp.dot(a.astype(BF16), b.astype(BF16), preferred_element_type=F32)


def _bdot_nt(a, b):
    return lax.dot_general(a.astype(BF16), b.astype(BF16), (((1,), (1,)), ((), ())),
                           preferred_element_type=F32)


def _bdot_tn(a, b):
    return lax.dot_general(a.astype(BF16), b.astype(BF16), (((0,), (0,)), ((), ())),
                           preferred_element_type=F32)


def _hdot(a, b):
    return jnp.dot(a, b, preferred_element_type=F32, precision=HIGHEST)


def _sigmoid(x):
    return 1.0 / (1.0 + jnp.exp(-x))


def _silu(x):
    return x * _sigmoid(x)


def _softplus(x):
    return jnp.maximum(x, 0.0) + jnp.log1p(jnp.exp(-jnp.abs(x)))


def _norm_mod(x, g, sc, sh):
    ms = jnp.mean(x * x, axis=-1, keepdims=True)
    return (x * lax.rsqrt(ms + NORM_EPS) * g) * (1.0 + sc) + sh


def _adaln_kernel(c_ref, w_ref, b_ref, o_ref):
    cond = _silu(c_ref[...])
    o_ref[0] = _bdot(cond, w_ref[0]) + b_ref[0]


def _adaln(c, ada_w, ada_b):
    depth, d, n = ada_w.shape
    b = c.shape[0]
    tn = 1024
    return pl.pallas_call(
        _adaln_kernel,
        out_shape=jax.ShapeDtypeStruct((depth, b, n), F32),
        grid=(depth, n // tn),
        in_specs=[
            pl.BlockSpec((b, d), lambda l, j: (0, 0)),
            pl.BlockSpec((1, d, tn), lambda l, j: (l, 0, j)),
            pl.BlockSpec((1, 1, tn), lambda l, j: (l, 0, j)),
        ],
        out_specs=pl.BlockSpec((1, b, tn), lambda l, j: (l, 0, j)),
        compiler_params=_cparams(("parallel", "parallel")),
        name="adaln",
    )(c, ada_w, ada_b.reshape(depth, 1, n))


def _nm_matmul_kernel(x_ref, g_ref, sc_ref, sh_ref, w_ref, o_ref, *h_ref):
    h = _norm_mod(x_ref[...], g_ref[...], sc_ref[0], sh_ref[0])
    o_ref[...] = jnp.dot(h.astype(BF16), w_ref[...], preferred_element_type=F32).astype(o_ref.dtype)
    if h_ref:
        h_ref[0][...] = h


def _nm_matmul(x2d, g, sc, sh, w, *, rows_per_batch, tm=512, tn=None, emit_h=False):
    m, d = x2d.shape
    n = w.shape[1]
    nb = sc.shape[0]
    if tn is None:
        tn = n
    tpb = rows_per_batch // tm
    out_shape = [jax.ShapeDtypeStruct((m, n), F32)]
    out_specs = [pl.BlockSpec((tm, tn), lambda j, i: (i, j))]
    if emit_h:
        assert tn == n
        out_shape.append(jax.ShapeDtypeStruct((m, d), F32))
        out_specs.append(pl.BlockSpec((tm, d), lambda j, i: (i, 0)))
    res = pl.pallas_call(
        _nm_matmul_kernel,
        out_shape=out_shape,
        grid=(n // tn, m // tm),
        in_specs=[
            pl.BlockSpec((tm, d), lambda j, i: (i, 0)),
            pl.BlockSpec((1, d), lambda j, i: (0, 0)),
            pl.BlockSpec((1, 1, d), lambda j, i: (i // tpb, 0, 0)),
            pl.BlockSpec((1, 1, d), lambda j, i: (i // tpb, 0, 0)),
            pl.BlockSpec((d, tn), lambda j, i: (0, j)),
        ],
        out_specs=out_specs,
        compiler_params=_cparams(("parallel", "parallel")),
        name="norm_mod_matmul",
    )(x2d, g.reshape(1, d), sc.reshape(nb, 1, d), sh.reshape(nb, 1, d), w)
    return res if emit_h else res[0]


def _mm_res_kernel(*refs, has_mul):
    if has_mul:
        a_ref, m_ref, w_ref, r_ref, gt_ref, o_ref = refs
        a = a_ref[...] * m_ref[...]
    else:
        a_ref, w_ref, r_ref, gt_ref, o_ref = refs
        a = a_ref[...]
    y = jnp.dot(a.astype(BF16), w_ref[...], preferred_element_type=F32)
    o_ref[...] = r_ref[...] + gt_ref[0] * y


def _mm_res(a, w, res, gt, *, rows_per_batch, mul=None, tm=512):
    m, k = a.shape
    n = w.shape[1]
    nb = gt.shape[0]
    tpb = rows_per_batch // tm
    ins = [a]
    specs = [pl.BlockSpec((tm, k), lambda i: (i, 0))]
    if mul is not None:
        ins.append(mul)
        specs.append(pl.BlockSpec((tm, k), lambda i: (i, 0)))
    ins += [w, res, gt.reshape(nb, 1, n)]
    specs += [
        pl.BlockSpec((k, n), lambda i: (0, 0)),
        pl.BlockSpec((tm, n), lambda i: (i, 0)),
        pl.BlockSpec((1, 1, n), lambda i: (i // tpb, 0, 0)),
    ]
    return pl.pallas_call(
        functools.partial(_mm_res_kernel, has_mul=mul is not None),
        out_shape=jax.ShapeDtypeStruct((m, n), F32),
        grid=(m // tm,),
        in_specs=specs,
        out_specs=pl.BlockSpec((tm, n), lambda i: (i, 0)),
        compiler_params=_cparams(("parallel",)),
        name="matmul_residual",
    )(*ins)


def _ret_kernel(q_ref, k_ref, v_ref, gate_ref, cos_ref, sin_ref, dintra_ref, dq_ref, dk_ref, dchunk_ref,
                gng_ref, gnb_ref, o_ref, state_ref):
    H, dk, dv = RET_HEADS, RET_DK, RET_DV
    half = dk // 2

    @pl.when(pl.program_id(1) == 0)
    def _():
        state_ref[...] = jnp.zeros_like(state_ref)

    cos = cos_ref[...]
    sin = sin_ref[...]

    def rot(ref, h):
        x1 = ref[0, :, h * dk:h * dk + half]
        x2 = ref[0, :, h * dk + half:(h + 1) * dk]
        return jnp.concatenate([x1 * cos - x2 * sin, x1 * sin + x2 * cos], axis=-1)

    for h in range(H):
        q = rot(q_ref, h)
        k = rot(k_ref, h) * (dk ** -0.5)
        v = v_ref[0, :, h * dv:(h + 1) * dv]
        scores = _bdot_nt(q, k) * dintra_ref[h]
        st = state_ref[h]
        o = _bdot(scores, v) + _bdot(q, st) * dq_ref[h]
        state_ref[h] = st * dchunk_ref[h] + _bdot_tn(k * dk_ref[h], v)
        mu = jnp.mean(o, axis=-1, keepdims=True)
        var = jnp.mean(jnp.square(o - mu), axis=-1, keepdims=True)
        on = (o - mu) * lax.rsqrt(var + RET_GN_EPS) * gng_ref[h] + gnb_ref[h]
        g = gate_ref[0, :, h * dv:(h + 1) * dv]
        o_ref[0, :, h * dv:(h + 1) * dv] = (_silu(g) * on).astype(o_ref.dtype)


def _retention_scan(proj, gn_g, gn_b):
    B, S, _ = proj.shape
    H, dk, dv, C = RET_HEADS, RET_DK, RET_DV, RET_CHUNK
    half = dk // 2
    N = S // C
    inv_freq = RET_ROPE_BASE ** (-jnp.arange(half, dtype=F32) / half)
    ang = jnp.arange(S, dtype=F32)[:, None] * inv_freq[None, :]
    cos, sin = jnp.cos(ang), jnp.sin(ang)
    log_gamma = jnp.log1p(-jnp.exp2(-5.0 - jnp.arange(H, dtype=F32)))
    idx = jnp.arange(C, dtype=F32)
    diff = idx[:, None] - idx[None, :]
    causal = diff >= 0
    d_intra = jnp.where(causal[None], jnp.exp(jnp.where(causal, diff, 0.0)[None] * log_gamma[:, None, None]), 0.0)
    d_q = jnp.exp((idx[None, :] + 1.0) * log_gamma[:, None])[:, :, None]
    d_k = jnp.exp((C - 1.0 - idx)[None, :] * log_gamma[:, None])[:, :, None]
    d_chunk = jnp.exp(C * log_gamma)[:, None, None]
    qw, vw = H * dk, H * dv
    return pl.pallas_call(
        _ret_kernel,
        out_shape=jax.ShapeDtypeStruct((B, S, vw), BF16),
        grid=(B, N),
        in_specs=[
            pl.BlockSpec((1, C, qw), lambda b, n: (b, n, 0)),
            pl.BlockSpec((1, C, qw), lambda b, n: (b, n, 1)),
            pl.BlockSpec((1, C, vw), lambda b, n: (b, n, 1)),
            pl.BlockSpec((1, C, vw), lambda b, n: (b, n, 2)),
            pl.BlockSpec((C, half), lambda b, n: (n, 0)),
            pl.BlockSpec((C, half), lambda b, n: (n, 0)),
            pl.BlockSpec((H, C, C), lambda b, n: (0, 0, 0)),
            pl.BlockSpec((H, C, 1), lambda b, n: (0, 0, 0)),
            pl.BlockSpec((H, C, 1), lambda b, n: (0, 0, 0)),
            pl.BlockSpec((H, 1, 1), lambda b, n: (0, 0, 0)),
            pl.BlockSpec((H, 1, dv), lambda b, n: (0, 0, 0)),
            pl.BlockSpec((H, 1, dv), lambda b, n: (0, 0, 0)),
        ],
        out_specs=pl.BlockSpec((1, C, vw), lambda b, n: (b, n, 0)),
        scratch_shapes=[pltpu.VMEM((H, dk, dv), F32)],
        compiler_params=_cparams(("parallel", "arbitrary")),
        name="retention_scan",
    )(proj, proj, proj, proj, cos, sin, d_intra, d_q, d_k, d_chunk,
      gn_g.reshape(H, 1, dv), gn_b.reshape(H, 1, dv))


def _shift_rows(cur, prev8, s):
    rows = lax.broadcasted_iota(jnp.int32, cur.shape, 0)
    rolled = pltpu.roll(cur, s, axis=0)
    head = pltpu.roll(prev8, s, axis=0)
    head = jnp.concatenate([head, jnp.zeros((cur.shape[0] - SUBLANES, cur.shape[1]), cur.dtype)], axis=0)
    return jnp.where(rows < s, head, rolled)


def _cumsum_rows(x):
    rows = lax.broadcasted_iota(jnp.int32, x.shape, 0)
    s = 1
    while s < x.shape[0]:
        x = x + jnp.where(rows >= s, pltpu.roll(x, s, axis=0), 0.0)
        s *= 2
    return x


def _unit_lower_inverse(L):
    C = L.shape[0]
    eye = (lax.broadcasted_iota(jnp.int32, (C, C), 0) == lax.broadcasted_iota(jnp.int32, (C, C), 1)).astype(F32)
    p = -L
    inv = eye + p
    s = 2
    while s < C:
        p = _hdot(p, p)
        inv = inv + _hdot(inv, p)
        s *= 2
    return inv


def _gdn_kernel(qkv_ref, gate_ref, ab_ref, cw_ref, alog_ref, dtb_ref, ng_ref, o_ref, state_ref, prev_ref):
    C, dk, H = GDN_CHUNK, GDN_DK, GDN_HEADS

    @pl.when(pl.program_id(1) == 0)
    def _():
        state_ref[...] = jnp.zeros_like(state_ref)
        prev_ref[...] = jnp.zeros_like(prev_ref)

    def conv_silu(col):
        cur = qkv_ref[0, :, col * dk:(col + 1) * dk]
        prev8 = prev_ref[:, col * dk:(col + 1) * dk]
        cw = cw_ref[:, col * dk:(col + 1) * dk]
        acc = cur * cw[GDN_CONV - 1:GDN_CONV]
        for s in range(1, GDN_CONV):
            acc = acc + _shift_rows(cur, prev8, s) * cw[GDN_CONV - 1 - s:GDN_CONV - s]
        prev_ref[:, col * dk:(col + 1) * dk] = cur[C - SUBLANES:]
        return _silu(acc)

    ab = ab_ref[0]
    g_all = -jnp.exp(alog_ref[...]) * _softplus(ab + dtb_ref[...])
    beta_all = _sigmoid(ab)
    ri = lax.broadcasted_iota(jnp.int32, (C, C), 0)
    ci = lax.broadcasted_iota(jnp.int32, (C, C), 1)
    incl = ri >= ci

    for h in range(H):
        q = conv_silu(h)
        k = conv_silu(H + h)
        v = conv_silu(2 * H + h)
        q = q * lax.rsqrt(jnp.sum(q * q, axis=-1, keepdims=True) + 1e-6) * (dk ** -0.5)
        k = k * lax.rsqrt(jnp.sum(k * k, axis=-1, keepdims=True) + 1e-6)
        g = g_all[:, h:h + 1]
        beta = beta_all[:, H + h:H + h + 1]

        cum = _cumsum_rows(jnp.broadcast_to(g, (C, LANES)))
        cum_c = cum[:, :1]
        cum_last = cum[C - 1:C, :1]
        cum_r = cum.T[:C, :]
        decay = jnp.where(incl, jnp.exp(jnp.where(incl, cum[:, :C] - cum_r, 0.0)), 0.0)
        L = jnp.where(ri > ci, _bdot_nt(k, k) * decay, 0.0) * beta
        rhs = jnp.concatenate([k * (beta * jnp.exp(cum_c)), v * beta], axis=-1)
        sol = _hdot(_unit_lower_inverse(L), rhs)
        w_c, u_c = sol[:, :dk], sol[:, dk:]
        a_qk = _bdot_nt(q, k) * decay
        q_dec = q * jnp.exp(cum_c)
        k_dec = k * jnp.exp(cum_last - cum_c)

        st = state_ref[h]
        u = u_c - _bdot(w_c, st)
        o = _bdot(q_dec, st) + _bdot(a_qk, u)
        state_ref[h] = st * jnp.exp(cum_last) + _bdot_tn(k_dec, u)

        ms = jnp.mean(o * o, axis=-1, keepdims=True)
        o = o * lax.rsqrt(ms + NORM_EPS) * ng_ref[...]
        o_ref[0, :, h * dk:(h + 1) * dk] = (o * _silu(gate_ref[0, :, h * dk:(h + 1) * dk])).astype(o_ref.dtype)


def _gdn_scan(proj, proj_ab, conv_w, a_log, dt_bias, norm_g):
    B, S, _ = proj.shape
    H, dk, C = GDN_HEADS, GDN_DK, GDN_CHUNK
    N = S // C
    pad = lambda t: jnp.pad(t.astype(F32), (0, LANES - H)).reshape(1, LANES)
    row = pl.BlockSpec((1, LANES), lambda b, n: (0, 0))
    return pl.pallas_call(
        _gdn_kernel,
        out_shape=jax.ShapeDtypeStruct((B, S, H * dk), BF16),
        grid=(B, N),
        in_specs=[pl.BlockSpec((1, C, 3 * H * dk), lambda b, n: (b, n, 0)),
                  pl.BlockSpec((1, C, H * dk), lambda b, n: (b, n, 3)),
                  pl.BlockSpec((1, C, LANES), lambda b, n: (b, n, 0)),
                  pl.BlockSpec((GDN_CONV, 3 * H * dk), lambda b, n: (0, 0)),
                  row, row, row],
        out_specs=pl.BlockSpec((1, C, H * dk), lambda b, n: (b, n, 0)),
        scratch_shapes=[pltpu.VMEM((H, dk, dk), F32), pltpu.VMEM((SUBLANES, 3 * H * dk), F32)],
        compiler_params=_cparams(("parallel", "arbitrary")),
        name="gdn_scan",
    )(proj, proj, proj_ab, conv_w, pad(a_log), pad(dt_bias), norm_g.reshape(1, dk))


def _rwkv_proj_kernel(x_ref, xp_ref, g_ref, sc_ref, sh_ref, mu_ref, wr_ref, wk_ref, wv_ref, w1_ref, w2_ref,
                      a1_ref, a2_ref, g1_ref, g2_ref, w0_ref, a0_ref, kk_ref, ka_ref,
                      r_o, dec_o, k_o, v_o, kk_o, a_o, g_o, *, tiles_per_seq):
    h = _norm_mod(x_ref[...], g_ref[...], sc_ref[0], sh_ref[0])
    hp8 = _norm_mod(xp_ref[...], g_ref[...], sc_ref[0], sh_ref[0])
    seq_start = pl.program_id(0) % tiles_per_seq == 0
    first = jnp.where(seq_start, 0.0, hp8[SUBLANES - 1:SUBLANES, :])
    rows = lax.broadcasted_iota(jnp.int32, h.shape, 0)
    xx = jnp.where(rows == 0, first, pltpu.roll(h, 1, axis=0)) - h
    mix = lambda j: h + xx * mu_ref[j:j + 1, :]
    r = _bdot(mix(0), wr_ref[...])
    lw = w0_ref[...] + _bdot(jnp.tanh(_bdot(mix(1), w1_ref[...])), w2_ref[...])
    k = _bdot(mix(2), wk_ref[...])
    v = _bdot(mix(3), wv_ref[...])
    a = _sigmoid(a0_ref[...] + _bdot(_bdot(mix(4), a1_ref[...]), a2_ref[...]))
    g = _bdot(_sigmoid(_bdot(mix(5), g1_ref[...])), g2_ref[...])
    w = -_softplus(-lw) - 0.5
    r_o[...] = r
    dec_o[...] = jnp.exp(-jnp.exp(w))
    k_o[...] = k * (1.0 + (a - 1.0) * ka_ref[...])
    v_o[...] = v
    kk_o[...] = k * kk_ref[...]
    a_o[...] = a
    g_o[...] = g


def _pad_cols(w, n):
    return jnp.pad(w, ((0, 0), (0, n - w.shape[1])))


def _pad_rows(w, n):
    return jnp.pad(w, ((0, n - w.shape[0]), (0, 0)))


def _rwkv_proj(x2d, g, sc, sh, mu, w_r, w_k, w_v, w1, w2, a1, a2, g1, g2, w0, a0, k_k, k_a, *, rows_per_batch, tm=256):
    m, d = x2d.shape
    nb = sc.shape[0]
    tpb = rows_per_batch // tm
    lora_w = LANES * pl.cdiv(w1.shape[1], LANES)
    lora_g = LANES * pl.cdiv(g1.shape[1], LANES)
    bf = lambda t: t.astype(BF16)
    full = lambda a: pl.BlockSpec(a.shape, lambda i: (0,) * a.ndim)
    row = lambda t: t.reshape(1, d)
    ws = [bf(w_r), bf(w_k), bf(w_v), bf(_pad_cols(w1, lora_w)), bf(_pad_rows(w2, lora_w)),
          bf(_pad_cols(a1, lora_w)), bf(_pad_rows(a2, lora_w)), bf(_pad_cols(g1, lora_g)), bf(_pad_rows(g2, lora_g)),
          row(w0), row(a0), row(k_k), row(k_a)]
    tile = pl.BlockSpec((tm, d), lambda i: (i, 0))
    return pl.pallas_call(
        functools.partial(_rwkv_proj_kernel, tiles_per_seq=tpb),
        out_shape=[jax.ShapeDtypeStruct((m, d), F32)] * 7,
        grid=(m // tm,),
        in_specs=[
            tile,
            pl.BlockSpec((SUBLANES, d), lambda i: (jnp.maximum(i * (tm // SUBLANES) - 1, 0), 0)),
            pl.BlockSpec((1, d), lambda i: (0, 0)),
            pl.BlockSpec((1, 1, d), lambda i: (i // tpb, 0, 0)),
            pl.BlockSpec((1, 1, d), lambda i: (i // tpb, 0, 0)),
            full(mu),
        ] + [full(w) for w in ws],
        out_specs=[tile] * 7,
        compiler_params=_cparams(("parallel",)),
        name="rwkv_proj",
    )(x2d, x2d, g.reshape(1, d), sc.reshape(nb, 1, d), sh.reshape(nb, 1, d), mu, *ws)


RWKV_VUNROLL = 4


def _rwkv_scan_kernel(r_ref, w_ref, k_ref, v_ref, kk_ref, a_ref, rk_ref, lng_ref, lnb_ref, y_ref, state_ref, yrow_ref):
    n = RWKV_HEAD

    @pl.when(pl.program_id(0) == 0)
    def _():
        state_ref[...] = jnp.zeros_like(state_ref)

    def step(t, carry):
        r, w, k, kkr, a = r_ref[t], w_ref[t], k_ref[t], kk_ref[t], a_ref[t]
        kk = kkr * lax.rsqrt(jnp.sum(kkr * kkr, axis=0, keepdims=True) + 1e-6)
        nkk = -kk
        kka = kk * a

        def vloop(vb, c):
            for j in range(RWKV_VUNROLL):
                vi = vb * RWKV_VUNROLL + j
                sv = state_ref[vi]
                sa = jnp.sum(sv * nkk, axis=0, keepdims=True)
                vrow = v_ref[t, pl.ds(vi, 1), :]
                sn = sv * w + sa * kka + vrow * k
                state_ref[vi] = sn
                yrow_ref[pl.ds(vi, 1), :] = jnp.sum(sn * r, axis=0, keepdims=True)
            return c

        lax.fori_loop(0, n // RWKV_VUNROLL, vloop, 0)
        y = yrow_ref[...]
        mu = jnp.mean(y, axis=0, keepdims=True)
        var = jnp.mean(jnp.square(y - mu), axis=0, keepdims=True)
        yn = (y - mu) * lax.rsqrt(var + RWKV_GN_EPS) * lng_ref[...] + lnb_ref[...]
        bonus = jnp.sum(r * k * rk_ref[...], axis=0, keepdims=True)
        y_ref[t] = yn + bonus * v_ref[t]
        return carry

    lax.fori_loop(0, r_ref.shape[0], step, 0)


def _rwkv_scan(r, dec, k, v, kk, a, r_k, ln_g, ln_b, *, tc=32):
    B, S, D = r.shape
    H, n = RWKV_HEADS, RWKV_HEAD
    lanes = B * H
    to_scan = lambda t: jnp.transpose(t.reshape(B, S, H, n), (1, 3, 0, 2)).reshape(S, n, lanes)
    per_head = lambda p: jnp.tile(p.T, (1, B))
    blk = pl.BlockSpec((tc, n, lanes), lambda i: (i, 0, 0))
    cst = pl.BlockSpec((n, lanes), lambda i: (0, 0))
    y = pl.pallas_call(
        _rwkv_scan_kernel,
        out_shape=jax.ShapeDtypeStruct((S, n, lanes), F32),
        grid=(S // tc,),
        in_specs=[blk] * 6 + [cst] * 3,
        out_specs=blk,
        scratch_shapes=[pltpu.VMEM((n, n, lanes), F32), pltpu.VMEM((n, lanes), F32)],
        compiler_params=_cparams(("arbitrary",)),
        name="rwkv_scan",
    )(to_scan(r), to_scan(dec), to_scan(k), to_scan(v), to_scan(kk), to_scan(a),
      per_head(r_k), per_head(ln_g), per_head(ln_b))
    return jnp.transpose(y.reshape(S, n, B, H), (2, 0, 3, 1)).reshape(B, S, D)


def _rwkv_mixer(x, g, sc, sh, gt, mu, w_r, w_k, w_v, w_o, w0, w1, w2, a0, a1, a2, g1, g2, k_k, k_a, r_k, ln_g, ln_b):
    B, S, D = x.shape
    x2d = x.reshape(B * S, D)
    tm = min(256, S)
    r, dec, k, v, kk, a, gg = _rwkv_proj(x2d, g, sc, sh, mu, w_r, w_k, w_v, w1, w2, a1, a2, g1, g2, w0, a0, k_k, k_a,
                                         rows_per_batch=S, tm=tm)
    sh3 = lambda t: t.reshape(B, S, D)
    y = _rwkv_scan(sh3(r), sh3(dec), sh3(k), sh3(v), sh3(kk), sh3(a), r_k, ln_g, ln_b, tc=min(32, S))
    out = _mm_res(y.reshape(B * S, D), w_o.astype(BF16), x2d, gt, rows_per_batch=S, mul=gg, tm=min(512, S))
    return out.reshape(B, S, D)


def _rwkv_mixer_test(x, g, sc, sh, p):
    gt = jnp.ones_like(sc)
    return _rwkv_mixer(x, g, sc, sh, gt, p['mu'], p['w_r'], p['w_k'], p['w_v'], p['w_o'], p['w0'], p['w1'], p['w2'],
                       p['a0'], p['a1'], p['a2'], p['g1'], p['g2'], p['k_k'], p['k_a'], p['r_k'], p['ln_g'],
                       p['ln_b']) - x


def _topk_rows(s, k, rows=None):
    if rows is None:
        rows = lax.broadcasted_iota(jnp.int32, s.shape, 0)
    n = jnp.iinfo(jnp.int32).max
    vals, ids = [], []
    for _ in range(k):
        m = jnp.max(s, axis=0, keepdims=True)
        idx = jnp.min(jnp.where(s == m, rows, n), axis=0, keepdims=True)
        vals.append(m)
        ids.append(idx)
        s = jnp.where(rows == idx, -jnp.inf, s)
    return jnp.concatenate(vals, axis=0), jnp.concatenate(ids, axis=0)


def _take_rows(table, pos):
    out = jnp.zeros(pos.shape, table.dtype)
    for m in range(table.shape[0]):
        out = jnp.where(pos == m, table[m:m + 1, :], out)
    return out


_PEER_CAND = [(i, PEER_TOPK // (i + 1)) for i in range(PEER_TOPK)]
_PEER_NCAND = sum(n for _, n in _PEER_CAND)
_PEER_NCAND_PAD = SUBLANES * pl.cdiv(_PEER_NCAND, SUBLANES)


def _peer_cand_codes():
    codes = [i * PEER_TOPK + j for i, n in _PEER_CAND for j in range(n)]
    codes += [PEER_TOPK * PEER_TOPK + p for p in range(_PEER_NCAND_PAD - _PEER_NCAND)]
    return jnp.broadcast_to(jnp.asarray(codes, jnp.int32)[:, None], (_PEER_NCAND_PAD, LANES))


def _peer_route_kernel(q_ref, keys_ref, codes_ref, idx_o, gate_o):
    K, half = PEER_TOPK, PEER_DQ // 2

    tm = q_ref.shape[0]
    G = 2
    codes = jnp.concatenate([codes_ref[...]] * (G * tm // LANES), axis=1)
    pad = jnp.full((_PEER_NCAND_PAD - _PEER_NCAND, tm), -jnp.inf, F32)

    def group(hg, carry):
        ss = []
        for dh in range(G):
            for p in range(2):
                c = pl.multiple_of((2 * (G * hg + dh) + p) * half, half)
                ss.append(_bdot_nt(keys_ref[G * hg + dh, p], q_ref[:, pl.ds(c, half)]))
        vals, ids = _topk_rows(jnp.concatenate(ss, axis=1), K)
        part = lambda t, j: t[:, j * tm:(j + 1) * tm]
        cands = []
        for dh in range(G):
            va, vb = part(vals, 2 * dh), part(vals, 2 * dh + 1)
            cands.append(jnp.concatenate([va[i:i + 1, :] + vb[:n, :] for i, n in _PEER_CAND] + [pad], axis=0))
        best, pos = _topk_rows(jnp.concatenate(cands, axis=1), K, codes)
        for dh in range(G):
            ia, ib = part(ids, 2 * dh), part(ids, 2 * dh + 1)
            ps, bs = part(pos, dh), part(best, dh)
            expert = _take_rows(ia, ps // K) * PEER_KEYS + _take_rows(ib, ps % K)
            e = jnp.exp(bs - bs[0:1, :])
            r0 = pl.multiple_of((G * hg + dh) * K, K)
            idx_o[pl.ds(r0, K), :] = expert
            gate_o[pl.ds(r0, K), :] = e / jnp.sum(e, axis=0, keepdims=True)
        return carry

    lax.fori_loop(0, PEER_HEADS // G, group, 0)


def _peer_route(q, sub_keys, *, tm=256):
    t, n = q.shape
    blk = pl.BlockSpec((PEER_SEL, tm), lambda i: (0, i))
    return pl.pallas_call(
        _peer_route_kernel,
        out_shape=[jax.ShapeDtypeStruct((PEER_SEL, t), jnp.int32), jax.ShapeDtypeStruct((PEER_SEL, t), F32)],
        grid=(t // tm,),
        in_specs=[pl.BlockSpec((tm, n), lambda i: (i, 0)),
                  pl.BlockSpec(sub_keys.shape, lambda i: (0, 0, 0, 0)),
                  pl.BlockSpec((_PEER_NCAND_PAD, LANES), lambda i: (0, 0))],
        out_specs=[blk, blk],
        compiler_params=_cparams(("parallel",)),
        name="peer_route",
    )(q, sub_keys.astype(BF16), _peer_cand_codes())


def _pack_uv(u, v):
    e, d = u.shape
    ub = lax.bitcast_convert_type(u.astype(BF16), jnp.uint16).astype(jnp.uint32)
    vb = lax.bitcast_convert_type(v.astype(BF16), jnp.uint16).astype(jnp.uint32)
    return lax.bitcast_convert_type((vb << 16) | ub, jnp.int32).reshape(e, d // LANES, LANES)


PEER_NBUF = 4


def _gelu(x):
    return 0.5 * x * (1.0 + lax.erf(x * (2.0 ** -0.5)))


def _peer_eval(packed, x, gate):
    u = lax.bitcast_convert_type(packed << 16, F32)
    v = lax.bitcast_convert_type(packed & jnp.int32(-65536), F32)
    act = jnp.sum(jnp.sum(u * x[None], axis=1), axis=1, keepdims=True)
    wgt = gate * _gelu(act)
    return jnp.sum(v * wgt[:, :, None], axis=0)


def _gate_column(gate_ref, t):
    lane = lax.broadcasted_iota(jnp.int32, gate_ref.shape, 1)
    return jnp.sum(jnp.where(lane == t, gate_ref[...], 0.0), axis=1, keepdims=True)


def _peer_expert_kernel(idx_hbm, gate_ref, h_ref, xres_ref, gt_ref, uv_hbm, o_ref, idx_smem, buf, sem_idx, sem, *,
                        tok0):
    tb = h_ref.shape[0]
    nsel = PEER_SEL
    base = (pl.program_id(0) * tb + tok0) * nsel
    cp = pltpu.make_async_copy(idx_hbm.at[pl.ds(base, tb * nsel)], idx_smem, sem_idx)
    cp.start()
    cp.wait()

    def issue(t, slot):
        for k in range(nsel):
            e = idx_smem[t * nsel + k]
            pltpu.make_async_copy(uv_hbm.at[e], buf.at[slot, k], sem.at[slot]).start()

    def wait(slot):
        pltpu.make_async_copy(uv_hbm.at[pl.ds(0, nsel)], buf.at[slot], sem.at[slot]).wait()

    for t0 in range(PEER_NBUF - 1):
        issue(t0, t0)

    def body(t, carry):
        slot = t % PEER_NBUF
        nxt = t + PEER_NBUF - 1

        @pl.when(nxt < tb)
        def _():
            issue(nxt, nxt % PEER_NBUF)

        wait(slot)
        out = _peer_eval(buf[slot], h_ref[t], _gate_column(gate_ref, t))
        o_ref[t] = xres_ref[t] + gt_ref[0] * out
        return carry

    lax.fori_loop(0, tb, body, 0)


def _peer_expert(idx_flat, gate_t, h3, xres3, gt3, uv, *, rows_per_batch, tok0, ntok, tb=128):
    _, c, _ = h3.shape
    tpb = rows_per_batch // tb
    b0 = tok0 // tb
    tok = pl.BlockSpec((tb, c, LANES), lambda i: (i + b0, 0, 0))
    return pl.pallas_call(
        functools.partial(_peer_expert_kernel, tok0=tok0),
        out_shape=jax.ShapeDtypeStruct((ntok, c, LANES), F32),
        grid=(ntok // tb,),
        in_specs=[
            pl.BlockSpec(memory_space=pl.ANY),
            pl.BlockSpec((PEER_SEL, tb), lambda i: (0, i + b0)),
            tok, tok,
            pl.BlockSpec((1, c, LANES), lambda i: ((i + b0) // tpb, 0, 0)),
            pl.BlockSpec(memory_space=pl.ANY),
        ],
        out_specs=pl.BlockSpec((tb, c, LANES), lambda i: (i, 0, 0)),
        scratch_shapes=[
            pltpu.SMEM((tb * PEER_SEL,), jnp.int32),
            pltpu.VMEM((PEER_NBUF, PEER_SEL, c, LANES), jnp.int32),
            pltpu.SemaphoreType.DMA,
            pltpu.SemaphoreType.DMA((PEER_NBUF,)),
        ],
        compiler_params=_cparams(("arbitrary",)),
        name="peer_expert",
    )(idx_flat, gate_t, h3, xres3, gt3, uv)


SC_CORES = 2
SC_SUBCORES = 16
SC_WORKERS = SC_CORES * SC_SUBCORES
SC_LANES = 16
SC_TOKENS = 8
SC_GATHER_ROWS = 32
SC_CHUNKS = 8
_SC_PARAMS = pltpu.CompilerParams(needs_layout_passes=False)


def _pack_pairs(t):
    half = t.shape[1] // 2
    b = lax.bitcast_convert_type(t.astype(BF16), jnp.uint16).astype(jnp.uint32)
    return lax.bitcast_convert_type((b[:, half:] << 16) | b[:, :half], jnp.int32)


def _sc_unpack(w):
    return (lax.bitcast_convert_type(w << 16, F32), lax.bitcast_convert_type(w & jnp.int32(-65536), F32))


def _sc_mesh():
    return plsc.VectorSubcoreMesh(core_axis_name="core", subcore_axis_name="subcore")


def _sc_act(u2, idx_flat, h_flat, *, tok0, ntok, d):
    nsel, L, G, CG, TBK = PEER_SEL, SC_LANES, SC_GATHER_ROWS, SC_CHUNKS, SC_TOKENS
    half = d // 2
    tpw = ntok // SC_WORKERS
    ng = nsel // G
    ngt = TBK * ng

    @functools.partial(
        pl.kernel, out_type=jax.ShapeDtypeStruct((ntok * nsel,), F32), mesh=_sc_mesh(),
        scratch_types=[pltpu.VMEM((TBK * nsel,), jnp.int32), pltpu.VMEM((TBK * d,), F32),
                       pltpu.VMEM((2, G, half), jnp.int32), pltpu.VMEM((TBK * nsel,), F32),
                       pltpu.VMEM((G * L,), F32), pltpu.SemaphoreType.DMA((2,))],
        compiler_params=_SC_PARAMS, name="peer_sc_act")
    def k(u_hbm, i_hbm, x_hbm, o_hbm, idx_v, x_v, buf, act_v, acc_v, sem):
        wid = lax.axis_index("core") * SC_SUBCORES + lax.axis_index("subcore")
        lanes = lax.iota(jnp.int32, L)

        def gather(gi, b):
            return pltpu.make_async_copy(u_hbm.at[idx_v.at[pl.ds(gi * G, G)]], buf.at[b], sem.at[b])

        @pl.loop(0, tpw // TBK)
        def _(bi):
            tl = wid * tpw + bi * TBK
            pltpu.sync_copy(i_hbm.at[pl.ds((tok0 + tl) * nsel, TBK * nsel)], idx_v)
            pltpu.sync_copy(x_hbm.at[pl.ds((tok0 + tl) * d, TBK * d)], x_v)
            gather(0, 0).start()

            @pl.loop(0, ngt, step=2)
            def _(g0):
                for b in range(2):
                    gi = g0 + b

                    @pl.when(gi + 1 < ngt)
                    def _():
                        gather(gi + 1, 1 - b).start()

                    gather(gi, b).wait()
                    xbase = (gi // ng) * d
                    for jg in range(half // (CG * L)):
                        xl = [x_v[pl.ds(pl.multiple_of(xbase + (jg * CG + c) * L, L), L)] for c in range(CG)]
                        xh = [x_v[pl.ds(pl.multiple_of(xbase + half + (jg * CG + c) * L, L), L)] for c in range(CG)]

                        def row(r, carry):
                            ps = []
                            for c in range(CG):
                                lo, hi = _sc_unpack(buf[b, r, pl.ds((jg * CG + c) * L, L)])
                                ps.append(lo * xl[c] + hi * xh[c])
                            while len(ps) > 1:
                                ps = [ps[i] + ps[i + 1] for i in range(0, len(ps), 2)]
                            off = pl.multiple_of(r * L, L)
                            if jg == 0:
                                acc_v[pl.ds(off, L)] = ps[0]
                            else:
                                acc_v[pl.ds(off, L)] = acc_v[pl.ds(off, L)] + ps[0]
                            return carry

                        plsc.parallel_loop(0, G, carry=jnp.int32(0))(row)
                    for part in range(G // L):
                        vec = jnp.zeros((L,), F32)
                        for r in range(L):
                            vec = jnp.where(lanes == r, jnp.sum(acc_v[pl.ds((part * L + r) * L, L)]), vec)
                        act_v[pl.ds(pl.multiple_of(gi * G + part * L, L), L)] = vec

            pltpu.sync_copy(act_v, o_hbm.at[pl.ds(tl * nsel, TBK * nsel)])

    return k(u2, idx_flat, h_flat)


def _sc_out(v2, idx_flat, w_flat, xres_flat, gt_flat, *, tok0, ntok, d, rows_per_batch):
    nsel, L, G, CG, TBK = PEER_SEL, SC_LANES, SC_GATHER_ROWS, SC_CHUNKS, SC_TOKENS
    half = d // 2
    tpw = ntok // SC_WORKERS
    ng = nsel // G
    ngt = TBK * ng

    @functools.partial(
        pl.kernel, out_type=jax.ShapeDtypeStruct((ntok * d,), F32), mesh=_sc_mesh(),
        scratch_types=[pltpu.VMEM((TBK * nsel,), jnp.int32), pltpu.VMEM((TBK * nsel,), F32),
                       pltpu.VMEM((2, G, half), jnp.int32), pltpu.VMEM((TBK * d,), F32),
                       pltpu.VMEM((TBK * d,), F32), pltpu.VMEM((d,), F32), pltpu.SemaphoreType.DMA((2,))],
        compiler_params=_SC_PARAMS, name="peer_sc_out")
    def k(v_hbm, i_hbm, w_hbm, xr_hbm, gt_hbm, o_hbm, idx_v, w_v, buf, out_v, xr_v, gt_v, sem):
        wid = lax.axis_index("core") * SC_SUBCORES + lax.axis_index("subcore")

        def gather(gi, b):
            return pltpu.make_async_copy(v_hbm.at[idx_v.at[pl.ds(gi * G, G)]], buf.at[b], sem.at[b])

        @pl.loop(0, tpw // TBK)
        def _(bi):
            tl = wid * tpw + bi * TBK
            pltpu.sync_copy(i_hbm.at[pl.ds((tok0 + tl) * nsel, TBK * nsel)], idx_v)
            pltpu.sync_copy(w_hbm.at[pl.ds(tl * nsel, TBK * nsel)], w_v)
            gather(0, 0).start()
            pltpu.sync_copy(xr_hbm.at[pl.ds((tok0 + tl) * d, TBK * d)], xr_v)
            pltpu.sync_copy(gt_hbm.at[pl.ds(((tok0 + tl) // rows_per_batch) * d, d)], gt_v)

            @pl.loop(0, TBK * d // L)
            def _(i):
                out_v[pl.ds(pl.multiple_of(i * L, L), L)] = jnp.zeros((L,), F32)

            @pl.loop(0, ngt, step=2)
            def _(g0):
                for b in range(2):
                    gi = g0 + b

                    @pl.when(gi + 1 < ngt)
                    def _():
                        gather(gi + 1, 1 - b).start()

                    gather(gi, b).wait()
                    obase = (gi // ng) * d
                    for jg in range(half // (CG * L)):
                        lo0 = obase + jg * CG * L
                        hi0 = lo0 + half
                        accs = tuple(out_v[pl.ds(pl.multiple_of(lo0 + c * L, L), L)] for c in range(CG)) + \
                            tuple(out_v[pl.ds(pl.multiple_of(hi0 + c * L, L), L)] for c in range(CG))

                        def row(r, accs):
                            wk = plsc.load_gather(w_v, [jnp.full((L,), gi * G + r, jnp.int32)])
                            new = list(accs)
                            for c in range(CG):
                                lo, hi = _sc_unpack(buf[b, r, pl.ds((jg * CG + c) * L, L)])
                                new[c] = new[c] + wk * lo
                                new[CG + c] = new[CG + c] + wk * hi
                            return tuple(new)

                        accs = lax.fori_loop(0, G, row, accs)
                        for c in range(CG):
                            out_v[pl.ds(pl.multiple_of(lo0 + c * L, L), L)] = accs[c]
                            out_v[pl.ds(pl.multiple_of(hi0 + c * L, L), L)] = accs[CG + c]

            @pl.loop(0, TBK * d // L)
            def _(i):
                off = pl.multiple_of(i * L, L)
                goff = pl.multiple_of((i % (d // L)) * L, L)
                out_v[pl.ds(off, L)] = xr_v[pl.ds(off, L)] + gt_v[pl.ds(goff, L)] * out_v[pl.ds(off, L)]

            pltpu.sync_copy(out_v, o_hbm.at[pl.ds(tl * d, TBK * d)])

    return k(v2, idx_flat, w_flat, xres_flat, gt_flat)


def _peer_weight_kernel(act_ref, gate_ref, o_ref):
    o_ref[...] = gate_ref[...] * _gelu(act_ref[...])


def _peer_weight(act, gate_tok, *, tok0, tm=1024):
    n, k = act.shape
    tm = min(tm, n)
    b0 = tok0 // tm
    return pl.pallas_call(
        _peer_weight_kernel,
        out_shape=jax.ShapeDtypeStruct((n, k), F32),
        grid=(n // tm,),
        in_specs=[pl.BlockSpec((tm, k), lambda i: (i, 0)), pl.BlockSpec((tm, k), lambda i: (i + b0, 0))],
        out_specs=pl.BlockSpec((tm, k), lambda i: (i, 0)),
        compiler_params=_cparams(("parallel",)),
        name="peer_weight",
    )(act, gate_tok)


PEER_SC_SHARE = (3, 4)
PEER_SC_CALLS = 2
PEER_TC_CALLS = 3


def _peer_experts(idx_flat, gate_t, h, xres, gt, uv, u2, v2, *, rows_per_batch):
    t, d = h.shape
    c = d // LANES
    tb = min(128, rows_per_batch)
    unit = SC_WORKERS * SC_TOKENS * PEER_SC_CALLS
    n_sc = t * PEER_SC_SHARE[0] // PEER_SC_SHARE[1] // unit * unit
    outs = []
    if n_sc:
        gate_tok = gate_t.T
        h_flat, xres_flat, gt_flat = h.reshape(-1), xres.reshape(-1), gt.reshape(-1)
        ch = n_sc // PEER_SC_CALLS
        for i in range(PEER_SC_CALLS):
            act = _sc_act(u2, idx_flat, h_flat, tok0=i * ch, ntok=ch, d=d)
            wgt = _peer_weight(act.reshape(ch, PEER_SEL), gate_tok, tok0=i * ch)
            out = _sc_out(v2, idx_flat, wgt.reshape(-1), xres_flat, gt_flat, tok0=i * ch, ntok=ch, d=d,
                          rows_per_batch=rows_per_batch)
            outs.append(out.reshape(ch, d))
    if n_sc < t:
        h3, xres3, gt3 = h.reshape(t, c, LANES), xres.reshape(t, c, LANES), gt.reshape(-1, c, LANES)
        blocks = (t - n_sc) // tb
        ncall = min(PEER_TC_CALLS, blocks)
        bounds = [n_sc + (blocks * j // ncall) * tb for j in range(ncall + 1)]
        for lo, hi in zip(bounds[:-1], bounds[1:]):
            outs.append(_peer_expert(idx_flat, gate_t, h3, xres3, gt3, uv, rows_per_batch=rows_per_batch, tok0=lo,
                                     ntok=hi - lo, tb=tb).reshape(hi - lo, d))
    return jnp.concatenate(outs, axis=0)


def _peer_ffn(x2d, g, sc, sh, gt, w_q, sub_keys, u, v, *, rows_per_batch):
    tm = min(512, rows_per_batch)
    q, h = _nm_matmul(x2d, g, sc, sh, w_q, rows_per_batch=rows_per_batch, tm=tm, emit_h=True)
    idx_t, gate_t = _peer_route(q, sub_keys, tm=min(256, rows_per_batch))
    return _peer_experts(idx_t.T.reshape(-1), gate_t, h, x2d, gt, _pack_uv(u, v), _pack_pairs(u), _pack_pairs(v),
                         rows_per_batch=rows_per_batch)


def _peer_test(x, g, sc, sh, w_q, sub_keys, u, v):
    B, S, D = x.shape
    x2d = x.reshape(B * S, D)
    return (_peer_ffn(x2d, g, sc, sh, jnp.ones_like(sc), w_q.astype(BF16), sub_keys, u, v,
                      rows_per_batch=S) - x2d).reshape(B, S, D)


def _final_norm_kernel(x_ref, g_ref, o_ref):
    x = x_ref[...]
    ms = jnp.mean(x * x, axis=-1, keepdims=True)
    o_ref[...] = x * lax.rsqrt(ms + NORM_EPS) * g_ref[...]


def _final_norm(x2d, g, *, tm=512):
    m, d = x2d.shape
    tile = pl.BlockSpec((tm, d), lambda i: (i, 0))
    return pl.pallas_call(
        _final_norm_kernel,
        out_shape=jax.ShapeDtypeStruct((m, d), F32),
        grid=(m // tm,),
        in_specs=[tile, pl.BlockSpec((1, d), lambda i: (0, 0))],
        out_specs=tile,
        compiler_params=_cparams(("parallel",)),
        name="final_norm",
    )(x2d, g.reshape(1, d))


def kernel(x, c, ada_w, ada_b, norm_mix_g, norm_ffn_g, final_norm_g, ret_w_in, ret_w_out, ret_gn_g, ret_gn_b, gdn_w_in, gdn_conv_w, gdn_a_log, gdn_dt_bias, gdn_norm_g, gdn_w_out, rwkv_mu, rwkv_w_r, rwkv_w_k, rwkv_w_v, rwkv_w_o, rwkv_w0, rwkv_w1, rwkv_w2, rwkv_a0, rwkv_a1, rwkv_a2, rwkv_g1, rwkv_g2, rwkv_k_k, rwkv_k_a, rwkv_r_k, rwkv_ln_g, rwkv_ln_b, peer_w_q, peer_sub_keys, peer_u, peer_v):
    B, S, D = x.shape
    T = B * S
    depth = ada_w.shape[0]
    bf = lambda t: t.astype(BF16)
    mod = _adaln(c, ada_w, ada_b)
    x2d = x.reshape(T, D)
    for layer in range(depth):
        sh_m, sc_m, gt_m, sh_f, sc_f, gt_f = [mod[layer, :, i * D:(i + 1) * D] for i in range(6)]
        g_mix = norm_mix_g[layer]
        kind, j = layer % 3, layer // 3
        if kind == 0:
            proj = _nm_matmul(x2d, g_mix, sc_m, sh_m, bf(ret_w_in[j]), rows_per_batch=S, tn=2048)
            o = _retention_scan(proj.reshape(B, S, -1), ret_gn_g[j], ret_gn_b[j])
            x2d = _mm_res(o.reshape(T, -1), bf(ret_w_out[j]), x2d, gt_m, rows_per_batch=S)
        elif kind == 1:
            w = gdn_w_in[j]
            wide = GDN_QKV + GDN_HEADS * GDN_DV
            proj = _nm_matmul(x2d, g_mix, sc_m, sh_m, bf(w[:, :wide]), rows_per_batch=S, tn=2048)
            proj_ab = _nm_matmul(x2d, g_mix, sc_m, sh_m, bf(_pad_cols(w[:, wide:], LANES)), rows_per_batch=S)
            o = _gdn_scan(proj.reshape(B, S, -1), proj_ab.reshape(B, S, -1), gdn_conv_w[j], gdn_a_log[j],
                          gdn_dt_bias[j], gdn_norm_g[j])
            x2d = _mm_res(o.reshape(T, -1), bf(gdn_w_out[j]), x2d, gt_m, rows_per_batch=S)
        else:
            x2d = _rwkv_mixer(x2d.reshape(B, S, D), g_mix, sc_m, sh_m, gt_m, rwkv_mu[j], rwkv_w_r[j], rwkv_w_k[j],
                              rwkv_w_v[j], rwkv_w_o[j], rwkv_w0[j], rwkv_w1[j], rwkv_w2[j], rwkv_a0[j], rwkv_a1[j],
                              rwkv_a2[j], rwkv_g1[j], rwkv_g2[j], rwkv_k_k[j], rwkv_k_a[j], rwkv_r_k[j],
                              rwkv_ln_g[j], rwkv_ln_b[j]).reshape(T, D)
        x2d = _peer_ffn(x2d, norm_ffn_g[layer], sc_f, sh_f, gt_f, bf(peer_w_q[layer]), peer_sub_keys[layer],
                        peer_u[layer], peer_v[layer], rows_per_batch=S)
    return _final_norm(x2d, final_norm_g).reshape(B, S, D)
```

```python
import functools
import math

import jax
import jax.numpy as jnp
from jax import lax
from jax.experimental import pallas as pl
from jax.experimental.pallas import tpu as pltpu
from jax.experimental.pallas import tpu_sc as plsc

F32 = jnp.float32
BF16 = jnp.bfloat16

D_MODEL = 1024
NORM_EPS = 1e-6

RET_HEADS = 4
RET_DK = D_MODEL // RET_HEADS
RET_DV = 2 * D_MODEL // RET_HEADS
RET_CHUNK = 128
RET_ROPE_BASE = 10000.0
RET_GN_EPS = 1e-5

GDN_HEADS = 8
GDN_DK = D_MODEL // GDN_HEADS
GDN_DV = D_MODEL // GDN_HEADS
GDN_CONV = 4
GDN_CHUNK = 64
GDN_QKV = GDN_HEADS * (2 * GDN_DK + GDN_DV)

RWKV_HEAD = 64
RWKV_HEADS = D_MODEL // RWKV_HEAD
RWKV_GN_EPS = 64e-5

PEER_KEYS = 128
PEER_HEADS = 8
PEER_DQ = 256
PEER_TOPK = 16
PEER_SEL = PEER_HEADS * PEER_TOPK

LANES = 128
SUBLANES = 8
VMEM_LIMIT = 56 * 1024 * 1024


def _cparams(sem):
    return pltpu.CompilerParams(dimension_semantics=sem, vmem_limit_bytes=VMEM_LIMIT)


def _bdot(a, b):
    return jnp.dot(a.astype(BF16), b.astype(BF16), preferred_element_type=F32)


def _bdot_nt(a, b):
    return lax.dot_general(a.astype(BF16), b.astype(BF16), (((1,), (1,)), ((), ())),
                           preferred_element_type=F32)


def _bdot_tn(a, b):
    return lax.dot_general(a.astype(BF16), b.astype(BF16), (((0,), (0,)), ((), ())),
                           preferred_element_type=F32)


def _sigmoid(x):
    return 1.0 / (1.0 + jnp.exp(-x))


def _silu(x):
    return x * _sigmoid(x)


def _softplus(x):
    return jnp.maximum(x, 0.0) + jnp.log1p(jnp.exp(-jnp.abs(x)))


def _norm_mod(x, g, sc, sh):
    ms = jnp.mean(x * x, axis=-1, keepdims=True)
    return (x * lax.rsqrt(ms + NORM_EPS) * g) * (1.0 + sc) + sh


def _adaln_kernel(c_ref, w_ref, b_ref, o_ref):
    cond = _silu(c_ref[...])
    o_ref[0] = _bdot(cond, w_ref[0]) + b_ref[0]


def _adaln(c, ada_w, ada_b):
    depth, d, n = ada_w.shape
    b = c.shape[0]
    tn = 1024
    return pl.pallas_call(
        _adaln_kernel,
        out_shape=jax.ShapeDtypeStruct((depth, b, n), F32),
        grid=(depth, n // tn),
        in_specs=[
            pl.BlockSpec((b, d), lambda l, j: (0, 0)),
            pl.BlockSpec((1, d, tn), lambda l, j: (l, 0, j)),
            pl.BlockSpec((1, 1, tn), lambda l, j: (l, 0, j)),
        ],
        out_specs=pl.BlockSpec((1, b, tn), lambda l, j: (l, 0, j)),
        compiler_params=_cparams(("parallel", "parallel")),
        name="adaln",
    )(c, ada_w, ada_b.reshape(depth, 1, n))


def _nm_matmul_kernel(x_ref, g_ref, sc_ref, sh_ref, w_ref, o_ref, *h_ref):
    h = _norm_mod(x_ref[...], g_ref[...], sc_ref[0], sh_ref[0])
    o_ref[...] = jnp.dot(h.astype(BF16), w_ref[...], preferred_element_type=F32).astype(o_ref.dtype)
    if h_ref:
        h_ref[0][...] = h


def _nm_matmul(x2d, g, sc, sh, w, *, rows_per_batch, tm=512, tn=None, emit_h=False):
    m, d = x2d.shape
    n = w.shape[1]
    nb = sc.shape[0]
    if tn is None:
        tn = n
    tpb = rows_per_batch // tm
    out_shape = [jax.ShapeDtypeStruct((m, n), F32)]
    out_specs = [pl.BlockSpec((tm, tn), lambda j, i: (i, j))]
    if emit_h:
        assert tn == n
        out_shape.append(jax.ShapeDtypeStruct((m, d), F32))
        out_specs.append(pl.BlockSpec((tm, d), lambda j, i: (i, 0)))
    res = pl.pallas_call(
        _nm_matmul_kernel,
        out_shape=out_shape,
        grid=(n // tn, m // tm),
        in_specs=[
            pl.BlockSpec((tm, d), lambda j, i: (i, 0)),
            pl.BlockSpec((1, d), lambda j, i: (0, 0)),
            pl.BlockSpec((1, 1, d), lambda j, i: (i // tpb, 0, 0)),
            pl.BlockSpec((1, 1, d), lambda j, i: (i // tpb, 0, 0)),
            pl.BlockSpec((d, tn), lambda j, i: (0, j)),
        ],
        out_specs=out_specs,
        compiler_params=_cparams(("parallel", "parallel")),
        name="norm_mod_matmul",
    )(x2d, g.reshape(1, d), sc.reshape(nb, 1, d), sh.reshape(nb, 1, d), w)
    return res if emit_h else res[0]


def _mm_res_kernel(*refs, has_mul):
    if has_mul:
        a_ref, m_ref, w_ref, r_ref, gt_ref, o_ref = refs
        a = a_ref[...] * m_ref[...]
    else:
        a_ref, w_ref, r_ref, gt_ref, o_ref = refs
        a = a_ref[...]
    y = jnp.dot(a.astype(BF16), w_ref[...], preferred_element_type=F32)
    o_ref[...] = r_ref[...] + gt_ref[0] * y


def _mm_res(a, w, res, gt, *, rows_per_batch, mul=None, tm=512):
    m, k = a.shape
    n = w.shape[1]
    nb = gt.shape[0]
    tpb = rows_per_batch // tm
    ins = [a]
    specs = [pl.BlockSpec((tm, k), lambda i: (i, 0))]
    if mul is not None:
        ins.append(mul)
        specs.append(pl.BlockSpec((tm, k), lambda i: (i, 0)))
    ins += [w, res, gt.reshape(nb, 1, n)]
    specs += [
        pl.BlockSpec((k, n), lambda i: (0, 0)),
        pl.BlockSpec((tm, n), lambda i: (i, 0)),
        pl.BlockSpec((1, 1, n), lambda i: (i // tpb, 0, 0)),
    ]
    return pl.pallas_call(
        functools.partial(_mm_res_kernel, has_mul=mul is not None),
        out_shape=jax.ShapeDtypeStruct((m, n), F32),
        grid=(m // tm,),
        in_specs=specs,
        out_specs=pl.BlockSpec((tm, n), lambda i: (i, 0)),
        compiler_params=_cparams(("parallel",)),
        name="matmul_residual",
    )(*ins)


def _ret_kernel(q_ref, k_ref, v_ref, gate_ref, cos_ref, sin_ref, dintra_ref, dq_ref, dk_ref, dchunk_ref,
                gng_ref, gnb_ref, o_ref, state_ref):
    H, dk, dv = RET_HEADS, RET_DK, RET_DV
    half = dk // 2

    @pl.when(pl.program_id(1) == 0)
    def _():
        state_ref[...] = jnp.zeros_like(state_ref)

    cos = cos_ref[...]
    sin = sin_ref[...]

    def rot(ref, h):
        x1 = ref[0, :, h * dk:h * dk + half]
        x2 = ref[0, :, h * dk + half:(h + 1) * dk]
        return jnp.concatenate([x1 * cos - x2 * sin, x1 * sin + x2 * cos], axis=-1)

    for h in range(H):
        q = rot(q_ref, h)
        k = rot(k_ref, h) * (dk ** -0.5)
        v = v_ref[0, :, h * dv:(h + 1) * dv]
        scores = _bdot_nt(q, k) * dintra_ref[h]
        st = state_ref[h]
        o = _bdot(scores, v) + _bdot(q, st) * dq_ref[h]
        state_ref[h] = st * dchunk_ref[h] + _bdot_tn(k * dk_ref[h], v)
        mu = jnp.mean(o, axis=-1, keepdims=True)
        var = jnp.mean(jnp.square(o - mu), axis=-1, keepdims=True)
        on = (o - mu) * lax.rsqrt(var + RET_GN_EPS) * gng_ref[h] + gnb_ref[h]
        g = gate_ref[0, :, h * dv:(h + 1) * dv]
        o_ref[0, :, h * dv:(h + 1) * dv] = (_silu(g) * on).astype(o_ref.dtype)


def _retention_scan(proj, gn_g, gn_b):
    B, S, _ = proj.shape
    H, dk, dv, C = RET_HEADS, RET_DK, RET_DV, RET_CHUNK
    half = dk // 2
    N = S // C
    inv_freq = RET_ROPE_BASE ** (-jnp.arange(half, dtype=F32) / half)
    ang = jnp.arange(S, dtype=F32)[:, None] * inv_freq[None, :]
    cos, sin = jnp.cos(ang), jnp.sin(ang)
    log_gamma = jnp.log1p(-jnp.exp2(-5.0 - jnp.arange(H, dtype=F32)))
    idx = jnp.arange(C, dtype=F32)
    diff = idx[:, None] - idx[None, :]
    causal = diff >= 0
    d_intra = jnp.where(causal[None], jnp.exp(jnp.where(causal, diff, 0.0)[None] * log_gamma[:, None, None]), 0.0)
    d_q = jnp.exp((idx[None, :] + 1.0) * log_gamma[:, None])[:, :, None]
    d_k = jnp.exp((C - 1.0 - idx)[None, :] * log_gamma[:, None])[:, :, None]
    d_chunk = jnp.exp(C * log_gamma)[:, None, None]
    qw, vw = H * dk, H * dv
    return pl.pallas_call(
        _ret_kernel,
        out_shape=jax.ShapeDtypeStruct((B, S, vw), BF16),
        grid=(B, N),
        in_specs=[
            pl.BlockSpec((1, C, qw), lambda b, n: (b, n, 0)),
            pl.BlockSpec((1, C, qw), lambda b, n: (b, n, 1)),
            pl.BlockSpec((1, C, vw), lambda b, n: (b, n, 1)),
            pl.BlockSpec((1, C, vw), lambda b, n: (b, n, 2)),
            pl.BlockSpec((C, half), lambda b, n: (n, 0)),
            pl.BlockSpec((C, half), lambda b, n: (n, 0)),
            pl.BlockSpec((H, C, C), lambda b, n: (0, 0, 0)),
            pl.BlockSpec((H, C, 1), lambda b, n: (0, 0, 0)),
            pl.BlockSpec((H, C, 1), lambda b, n: (0, 0, 0)),
            pl.BlockSpec((H, 1, 1), lambda b, n: (0, 0, 0)),
            pl.BlockSpec((H, 1, dv), lambda b, n: (0, 0, 0)),
            pl.BlockSpec((H, 1, dv), lambda b, n: (0, 0, 0)),
        ],
        out_specs=pl.BlockSpec((1, C, vw), lambda b, n: (b, n, 0)),
        scratch_shapes=[pltpu.VMEM((H, dk, dv), F32)],
        compiler_params=_cparams(("parallel", "arbitrary")),
        name="retention_scan",
    )(proj, proj, proj, proj, cos, sin, d_intra, d_q, d_k, d_chunk,
      gn_g.reshape(H, 1, dv), gn_b.reshape(H, 1, dv))


def _shift_rows(cur, prev8, s):
    rows = lax.broadcasted_iota(jnp.int32, cur.shape, 0)
    rolled = pltpu.roll(cur, s, axis=0)
    head = pltpu.roll(prev8, s, axis=0)
    head = jnp.concatenate([head, jnp.zeros((cur.shape[0] - SUBLANES, cur.shape[1]), cur.dtype)], axis=0)
    return jnp.where(rows < s, head, rolled)


def _cumsum_rows(x):
    rows = lax.broadcasted_iota(jnp.int32, x.shape, 0)
    s = 1
    while s < x.shape[0]:
        x = x + jnp.where(rows >= s, pltpu.roll(x, s, axis=0), 0.0)
        s *= 2
    return x


def _gdn_kernel(qkv_ref, gate_ref, ab_ref, cw_ref, alog_ref, dtb_ref, ng_ref, o_ref, state_ref, prev_ref):
    C, dk, H = GDN_CHUNK, GDN_DK, GDN_HEADS

    @pl.when(pl.program_id(1) == 0)
    def _():
        state_ref[...] = jnp.zeros_like(state_ref)
        prev_ref[...] = jnp.zeros_like(prev_ref)

    def conv_silu(col):
        cur = qkv_ref[0, :, col * dk:(col + 1) * dk]
        prev8 = prev_ref[:, col * dk:(col + 1) * dk]
        cw = cw_ref[:, col * dk:(col + 1) * dk]
        acc = cur * cw[GDN_CONV - 1:GDN_CONV]
        for s in range(1, GDN_CONV):
            acc = acc + _shift_rows(cur, prev8, s) * cw[GDN_CONV - 1 - s:GDN_CONV - s]
        prev_ref[:, col * dk:(col + 1) * dk] = cur[C - SUBLANES:]
        return _silu(acc)

    ab = ab_ref[0]
    g_all = -jnp.exp(alog_ref[...]) * _softplus(ab + dtb_ref[...])
    beta_all = _sigmoid(ab)
    ri = lax.broadcasted_iota(jnp.int32, (C, C), 0)
    ci = lax.broadcasted_iota(jnp.int32, (C, C), 1)
    incl = ri >= ci

    bdot_nt = lambda x, y: jnp.einsum('hid,hjd->hij', x.astype(BF16), y.astype(BF16), preferred_element_type=F32)
    bdot = lambda x, y: jnp.einsum('hij,hjk->hik', x, y, preferred_element_type=F32)
    stack = lambda xs: jnp.stack(xs, axis=0)

    def hdot(x, y):
        xh, yh = x.astype(BF16), y.astype(BF16)
        xl, yl = (x - xh.astype(F32)).astype(BF16), (y - yh.astype(F32)).astype(BF16)
        return bdot(xh, yh) + (bdot(xh, yl) + bdot(xl, yh))

    qs, ks, vs, betas, cums = [], [], [], [], []
    for h in range(H):
        q = conv_silu(h)
        k = conv_silu(H + h)
        qs.append(q * lax.rsqrt(jnp.sum(q * q, axis=-1, keepdims=True) + 1e-6) * (dk ** -0.5))
        ks.append(k * lax.rsqrt(jnp.sum(k * k, axis=-1, keepdims=True) + 1e-6))
        vs.append(conv_silu(2 * H + h))
        betas.append(beta_all[:, H + h:H + h + 1])
        cums.append(_cumsum_rows(jnp.broadcast_to(g_all[:, h:h + 1], (C, LANES))))
    q, k, v, beta, cum = stack(qs), stack(ks), stack(vs), stack(betas), stack(cums)
    cum_c = cum[:, :, :1]
    cum_last = cum[:, C - 1:C, :1]
    cum_r = stack([cums[h].T[:C, :] for h in range(H)])
    decay = jnp.where(incl, jnp.exp(jnp.where(incl, cum[:, :, :C] - cum_r, 0.0)), 0.0)

    L = jnp.where(ri > ci, bdot_nt(k, k) * decay, 0.0) * beta
    rhs = jnp.concatenate([k * (beta * jnp.exp(cum_c)), v * beta], axis=-1)
    eye = (ri == ci).astype(F32)
    p = -L
    inv = eye + p
    s = 2
    while s < C:
        p = hdot(p, p)
        inv = inv + hdot(inv, p)
        s *= 2
    sol = hdot(inv, rhs)
    a_qk = bdot_nt(q, k) * decay
    q_dec = q * jnp.exp(cum_c)
    k_dec = k * jnp.exp(cum_last - cum_c)
    e_last = jnp.exp(cum_last)

    for h in range(H):
        st = state_ref[h]
        u = sol[h, :, dk:] - _bdot(sol[h, :, :dk], st)
        o = _bdot(q_dec[h], st) + _bdot(a_qk[h], u)
        state_ref[h] = st * e_last[h] + _bdot_tn(k_dec[h], u)
        ms = jnp.mean(o * o, axis=-1, keepdims=True)
        o = o * lax.rsqrt(ms + NORM_EPS) * ng_ref[...]
        o_ref[0, :, h * dk:(h + 1) * dk] = (o * _silu(gate_ref[0, :, h * dk:(h + 1) * dk])).astype(o_ref.dtype)


def _gdn_scan(proj, proj_ab, conv_w, a_log, dt_bias, norm_g):
    B, S, _ = proj.shape
    H, dk, C = GDN_HEADS, GDN_DK, GDN_CHUNK
    N = S // C
    pad = lambda t: jnp.pad(t.astype(F32), (0, LANES - H)).reshape(1, LANES)
    row = pl.BlockSpec((1, LANES), lambda b, n: (0, 0))
    return pl.pallas_call(
        _gdn_kernel,
        out_shape=jax.ShapeDtypeStruct((B, S, H * dk), BF16),
        grid=(B, N),
        in_specs=[pl.BlockSpec((1, C, 3 * H * dk), lambda b, n: (b, n, 0)),
                  pl.BlockSpec((1, C, H * dk), lambda b, n: (b, n, 3)),
                  pl.BlockSpec((1, C, LANES), lambda b, n: (b, n, 0)),
                  pl.BlockSpec((GDN_CONV, 3 * H * dk), lambda b, n: (0, 0)),
                  row, row, row],
        out_specs=pl.BlockSpec((1, C, H * dk), lambda b, n: (b, n, 0)),
        scratch_shapes=[pltpu.VMEM((H, dk, dk), F32), pltpu.VMEM((SUBLANES, 3 * H * dk), F32)],
        compiler_params=_cparams(("parallel", "arbitrary")),
        name="gdn_scan",
    )(proj, proj, proj_ab, conv_w, pad(a_log), pad(dt_bias), norm_g.reshape(1, dk))


def _rwkv_proj_kernel(x_ref, xp_ref, g_ref, sc_ref, sh_ref, mu_ref, wr_ref, wk_ref, wv_ref, w1_ref, w2_ref,
                      a1_ref, a2_ref, g1_ref, g2_ref, w0_ref, a0_ref, kk_ref, ka_ref,
                      r_o, dec_o, k_o, v_o, kk_o, a_o, g_o, *, tiles_per_seq):
    h = _norm_mod(x_ref[...], g_ref[...], sc_ref[0], sh_ref[0])
    hp8 = _norm_mod(xp_ref[...], g_ref[...], sc_ref[0], sh_ref[0])
    seq_start = pl.program_id(0) % tiles_per_seq == 0
    first = jnp.where(seq_start, 0.0, hp8[SUBLANES - 1:SUBLANES, :])
    rows = lax.broadcasted_iota(jnp.int32, h.shape, 0)
    xx = jnp.where(rows == 0, first, pltpu.roll(h, 1, axis=0)) - h
    mix = lambda j: h + xx * mu_ref[j:j + 1, :]
    r = _bdot(mix(0), wr_ref[...])
    lw = w0_ref[...] + _bdot(jnp.tanh(_bdot(mix(1), w1_ref[...])), w2_ref[...])
    k = _bdot(mix(2), wk_ref[...])
    v = _bdot(mix(3), wv_ref[...])
    a = _sigmoid(a0_ref[...] + _bdot(_bdot(mix(4), a1_ref[...]), a2_ref[...]))
    g = _bdot(_sigmoid(_bdot(mix(5), g1_ref[...])), g2_ref[...])
    w = -_softplus(-lw) - 0.5
    r_o[...] = r
    dec_o[...] = jnp.exp(-jnp.exp(w))
    k_o[...] = k * (1.0 + (a - 1.0) * ka_ref[...])
    v_o[...] = v
    kk_o[...] = k * kk_ref[...]
    a_o[...] = a
    g_o[...] = g


def _pad_cols(w, n):
    return jnp.pad(w, ((0, 0), (0, n - w.shape[1])))


def _pad_rows(w, n):
    return jnp.pad(w, ((0, n - w.shape[0]), (0, 0)))


def _rwkv_proj(x2d, g, sc, sh, mu, w_r, w_k, w_v, w1, w2, a1, a2, g1, g2, w0, a0, k_k, k_a, *, rows_per_batch, tm=256):
    m, d = x2d.shape
    nb = sc.shape[0]
    tpb = rows_per_batch // tm
    lora_w = LANES * pl.cdiv(w1.shape[1], LANES)
    lora_g = LANES * pl.cdiv(g1.shape[1], LANES)
    bf = lambda t: t.astype(BF16)
    full = lambda a: pl.BlockSpec(a.shape, lambda i: (0,) * a.ndim)
    row = lambda t: t.reshape(1, d)
    ws = [bf(w_r), bf(w_k), bf(w_v), bf(_pad_cols(w1, lora_w)), bf(_pad_rows(w2, lora_w)),
          bf(_pad_cols(a1, lora_w)), bf(_pad_rows(a2, lora_w)), bf(_pad_cols(g1, lora_g)), bf(_pad_rows(g2, lora_g)),
          row(w0), row(a0), row(k_k), row(k_a)]
    tile = pl.BlockSpec((tm, d), lambda i: (i, 0))
    return pl.pallas_call(
        functools.partial(_rwkv_proj_kernel, tiles_per_seq=tpb),
        out_shape=[jax.ShapeDtypeStruct((m, d), F32)] * 7,
        grid=(m // tm,),
        in_specs=[
            tile,
            pl.BlockSpec((SUBLANES, d), lambda i: (jnp.maximum(i * (tm // SUBLANES) - 1, 0), 0)),
            pl.BlockSpec((1, d), lambda i: (0, 0)),
            pl.BlockSpec((1, 1, d), lambda i: (i // tpb, 0, 0)),
            pl.BlockSpec((1, 1, d), lambda i: (i // tpb, 0, 0)),
            full(mu),
        ] + [full(w) for w in ws],
        out_specs=[tile] * 7,
        compiler_params=_cparams(("parallel",)),
        name="rwkv_proj",
    )(x2d, x2d, g.reshape(1, d), sc.reshape(nb, 1, d), sh.reshape(nb, 1, d), mu, *ws)


RWKV_VUNROLL = 4


def _rwkv_scan_kernel(r_ref, w_ref, k_ref, v_ref, kk_ref, a_ref, rk_ref, lng_ref, lnb_ref, y_ref, state_ref, yrow_ref):
    n = RWKV_HEAD

    @pl.when(pl.program_id(0) == 0)
    def _():
        state_ref[...] = jnp.zeros_like(state_ref)

    def step(t, carry):
        r, w, k, kkr, a = r_ref[t], w_ref[t], k_ref[t], kk_ref[t], a_ref[t]
        kk = kkr * lax.rsqrt(jnp.sum(kkr * kkr, axis=0, keepdims=True) + 1e-6)
        nkk = -kk
        kka = kk * a

        def vloop(vb, c):
            for j in range(RWKV_VUNROLL):
                vi = vb * RWKV_VUNROLL + j
                sv = state_ref[vi]
                sa = jnp.sum(sv * nkk, axis=0, keepdims=True)
                vrow = v_ref[t, pl.ds(vi, 1), :]
                sn = sv * w + sa * kka + vrow * k
                state_ref[vi] = sn
                yrow_ref[pl.ds(vi, 1), :] = jnp.sum(sn * r, axis=0, keepdims=True)
            return c

        lax.fori_loop(0, n // RWKV_VUNROLL, vloop, 0)
        y = yrow_ref[...]
        mu = jnp.mean(y, axis=0, keepdims=True)
        var = jnp.mean(jnp.square(y - mu), axis=0, keepdims=True)
        yn = (y - mu) * lax.rsqrt(var + RWKV_GN_EPS) * lng_ref[...] + lnb_ref[...]
        bonus = jnp.sum(r * k * rk_ref[...], axis=0, keepdims=True)
        y_ref[t] = yn + bonus * v_ref[t]
        return carry

    lax.fori_loop(0, r_ref.shape[0], step, 0)


def _rwkv_scan(r, dec, k, v, kk, a, r_k, ln_g, ln_b, *, tc=32):
    B, S, D = r.shape
    H, n = RWKV_HEADS, RWKV_HEAD
    lanes = B * H
    to_scan = lambda t: jnp.transpose(t.reshape(B, S, H, n), (1, 3, 0, 2)).reshape(S, n, lanes)
    per_head = lambda p: jnp.tile(p.T, (1, B))
    blk = pl.BlockSpec((tc, n, lanes), lambda i: (i, 0, 0))
    cst = pl.BlockSpec((n, lanes), lambda i: (0, 0))
    y = pl.pallas_call(
        _rwkv_scan_kernel,
        out_shape=jax.ShapeDtypeStruct((S, n, lanes), F32),
        grid=(S // tc,),
        in_specs=[blk] * 6 + [cst] * 3,
        out_specs=blk,
        scratch_shapes=[pltpu.VMEM((n, n, lanes), F32), pltpu.VMEM((n, lanes), F32)],
        compiler_params=_cparams(("arbitrary",)),
        name="rwkv_scan",
    )(to_scan(r), to_scan(dec), to_scan(k), to_scan(v), to_scan(kk), to_scan(a),
      per_head(r_k), per_head(ln_g), per_head(ln_b))
    return jnp.transpose(y.reshape(S, n, B, H), (2, 0, 3, 1)).reshape(B, S, D)


def _rwkv_mixer(x, g, sc, sh, gt, mu, w_r, w_k, w_v, w_o, w0, w1, w2, a0, a1, a2, g1, g2, k_k, k_a, r_k, ln_g, ln_b):
    B, S, D = x.shape
    x2d = x.reshape(B * S, D)
    tm = min(256, S)
    r, dec, k, v, kk, a, gg = _rwkv_proj(x2d, g, sc, sh, mu, w_r, w_k, w_v, w1, w2, a1, a2, g1, g2, w0, a0, k_k, k_a,
                                         rows_per_batch=S, tm=tm)
    sh3 = lambda t: t.reshape(B, S, D)
    y = _rwkv_scan(sh3(r), sh3(dec), sh3(k), sh3(v), sh3(kk), sh3(a), r_k, ln_g, ln_b, tc=min(32, S))
    out = _mm_res(y.reshape(B * S, D), w_o.astype(BF16), x2d, gt, rows_per_batch=S, mul=gg, tm=min(512, S))
    return out.reshape(B, S, D)


def _rwkv_mixer_test(x, g, sc, sh, p):
    gt = jnp.ones_like(sc)
    return _rwkv_mixer(x, g, sc, sh, gt, p['mu'], p['w_r'], p['w_k'], p['w_v'], p['w_o'], p['w0'], p['w1'], p['w2'],
                       p['a0'], p['a1'], p['a2'], p['g1'], p['g2'], p['k_k'], p['k_a'], p['r_k'], p['ln_g'],
                       p['ln_b']) - x


def _topk_rows(s, k, rows=None):
    if rows is None:
        rows = lax.broadcasted_iota(jnp.int32, s.shape, 0)
    n = jnp.iinfo(jnp.int32).max
    vals, ids = [], []
    for _ in range(k):
        m = jnp.max(s, axis=0, keepdims=True)
        idx = jnp.min(jnp.where(s == m, rows, n), axis=0, keepdims=True)
        vals.append(m)
        ids.append(idx)
        s = jnp.where(rows == idx, -jnp.inf, s)
    return jnp.concatenate(vals, axis=0), jnp.concatenate(ids, axis=0)


def _take_rows(table, pos):
    out = jnp.zeros(pos.shape, table.dtype)
    for m in range(table.shape[0]):
        out = jnp.where(pos == m, table[m:m + 1, :], out)
    return out


_PEER_CAND = [(i, PEER_TOPK // (i + 1)) for i in range(PEER_TOPK)]
_PEER_NCAND = sum(n for _, n in _PEER_CAND)
_PEER_NCAND_PAD = SUBLANES * pl.cdiv(_PEER_NCAND, SUBLANES)


def _peer_cand_codes():
    codes = [i * PEER_TOPK + j for i, n in _PEER_CAND for j in range(n)]
    codes += [PEER_TOPK * PEER_TOPK + p for p in range(_PEER_NCAND_PAD - _PEER_NCAND)]
    return jnp.broadcast_to(jnp.asarray(codes, jnp.int32)[:, None], (_PEER_NCAND_PAD, LANES))


def _peer_route_kernel(q_ref, keys_ref, codes_ref, idx_o, gate_o):
    K, half = PEER_TOPK, PEER_DQ // 2

    tm = q_ref.shape[0]
    G = 2
    codes = jnp.concatenate([codes_ref[...]] * (G * tm // LANES), axis=1)
    pad = jnp.full((_PEER_NCAND_PAD - _PEER_NCAND, tm), -jnp.inf, F32)

    def group(hg, carry):
        ss = []
        for dh in range(G):
            for p in range(2):
                c = pl.multiple_of((2 * (G * hg + dh) + p) * half, half)
                ss.append(_bdot_nt(keys_ref[G * hg + dh, p], q_ref[:, pl.ds(c, half)]))
        vals, ids = _topk_rows(jnp.concatenate(ss, axis=1), K)
        part = lambda t, j: t[:, j * tm:(j + 1) * tm]
        cands = []
        for dh in range(G):
            va, vb = part(vals, 2 * dh), part(vals, 2 * dh + 1)
            cands.append(jnp.concatenate([va[i:i + 1, :] + vb[:n, :] for i, n in _PEER_CAND] + [pad], axis=0))
        best, pos = _topk_rows(jnp.concatenate(cands, axis=1), K, codes)
        for dh in range(G):
            ia, ib = part(ids, 2 * dh), part(ids, 2 * dh + 1)
            ps, bs = part(pos, dh), part(best, dh)
            expert = _take_rows(ia, ps // K) * PEER_KEYS + _take_rows(ib, ps % K)
            e = jnp.exp(bs - bs[0:1, :])
            r0 = pl.multiple_of((G * hg + dh) * K, K)
            idx_o[pl.ds(r0, K), :] = expert
            gate_o[pl.ds(r0, K), :] = e / jnp.sum(e, axis=0, keepdims=True)
        return carry

    lax.fori_loop(0, PEER_HEADS // G, group, 0)


def _peer_route(q, sub_keys, *, tm=256):
    t, n = q.shape
    blk = pl.BlockSpec((PEER_SEL, tm), lambda i: (0, i))
    return pl.pallas_call(
        _peer_route_kernel,
        out_shape=[jax.ShapeDtypeStruct((PEER_SEL, t), jnp.int32), jax.ShapeDtypeStruct((PEER_SEL, t), F32)],
        grid=(t // tm,),
        in_specs=[pl.BlockSpec((tm, n), lambda i: (i, 0)),
                  pl.BlockSpec(sub_keys.shape, lambda i: (0, 0, 0, 0)),
                  pl.BlockSpec((_PEER_NCAND_PAD, LANES), lambda i: (0, 0))],
        out_specs=[blk, blk],
        compiler_params=_cparams(("parallel",)),
        name="peer_route",
    )(q, sub_keys.astype(BF16), _peer_cand_codes())


def _pack_uv(u, v):
    e, d = u.shape
    ub = lax.bitcast_convert_type(u.astype(BF16), jnp.uint16).astype(jnp.uint32)
    vb = lax.bitcast_convert_type(v.astype(BF16), jnp.uint16).astype(jnp.uint32)
    return lax.bitcast_convert_type((vb << 16) | ub, jnp.int32).reshape(e, d // LANES, LANES)


PEER_NBUF = 4


def _gelu(x):
    return 0.5 * x * (1.0 + lax.erf(x * (2.0 ** -0.5)))


def _peer_eval(packed, x, gate):
    u = lax.bitcast_convert_type(packed << 16, F32)
    v = lax.bitcast_convert_type(packed & jnp.int32(-65536), F32)
    act = jnp.sum(jnp.sum(u * x[None], axis=1), axis=1, keepdims=True)
    wgt = gate * _gelu(act)
    return jnp.sum(v * wgt[:, :, None], axis=0)


def _gate_column(gate_ref, t):
    lane = lax.broadcasted_iota(jnp.int32, gate_ref.shape, 1)
    return jnp.sum(jnp.where(lane == t, gate_ref[...], 0.0), axis=1, keepdims=True)


def _peer_expert_kernel(idx_hbm, gate_ref, h_ref, xres_ref, gt_ref, uv_hbm, *rest, tok0):
    o_ref, idx_smem, buf, sem_idx, sem = rest[-5:]
    tb = h_ref.shape[0]
    nsel = PEER_SEL
    base = (pl.program_id(0) * tb + tok0) * nsel
    cp = pltpu.make_async_copy(idx_hbm.at[pl.ds(base, tb * nsel)], idx_smem, sem_idx)
    cp.start()
    cp.wait()

    def issue(t, slot):
        for k in range(nsel):
            e = idx_smem[t * nsel + k]
            pltpu.make_async_copy(uv_hbm.at[e], buf.at[slot, k], sem.at[slot]).start()

    def wait(slot):
        pltpu.make_async_copy(uv_hbm.at[pl.ds(0, nsel)], buf.at[slot], sem.at[slot]).wait()

    for t0 in range(PEER_NBUF - 1):
        issue(t0, t0)

    def body(t, carry):
        slot = t % PEER_NBUF
        nxt = t + PEER_NBUF - 1

        @pl.when(nxt < tb)
        def _():
            issue(nxt, nxt % PEER_NBUF)

        wait(slot)
        out = _peer_eval(buf[slot], h_ref[t], _gate_column(gate_ref, t))
        o_ref[t] = xres_ref[t] + gt_ref[0] * out
        return carry

    lax.fori_loop(0, tb, body, 0)


def _peer_expert(idx_flat, gate_t, h3, xres3, gt3, uv, *, rows_per_batch, tok0, ntok, tb=128, after=None):
    _, c, _ = h3.shape
    tpb = rows_per_batch // tb
    b0 = tok0 // tb
    tok = pl.BlockSpec((tb, c, LANES), lambda i: (i + b0, 0, 0))
    return pl.pallas_call(
        functools.partial(_peer_expert_kernel, tok0=tok0),
        out_shape=jax.ShapeDtypeStruct((ntok, c, LANES), F32),
        grid=(ntok // tb,),
        in_specs=[
            pl.BlockSpec(memory_space=pl.ANY),
            pl.BlockSpec((PEER_SEL, tb), lambda i: (0, i + b0)),
            tok, tok,
            pl.BlockSpec((1, c, LANES), lambda i: ((i + b0) // tpb, 0, 0)),
            pl.BlockSpec(memory_space=pl.ANY),
        ] + ([] if after is None else [pl.BlockSpec(memory_space=pl.ANY)]),
        out_specs=pl.BlockSpec((tb, c, LANES), lambda i: (i, 0, 0)),
        scratch_shapes=[
            pltpu.SMEM((tb * PEER_SEL,), jnp.int32),
            pltpu.VMEM((PEER_NBUF, PEER_SEL, c, LANES), jnp.int32),
            pltpu.SemaphoreType.DMA,
            pltpu.SemaphoreType.DMA((PEER_NBUF,)),
        ],
        compiler_params=_cparams(("arbitrary",)),
        name="peer_expert",
    )(idx_flat, gate_t, h3, xres3, gt3, uv, *(() if after is None else (after,)))


SC_CORES = 2
SC_SUBCORES = 16
SC_WORKERS = SC_CORES * SC_SUBCORES
SC_LANES = 16
SC_TOKENS = 8
SC_GATHER_ROWS = 32
SC_CHUNKS = 8
_SC_PARAMS = pltpu.CompilerParams(needs_layout_passes=False)


def _pack_pairs(t):
    half = t.shape[1] // 2
    b = lax.bitcast_convert_type(t.astype(BF16), jnp.uint16).astype(jnp.uint32)
    return lax.bitcast_convert_type((b[:, half:] << 16) | b[:, :half], jnp.int32)


def _sc_unpack(w):
    return (lax.bitcast_convert_type(w << 16, F32), lax.bitcast_convert_type(w & jnp.int32(-65536), F32))


def _sc_mesh():
    return plsc.VectorSubcoreMesh(core_axis_name="core", subcore_axis_name="subcore")


def _sc_act(u2, idx_flat, h_flat, *, tok0, ntok, d):
    nsel, L, G, CG, TBK = PEER_SEL, SC_LANES, SC_GATHER_ROWS, SC_CHUNKS, SC_TOKENS
    half = d // 2
    tpw = ntok // SC_WORKERS
    ng = nsel // G
    ngt = TBK * ng

    @functools.partial(
        pl.kernel, out_type=jax.ShapeDtypeStruct((ntok * nsel,), F32), mesh=_sc_mesh(),
        scratch_types=[pltpu.VMEM((TBK * nsel,), jnp.int32), pltpu.VMEM((TBK * d,), F32),
                       pltpu.VMEM((2, G, half), jnp.int32), pltpu.VMEM((TBK * nsel,), F32),
                       pltpu.VMEM((G * L,), F32), pltpu.SemaphoreType.DMA((2,))],
        compiler_params=_SC_PARAMS, name="peer_sc_act")
    def k(u_hbm, i_hbm, x_hbm, o_hbm, idx_v, x_v, buf, act_v, acc_v, sem):
        wid = lax.axis_index("core") * SC_SUBCORES + lax.axis_index("subcore")
        lanes = lax.iota(jnp.int32, L)

        def gather(gi, b):
            return pltpu.make_async_copy(u_hbm.at[idx_v.at[pl.ds(gi * G, G)]], buf.at[b], sem.at[b])

        @pl.loop(0, tpw // TBK)
        def _(bi):
            tl = wid * tpw + bi * TBK
            pltpu.sync_copy(i_hbm.at[pl.ds((tok0 + tl) * nsel, TBK * nsel)], idx_v)
            pltpu.sync_copy(x_hbm.at[pl.ds((tok0 + tl) * d, TBK * d)], x_v)
            gather(0, 0).start()

            @pl.loop(0, ngt, step=2)
            def _(g0):
                for b in range(2):
                    gi = g0 + b

                    @pl.when(gi + 1 < ngt)
                    def _():
                        gather(gi + 1, 1 - b).start()

                    gather(gi, b).wait()
                    xbase = (gi // ng) * d
                    for jg in range(half // (CG * L)):
                        xl = [x_v[pl.ds(pl.multiple_of(xbase + (jg * CG + c) * L, L), L)] for c in range(CG)]
                        xh = [x_v[pl.ds(pl.multiple_of(xbase + half + (jg * CG + c) * L, L), L)] for c in range(CG)]

                        def row(r, carry):
                            ps = []
                            for c in range(CG):
                                lo, hi = _sc_unpack(buf[b, r, pl.ds((jg * CG + c) * L, L)])
                                ps.append(lo * xl[c] + hi * xh[c])
                            while len(ps) > 1:
                                ps = [ps[i] + ps[i + 1] for i in range(0, len(ps), 2)]
                            off = pl.multiple_of(r * L, L)
                            if jg == 0:
                                acc_v[pl.ds(off, L)] = ps[0]
                            else:
                                acc_v[pl.ds(off, L)] = acc_v[pl.ds(off, L)] + ps[0]
                            return carry

                        plsc.parallel_loop(0, G, carry=jnp.int32(0))(row)
                    for part in range(G // L):
                        vec = jnp.zeros((L,), F32)
                        for r in range(L):
                            vec = jnp.where(lanes == r, jnp.sum(acc_v[pl.ds((part * L + r) * L, L)]), vec)
                        act_v[pl.ds(pl.multiple_of(gi * G + part * L, L), L)] = vec

            pltpu.sync_copy(act_v, o_hbm.at[pl.ds(tl * nsel, TBK * nsel)])

    return k(u2, idx_flat, h_flat)


def _sc_out(v2, idx_flat, w_flat, xres_flat, gt_flat, *, tok0, ntok, d, rows_per_batch):
    nsel, L, G, CG, TBK = PEER_SEL, SC_LANES, SC_GATHER_ROWS, SC_CHUNKS, SC_TOKENS
    half = d // 2
    tpw = ntok // SC_WORKERS
    ng = nsel // G
    ngt = TBK * ng

    @functools.partial(
        pl.kernel, out_type=jax.ShapeDtypeStruct((ntok * d,), F32), mesh=_sc_mesh(),
        scratch_types=[pltpu.VMEM((TBK * nsel,), jnp.int32), pltpu.VMEM((TBK * nsel,), F32),
                       pltpu.VMEM((2, G, half), jnp.int32), pltpu.VMEM((TBK * d,), F32),
                       pltpu.VMEM((TBK * d,), F32), pltpu.VMEM((d,), F32), pltpu.SemaphoreType.DMA((2,))],
        compiler_params=_SC_PARAMS, name="peer_sc_out")
    def k(v_hbm, i_hbm, w_hbm, xr_hbm, gt_hbm, o_hbm, idx_v, w_v, buf, out_v, xr_v, gt_v, sem):
        wid = lax.axis_index("core") * SC_SUBCORES + lax.axis_index("subcore")

        def gather(gi, b):
            return pltpu.make_async_copy(v_hbm.at[idx_v.at[pl.ds(gi * G, G)]], buf.at[b], sem.at[b])

        @pl.loop(0, tpw // TBK)
        def _(bi):
            tl = wid * tpw + bi * TBK
            pltpu.sync_copy(i_hbm.at[pl.ds((tok0 + tl) * nsel, TBK * nsel)], idx_v)
            pltpu.sync_copy(w_hbm.at[pl.ds(tl * nsel, TBK * nsel)], w_v)
            gather(0, 0).start()
            pltpu.sync_copy(xr_hbm.at[pl.ds((tok0 + tl) * d, TBK * d)], xr_v)
            pltpu.sync_copy(gt_hbm.at[pl.ds(((tok0 + tl) // rows_per_batch) * d, d)], gt_v)

            @pl.loop(0, TBK * d // L)
            def _(i):
                out_v[pl.ds(pl.multiple_of(i * L, L), L)] = jnp.zeros((L,), F32)

            @pl.loop(0, ngt, step=2)
            def _(g0):
                for b in range(2):
                    gi = g0 + b

                    @pl.when(gi + 1 < ngt)
                    def _():
                        gather(gi + 1, 1 - b).start()

                    gather(gi, b).wait()
                    obase = (gi // ng) * d
                    for jg in range(half // (CG * L)):
                        lo0 = obase + jg * CG * L
                        hi0 = lo0 + half
                        accs = tuple(out_v[pl.ds(pl.multiple_of(lo0 + c * L, L), L)] for c in range(CG)) + \
                            tuple(out_v[pl.ds(pl.multiple_of(hi0 + c * L, L), L)] for c in range(CG))

                        def row(r, accs):
                            wk = plsc.load_gather(w_v, [jnp.full((L,), gi * G + r, jnp.int32)])
                            new = list(accs)
                            for c in range(CG):
                                lo, hi = _sc_unpack(buf[b, r, pl.ds((jg * CG + c) * L, L)])
                                new[c] = new[c] + wk * lo
                                new[CG + c] = new[CG + c] + wk * hi
                            return tuple(new)

                        accs = lax.fori_loop(0, G, row, accs)
                        for c in range(CG):
                            out_v[pl.ds(pl.multiple_of(lo0 + c * L, L), L)] = accs[c]
                            out_v[pl.ds(pl.multiple_of(hi0 + c * L, L), L)] = accs[CG + c]

            @pl.loop(0, TBK * d // L)
            def _(i):
                off = pl.multiple_of(i * L, L)
                goff = pl.multiple_of((i % (d // L)) * L, L)
                out_v[pl.ds(off, L)] = xr_v[pl.ds(off, L)] + gt_v[pl.ds(goff, L)] * out_v[pl.ds(off, L)]

            pltpu.sync_copy(out_v, o_hbm.at[pl.ds(tl * d, TBK * d)])

    return k(v2, idx_flat, w_flat, xres_flat, gt_flat)


def _peer_weight_kernel(act_ref, gate_ref, *rest):
    rest[-1][...] = gate_ref[...] * _gelu(act_ref[...])


def _peer_weight(act, gate_tok, *, tok0, tm=1024, after=None):
    n, k = act.shape
    tm = math.gcd(math.gcd(n, tok0), tm)
    assert tm % SUBLANES == 0
    b0 = tok0 // tm
    return pl.pallas_call(
        _peer_weight_kernel,
        out_shape=jax.ShapeDtypeStruct((n, k), F32),
        grid=(n // tm,),
        in_specs=[pl.BlockSpec((tm, k), lambda i: (i, 0)), pl.BlockSpec((tm, k), lambda i: (i + b0, 0))]
        + ([] if after is None else [pl.BlockSpec(memory_space=pl.ANY)]),
        out_specs=pl.BlockSpec((tm, k), lambda i: (i, 0)),
        compiler_params=_cparams(("parallel",)),
        name="peer_weight",
    )(act, gate_tok, *(() if after is None else (after,)))


PEER_SC_SHARE = (19, 32)
PEER_SC_CALLS = 2
PEER_TC_SPLIT = (1, 1, 2)


def _peer_experts(idx_flat, gate_t, h, xres, gt, uv, u2, v2, *, rows_per_batch):
    t, d = h.shape
    c = d // LANES
    tb = min(128, rows_per_batch)
    unit = SC_WORKERS * SC_TOKENS * PEER_SC_CALLS
    n_sc = t * PEER_SC_SHARE[0] // PEER_SC_SHARE[1] // unit * unit
    ch = n_sc // PEER_SC_CALLS
    h3, xres3, gt3 = h.reshape(t, c, LANES), xres.reshape(t, c, LANES), gt.reshape(-1, c, LANES)
    blocks = (t - n_sc) // tb
    split = PEER_TC_SPLIT if n_sc and blocks >= sum(PEER_TC_SPLIT) else (1,) if blocks else ()
    cum = [sum(split[:j]) for j in range(len(split) + 1)]
    bounds = [n_sc + (blocks * cj // max(cum[-1], 1)) * tb for cj in cum]

    def gather_call(j, after):
        lo, hi = bounds[j], bounds[j + 1]
        return _peer_expert(idx_flat, gate_t, h3, xres3, gt3, uv, rows_per_batch=rows_per_batch, tok0=lo, ntok=hi - lo,
                            tb=tb, after=after)

    sc_outs, tc_outs, last = [], [], None
    if n_sc:
        gate_tok = gate_t.T
        h_flat, xres_flat, gt_flat = h.reshape(-1), xres.reshape(-1), gt.reshape(-1)
        acts = [_sc_act(u2, idx_flat, h_flat, tok0=i * ch, ntok=ch, d=d) for i in range(PEER_SC_CALLS)]
        for i, a in enumerate(acts):
            if len(tc_outs) < len(split):
                tc_outs.append(gather_call(len(tc_outs), last))
                last = tc_outs[-1]
            last = _peer_weight(a.reshape(ch, PEER_SEL), gate_tok, tok0=i * ch, after=last)
            sc_outs.append(_sc_out(v2, idx_flat, last.reshape(-1), xres_flat, gt_flat, tok0=i * ch, ntok=ch, d=d,
                                   rows_per_batch=rows_per_batch).reshape(ch, d))
    while len(tc_outs) < len(split):
        tc_outs.append(gather_call(len(tc_outs), last))
        last = tc_outs[-1]
    return jnp.concatenate(sc_outs + [o.reshape(-1, d) for o in tc_outs], axis=0)


def _peer_ffn(x2d, g, sc, sh, gt, w_q, sub_keys, u, v, *, rows_per_batch):
    tm = min(512, rows_per_batch)
    q, h = _nm_matmul(x2d, g, sc, sh, w_q, rows_per_batch=rows_per_batch, tm=tm, emit_h=True)
    idx_t, gate_t = _peer_route(q, sub_keys, tm=min(256, rows_per_batch))
    return _peer_experts(idx_t.T.reshape(-1), gate_t, h, x2d, gt, _pack_uv(u, v), _pack_pairs(u), _pack_pairs(v),
                         rows_per_batch=rows_per_batch)


def _peer_test(x, g, sc, sh, w_q, sub_keys, u, v):
    B, S, D = x.shape
    x2d = x.reshape(B * S, D)
    return (_peer_ffn(x2d, g, sc, sh, jnp.ones_like(sc), w_q.astype(BF16), sub_keys, u, v,
                      rows_per_batch=S) - x2d).reshape(B, S, D)


def _final_norm_kernel(x_ref, g_ref, o_ref):
    x = x_ref[...]
    ms = jnp.mean(x * x, axis=-1, keepdims=True)
    o_ref[...] = x * lax.rsqrt(ms + NORM_EPS) * g_ref[...]


def _final_norm(x2d, g, *, tm=512):
    m, d = x2d.shape
    tile = pl.BlockSpec((tm, d), lambda i: (i, 0))
    return pl.pallas_call(
        _final_norm_kernel,
        out_shape=jax.ShapeDtypeStruct((m, d), F32),
        grid=(m // tm,),
        in_specs=[tile, pl.BlockSpec((1, d), lambda i: (0, 0))],
        out_specs=tile,
        compiler_params=_cparams(("parallel",)),
        name="final_norm",
    )(x2d, g.reshape(1, d))


def kernel(x, c, ada_w, ada_b, norm_mix_g, norm_ffn_g, final_norm_g, ret_w_in, ret_w_out, ret_gn_g, ret_gn_b, gdn_w_in, gdn_conv_w, gdn_a_log, gdn_dt_bias, gdn_norm_g, gdn_w_out, rwkv_mu, rwkv_w_r, rwkv_w_k, rwkv_w_v, rwkv_w_o, rwkv_w0, rwkv_w1, rwkv_w2, rwkv_a0, rwkv_a1, rwkv_a2, rwkv_g1, rwkv_g2, rwkv_k_k, rwkv_k_a, rwkv_r_k, rwkv_ln_g, rwkv_ln_b, peer_w_q, peer_sub_keys, peer_u, peer_v):
    B, S, D = x.shape
    T = B * S
    depth = ada_w.shape[0]
    bf = lambda t: t.astype(BF16)
    mod = _adaln(c, ada_w, ada_b)
    x2d = x.reshape(T, D)
    for layer in range(depth):
        sh_m, sc_m, gt_m, sh_f, sc_f, gt_f = [mod[layer, :, i * D:(i + 1) * D] for i in range(6)]
        g_mix = norm_mix_g[layer]
        kind, j = layer % 3, layer // 3
        if kind == 0:
            proj = _nm_matmul(x2d, g_mix, sc_m, sh_m, bf(ret_w_in[j]), rows_per_batch=S, tn=2048)
            o = _retention_scan(proj.reshape(B, S, -1), ret_gn_g[j], ret_gn_b[j])
            x2d = _mm_res(o.reshape(T, -1), bf(ret_w_out[j]), x2d, gt_m, rows_per_batch=S)
        elif kind == 1:
            w = gdn_w_in[j]
            wide = GDN_QKV + GDN_HEADS * GDN_DV
            proj = _nm_matmul(x2d, g_mix, sc_m, sh_m, bf(w[:, :wide]), rows_per_batch=S, tn=2048)
            proj_ab = _nm_matmul(x2d, g_mix, sc_m, sh_m, bf(_pad_cols(w[:, wide:], LANES)), rows_per_batch=S)
            o = _gdn_scan(proj.reshape(B, S, -1), proj_ab.reshape(B, S, -1), gdn_conv_w[j], gdn_a_log[j],
                          gdn_dt_bias[j], gdn_norm_g[j])
            x2d = _mm_res(o.reshape(T, -1), bf(gdn_w_out[j]), x2d, gt_m, rows_per_batch=S)
        else:
            x2d = _rwkv_mixer(x2d.reshape(B, S, D), g_mix, sc_m, sh_m, gt_m, rwkv_mu[j], rwkv_w_r[j], rwkv_w_k[j],
                              rwkv_w_v[j], rwkv_w_o[j], rwkv_w0[j], rwkv_w1[j], rwkv_w2[j], rwkv_a0[j], rwkv_a1[j],
                              rwkv_a2[j], rwkv_g1[j], rwkv_g2[j], rwkv_k_k[j], rwkv_k_a[j], rwkv_r_k[j],
                              rwkv_ln_g[j], rwkv_ln_b[j]).reshape(T, D)
        x2d = _peer_ffn(x2d, norm_ffn_g[layer], sc_f, sh_f, gt_f, bf(peer_w_q[layer]), peer_sub_keys[layer],
                        peer_u[layer], peer_v[layer], rows_per_batch=S)
    return _final_norm(x2d, final_norm_g).reshape(B, S, D)
```

```python
import functools
import math

import jax
import jax.numpy as jnp
from jax import lax
from jax.experimental import pallas as pl
from jax.experimental.pallas import tpu as pltpu
from jax.experimental.pallas import tpu_sc as plsc

F32 = jnp.float32
BF16 = jnp.bfloat16

D_MODEL = 1024
NORM_EPS = 1e-6

RET_HEADS = 4
RET_DK = D_MODEL // RET_HEADS
RET_DV = 2 * D_MODEL // RET_HEADS
RET_CHUNK = 128
RET_ROPE_BASE = 10000.0
RET_GN_EPS = 1e-5

GDN_HEADS = 8
GDN_DK = D_MODEL // GDN_HEADS
GDN_DV = D_MODEL // GDN_HEADS
GDN_CONV = 4
GDN_CHUNK = 64
GDN_QKV = GDN_HEADS * (2 * GDN_DK + GDN_DV)

RWKV_HEAD = 64
RWKV_HEADS = D_MODEL // RWKV_HEAD
RWKV_GN_EPS = 64e-5

PEER_KEYS = 128
PEER_HEADS = 8
PEER_DQ = 256
PEER_TOPK = 16
PEER_SEL = PEER_HEADS * PEER_TOPK

LANES = 128
SUBLANES = 8
VMEM_LIMIT = 56 * 1024 * 1024


def _cparams(sem):
    return pltpu.CompilerParams(dimension_semantics=sem, vmem_limit_bytes=VMEM_LIMIT)


def _bdot(a, b):
    return jnp.dot(a.astype(BF16), b.astype(BF16), preferred_element_type=F32)


def _bdot_nt(a, b):
    return lax.dot_general(a.astype(BF16), b.astype(BF16), (((1,), (1,)), ((), ())),
                           preferred_element_type=F32)


def _bdot_tn(a, b):
    return lax.dot_general(a.astype(BF16), b.astype(BF16), (((0,), (0,)), ((), ())),
                           preferred_element_type=F32)


def _sigmoid(x):
    return 1.0 / (1.0 + jnp.exp(-x))


def _silu(x):
    return x * _sigmoid(x)


def _softplus(x):
    return jnp.maximum(x, 0.0) + jnp.log1p(jnp.exp(-jnp.abs(x)))


def _norm_mod(x, g, sc, sh):
    ms = jnp.mean(x * x, axis=-1, keepdims=True)
    return (x * lax.rsqrt(ms + NORM_EPS) * g) * (1.0 + sc) + sh


def _adaln_kernel(c_ref, w_ref, b_ref, o_ref):
    cond = _silu(c_ref[...])
    o_ref[0] = _bdot(cond, w_ref[0]) + b_ref[0]


def _adaln(c, ada_w, ada_b):
    depth, d, n = ada_w.shape
    b = c.shape[0]
    tn = 1024
    return pl.pallas_call(
        _adaln_kernel,
        out_shape=jax.ShapeDtypeStruct((depth, b, n), F32),
        grid=(depth, n // tn),
        in_specs=[
            pl.BlockSpec((b, d), lambda l, j: (0, 0)),
            pl.BlockSpec((1, d, tn), lambda l, j: (l, 0, j)),
            pl.BlockSpec((1, 1, tn), lambda l, j: (l, 0, j)),
        ],
        out_specs=pl.BlockSpec((1, b, tn), lambda l, j: (l, 0, j)),
        compiler_params=_cparams(("parallel", "parallel")),
        name="adaln",
    )(c, ada_w, ada_b.reshape(depth, 1, n))


def _nm_matmul_kernel(x_ref, g_ref, sc_ref, sh_ref, w_ref, o_ref, *h_ref):
    h = _norm_mod(x_ref[...], g_ref[...], sc_ref[0], sh_ref[0])
    o_ref[...] = jnp.dot(h.astype(BF16), w_ref[...], preferred_element_type=F32).astype(o_ref.dtype)
    if h_ref:
        h_ref[0][...] = h


def _nm_matmul(x2d, g, sc, sh, w, *, rows_per_batch, tm=512, tn=None, emit_h=False):
    m, d = x2d.shape
    n = w.shape[1]
    nb = sc.shape[0]
    if tn is None:
        tn = n
    tpb = rows_per_batch // tm
    out_shape = [jax.ShapeDtypeStruct((m, n), F32)]
    out_specs = [pl.BlockSpec((tm, tn), lambda j, i: (i, j))]
    if emit_h:
        assert tn == n
        out_shape.append(jax.ShapeDtypeStruct((m, d), F32))
        out_specs.append(pl.BlockSpec((tm, d), lambda j, i: (i, 0)))
    res = pl.pallas_call(
        _nm_matmul_kernel,
        out_shape=out_shape,
        grid=(n // tn, m // tm),
        in_specs=[
            pl.BlockSpec((tm, d), lambda j, i: (i, 0)),
            pl.BlockSpec((1, d), lambda j, i: (0, 0)),
            pl.BlockSpec((1, 1, d), lambda j, i: (i // tpb, 0, 0)),
            pl.BlockSpec((1, 1, d), lambda j, i: (i // tpb, 0, 0)),
            pl.BlockSpec((d, tn), lambda j, i: (0, j)),
        ],
        out_specs=out_specs,
        compiler_params=_cparams(("parallel", "parallel")),
        name="norm_mod_matmul",
    )(x2d, g.reshape(1, d), sc.reshape(nb, 1, d), sh.reshape(nb, 1, d), w)
    return res if emit_h else res[0]


def _mm_res_kernel(*refs, has_mul):
    if has_mul:
        a_ref, m_ref, w_ref, r_ref, gt_ref, o_ref = refs
        a = a_ref[...] * m_ref[...]
    else:
        a_ref, w_ref, r_ref, gt_ref, o_ref = refs
        a = a_ref[...]
    y = jnp.dot(a.astype(BF16), w_ref[...], preferred_element_type=F32)
    o_ref[...] = r_ref[...] + gt_ref[0] * y


def _mm_res(a, w, res, gt, *, rows_per_batch, mul=None, tm=512):
    m, k = a.shape
    n = w.shape[1]
    nb = gt.shape[0]
    tpb = rows_per_batch // tm
    ins = [a]
    specs = [pl.BlockSpec((tm, k), lambda i: (i, 0))]
    if mul is not None:
        ins.append(mul)
        specs.append(pl.BlockSpec((tm, k), lambda i: (i, 0)))
    ins += [w, res, gt.reshape(nb, 1, n)]
    specs += [
        pl.BlockSpec((k, n), lambda i: (0, 0)),
        pl.BlockSpec((tm, n), lambda i: (i, 0)),
        pl.BlockSpec((1, 1, n), lambda i: (i // tpb, 0, 0)),
    ]
    return pl.pallas_call(
        functools.partial(_mm_res_kernel, has_mul=mul is not None),
        out_shape=jax.ShapeDtypeStruct((m, n), F32),
        grid=(m // tm,),
        in_specs=specs,
        out_specs=pl.BlockSpec((tm, n), lambda i: (i, 0)),
        compiler_params=_cparams(("parallel",)),
        name="matmul_residual",
    )(*ins)


def _ret_kernel(q_ref, k_ref, v_ref, gate_ref, cos_ref, sin_ref, dintra_ref, dq_ref, dk_ref, dchunk_ref,
                gng_ref, gnb_ref, o_ref, state_ref):
    H, dk, dv = RET_HEADS, RET_DK, RET_DV
    half = dk // 2

    @pl.when(pl.program_id(1) == 0)
    def _():
        state_ref[...] = jnp.zeros_like(state_ref)

    cos = cos_ref[...]
    sin = sin_ref[...]

    def rot(ref, h):
        x1 = ref[0, :, h * dk:h * dk + half]
        x2 = ref[0, :, h * dk + half:(h + 1) * dk]
        return jnp.concatenate([x1 * cos - x2 * sin, x1 * sin + x2 * cos], axis=-1)

    for h in range(H):
        q = rot(q_ref, h)
        k = rot(k_ref, h) * (dk ** -0.5)
        v = v_ref[0, :, h * dv:(h + 1) * dv]
        scores = _bdot_nt(q, k) * dintra_ref[h]
        st = state_ref[h]
        o = _bdot(scores, v) + _bdot(q, st) * dq_ref[h]
        state_ref[h] = st * dchunk_ref[h] + _bdot_tn(k * dk_ref[h], v)
        mu = jnp.mean(o, axis=-1, keepdims=True)
        var = jnp.mean(jnp.square(o - mu), axis=-1, keepdims=True)
        on = (o - mu) * lax.rsqrt(var + RET_GN_EPS) * gng_ref[h] + gnb_ref[h]
        g = gate_ref[0, :, h * dv:(h + 1) * dv]
        o_ref[0, :, h * dv:(h + 1) * dv] = (_silu(g) * on).astype(o_ref.dtype)


def _retention_scan(proj, gn_g, gn_b):
    B, S, _ = proj.shape
    H, dk, dv, C = RET_HEADS, RET_DK, RET_DV, RET_CHUNK
    half = dk // 2
    N = S // C
    inv_freq = RET_ROPE_BASE ** (-jnp.arange(half, dtype=F32) / half)
    ang = jnp.arange(S, dtype=F32)[:, None] * inv_freq[None, :]
    cos, sin = jnp.cos(ang), jnp.sin(ang)
    log_gamma = jnp.log1p(-jnp.exp2(-5.0 - jnp.arange(H, dtype=F32)))
    idx = jnp.arange(C, dtype=F32)
    diff = idx[:, None] - idx[None, :]
    causal = diff >= 0
    d_intra = jnp.where(causal[None], jnp.exp(jnp.where(causal, diff, 0.0)[None] * log_gamma[:, None, None]), 0.0)
    d_q = jnp.exp((idx[None, :] + 1.0) * log_gamma[:, None])[:, :, None]
    d_k = jnp.exp((C - 1.0 - idx)[None, :] * log_gamma[:, None])[:, :, None]
    d_chunk = jnp.exp(C * log_gamma)[:, None, None]
    qw, vw = H * dk, H * dv
    return pl.pallas_call(
        _ret_kernel,
        out_shape=jax.ShapeDtypeStruct((B, S, vw), BF16),
        grid=(B, N),
        in_specs=[
            pl.BlockSpec((1, C, qw), lambda b, n: (b, n, 0)),
            pl.BlockSpec((1, C, qw), lambda b, n: (b, n, 1)),
            pl.BlockSpec((1, C, vw), lambda b, n: (b, n, 1)),
            pl.BlockSpec((1, C, vw), lambda b, n: (b, n, 2)),
            pl.BlockSpec((C, half), lambda b, n: (n, 0)),
            pl.BlockSpec((C, half), lambda b, n: (n, 0)),
            pl.BlockSpec((H, C, C), lambda b, n: (0, 0, 0)),
            pl.BlockSpec((H, C, 1), lambda b, n: (0, 0, 0)),
            pl.BlockSpec((H, C, 1), lambda b, n: (0, 0, 0)),
            pl.BlockSpec((H, 1, 1), lambda b, n: (0, 0, 0)),
            pl.BlockSpec((H, 1, dv), lambda b, n: (0, 0, 0)),
            pl.BlockSpec((H, 1, dv), lambda b, n: (0, 0, 0)),
        ],
        out_specs=pl.BlockSpec((1, C, vw), lambda b, n: (b, n, 0)),
        scratch_shapes=[pltpu.VMEM((H, dk, dv), F32)],
        compiler_params=_cparams(("parallel", "arbitrary")),
        name="retention_scan",
    )(proj, proj, proj, proj, cos, sin, d_intra, d_q, d_k, d_chunk,
      gn_g.reshape(H, 1, dv), gn_b.reshape(H, 1, dv))


def _shift_rows(cur, prev8, s):
    rows = lax.broadcasted_iota(jnp.int32, cur.shape, 0)
    rolled = pltpu.roll(cur, s, axis=0)
    head = pltpu.roll(prev8, s, axis=0)
    head = jnp.concatenate([head, jnp.zeros((cur.shape[0] - SUBLANES, cur.shape[1]), cur.dtype)], axis=0)
    return jnp.where(rows < s, head, rolled)


def _cumsum_rows(x):
    rows = lax.broadcasted_iota(jnp.int32, x.shape, 0)
    s = 1
    while s < x.shape[0]:
        x = x + jnp.where(rows >= s, pltpu.roll(x, s, axis=0), 0.0)
        s *= 2
    return x


def _gdn_kernel(qkv_ref, gate_ref, ab_ref, cw_ref, alog_ref, dtb_ref, ng_ref, o_ref, state_ref, prev_ref):
    C, dk, H = GDN_CHUNK, GDN_DK, GDN_HEADS

    @pl.when(pl.program_id(1) == 0)
    def _():
        state_ref[...] = jnp.zeros_like(state_ref)
        prev_ref[...] = jnp.zeros_like(prev_ref)

    def conv_silu(col):
        cur = qkv_ref[0, :, col * dk:(col + 1) * dk]
        prev8 = prev_ref[:, col * dk:(col + 1) * dk]
        cw = cw_ref[:, col * dk:(col + 1) * dk]
        acc = cur * cw[GDN_CONV - 1:GDN_CONV]
        for s in range(1, GDN_CONV):
            acc = acc + _shift_rows(cur, prev8, s) * cw[GDN_CONV - 1 - s:GDN_CONV - s]
        prev_ref[:, col * dk:(col + 1) * dk] = cur[C - SUBLANES:]
        return _silu(acc)

    ab = ab_ref[0]
    g_all = -jnp.exp(alog_ref[...]) * _softplus(ab + dtb_ref[...])
    beta_all = _sigmoid(ab)
    ri = lax.broadcasted_iota(jnp.int32, (C, C), 0)
    ci = lax.broadcasted_iota(jnp.int32, (C, C), 1)
    incl = ri >= ci

    bdot_nt = lambda x, y: jnp.einsum('hid,hjd->hij', x.astype(BF16), y.astype(BF16), preferred_element_type=F32)
    bdot = lambda x, y: jnp.einsum('hij,hjk->hik', x, y, preferred_element_type=F32)
    stack = lambda xs: jnp.stack(xs, axis=0)

    def hdot(x, y):
        xh, yh = x.astype(BF16), y.astype(BF16)
        xl, yl = (x - xh.astype(F32)).astype(BF16), (y - yh.astype(F32)).astype(BF16)
        return bdot(xh, yh) + (bdot(xh, yl) + bdot(xl, yh))

    qs, ks, vs, betas, cums = [], [], [], [], []
    for h in range(H):
        q = conv_silu(h)
        k = conv_silu(H + h)
        qs.append(q * lax.rsqrt(jnp.sum(q * q, axis=-1, keepdims=True) + 1e-6) * (dk ** -0.5))
        ks.append(k * lax.rsqrt(jnp.sum(k * k, axis=-1, keepdims=True) + 1e-6))
        vs.append(conv_silu(2 * H + h))
        betas.append(beta_all[:, H + h:H + h + 1])
        cums.append(_cumsum_rows(jnp.broadcast_to(g_all[:, h:h + 1], (C, LANES))))
    q, k, v, beta, cum = stack(qs), stack(ks), stack(vs), stack(betas), stack(cums)
    cum_c = cum[:, :, :1]
    cum_last = cum[:, C - 1:C, :1]
    cum_r = stack([cums[h].T[:C, :] for h in range(H)])
    decay = jnp.where(incl, jnp.exp(jnp.where(incl, cum[:, :, :C] - cum_r, 0.0)), 0.0)

    L = jnp.where(ri > ci, bdot_nt(k, k) * decay, 0.0) * beta
    rhs = jnp.concatenate([k * (beta * jnp.exp(cum_c)), v * beta], axis=-1)
    eye = (ri == ci).astype(F32)
    p = -L
    inv = eye + p
    s = 2
    while s < C:
        p = hdot(p, p)
        inv = inv + hdot(inv, p)
        s *= 2
    sol = hdot(inv, rhs)
    a_qk = bdot_nt(q, k) * decay
    q_dec = q * jnp.exp(cum_c)
    k_dec = k * jnp.exp(cum_last - cum_c)
    e_last = jnp.exp(cum_last)

    for h in range(H):
        st = state_ref[h]
        u = sol[h, :, dk:] - _bdot(sol[h, :, :dk], st)
        o = _bdot(q_dec[h], st) + _bdot(a_qk[h], u)
        state_ref[h] = st * e_last[h] + _bdot_tn(k_dec[h], u)
        ms = jnp.mean(o * o, axis=-1, keepdims=True)
        o = o * lax.rsqrt(ms + NORM_EPS) * ng_ref[...]
        o_ref[0, :, h * dk:(h + 1) * dk] = (o * _silu(gate_ref[0, :, h * dk:(h + 1) * dk])).astype(o_ref.dtype)


def _gdn_scan(proj, proj_ab, conv_w, a_log, dt_bias, norm_g):
    B, S, _ = proj.shape
    H, dk, C = GDN_HEADS, GDN_DK, GDN_CHUNK
    N = S // C
    pad = lambda t: jnp.pad(t.astype(F32), (0, LANES - H)).reshape(1, LANES)
    row = pl.BlockSpec((1, LANES), lambda b, n: (0, 0))
    return pl.pallas_call(
        _gdn_kernel,
        out_shape=jax.ShapeDtypeStruct((B, S, H * dk), BF16),
        grid=(B, N),
        in_specs=[pl.BlockSpec((1, C, 3 * H * dk), lambda b, n: (b, n, 0)),
                  pl.BlockSpec((1, C, H * dk), lambda b, n: (b, n, 3)),
                  pl.BlockSpec((1, C, LANES), lambda b, n: (b, n, 0)),
                  pl.BlockSpec((GDN_CONV, 3 * H * dk), lambda b, n: (0, 0)),
                  row, row, row],
        out_specs=pl.BlockSpec((1, C, H * dk), lambda b, n: (b, n, 0)),
        scratch_shapes=[pltpu.VMEM((H, dk, dk), F32), pltpu.VMEM((SUBLANES, 3 * H * dk), F32)],
        compiler_params=_cparams(("parallel", "arbitrary")),
        name="gdn_scan",
    )(proj, proj, proj_ab, conv_w, pad(a_log), pad(dt_bias), norm_g.reshape(1, dk))


def _rwkv_proj_kernel(x_ref, xp_ref, g_ref, sc_ref, sh_ref, mu_ref, wr_ref, wk_ref, wv_ref, w1_ref, w2_ref,
                      a1_ref, a2_ref, g1_ref, g2_ref, w0_ref, a0_ref, kk_ref, ka_ref,
                      r_o, dec_o, k_o, v_o, kk_o, a_o, g_o, *, tiles_per_seq):
    h = _norm_mod(x_ref[...], g_ref[...], sc_ref[0], sh_ref[0])
    hp8 = _norm_mod(xp_ref[...], g_ref[...], sc_ref[0], sh_ref[0])
    seq_start = pl.program_id(0) % tiles_per_seq == 0
    first = jnp.where(seq_start, 0.0, hp8[SUBLANES - 1:SUBLANES, :])
    rows = lax.broadcasted_iota(jnp.int32, h.shape, 0)
    xx = jnp.where(rows == 0, first, pltpu.roll(h, 1, axis=0)) - h
    mix = lambda j: h + xx * mu_ref[j:j + 1, :]
    r = _bdot(mix(0), wr_ref[...])
    lw = w0_ref[...] + _bdot(jnp.tanh(_bdot(mix(1), w1_ref[...])), w2_ref[...])
    k = _bdot(mix(2), wk_ref[...])
    v = _bdot(mix(3), wv_ref[...])
    a = _sigmoid(a0_ref[...] + _bdot(_bdot(mix(4), a1_ref[...]), a2_ref[...]))
    g = _bdot(_sigmoid(_bdot(mix(5), g1_ref[...])), g2_ref[...])
    w = -_softplus(-lw) - 0.5
    r_o[...] = r
    dec_o[...] = jnp.exp(-jnp.exp(w))
    k_o[...] = k * (1.0 + (a - 1.0) * ka_ref[...])
    v_o[...] = v
    kk_o[...] = k * kk_ref[...]
    a_o[...] = a
    g_o[...] = g


def _pad_cols(w, n):
    return jnp.pad(w, ((0, 0), (0, n - w.shape[1])))


def _pad_rows(w, n):
    return jnp.pad(w, ((0, n - w.shape[0]), (0, 0)))


def _rwkv_proj(x2d, g, sc, sh, mu, w_r, w_k, w_v, w1, w2, a1, a2, g1, g2, w0, a0, k_k, k_a, *, rows_per_batch, tm=256):
    m, d = x2d.shape
    nb = sc.shape[0]
    tpb = rows_per_batch // tm
    lora_w = LANES * pl.cdiv(w1.shape[1], LANES)
    lora_g = LANES * pl.cdiv(g1.shape[1], LANES)
    bf = lambda t: t.astype(BF16)
    full = lambda a: pl.BlockSpec(a.shape, lambda i: (0,) * a.ndim)
    row = lambda t: t.reshape(1, d)
    ws = [bf(w_r), bf(w_k), bf(w_v), bf(_pad_cols(w1, lora_w)), bf(_pad_rows(w2, lora_w)),
          bf(_pad_cols(a1, lora_w)), bf(_pad_rows(a2, lora_w)), bf(_pad_cols(g1, lora_g)), bf(_pad_rows(g2, lora_g)),
          row(w0), row(a0), row(k_k), row(k_a)]
    tile = pl.BlockSpec((tm, d), lambda i: (i, 0))
    return pl.pallas_call(
        functools.partial(_rwkv_proj_kernel, tiles_per_seq=tpb),
        out_shape=[jax.ShapeDtypeStruct((m, d), F32)] * 7,
        grid=(m // tm,),
        in_specs=[
            tile,
            pl.BlockSpec((SUBLANES, d), lambda i: (jnp.maximum(i * (tm // SUBLANES) - 1, 0), 0)),
            pl.BlockSpec((1, d), lambda i: (0, 0)),
            pl.BlockSpec((1, 1, d), lambda i: (i // tpb, 0, 0)),
            pl.BlockSpec((1, 1, d), lambda i: (i // tpb, 0, 0)),
            full(mu),
        ] + [full(w) for w in ws],
        out_specs=[tile] * 7,
        compiler_params=_cparams(("parallel",)),
        name="rwkv_proj",
    )(x2d, x2d, g.reshape(1, d), sc.reshape(nb, 1, d), sh.reshape(nb, 1, d), mu, *ws)


RWKV_VUNROLL = 32


def _rwkv_scan_kernel(r_ref, w_ref, k_ref, v_ref, kk_ref, a_ref, rk_ref, lng_ref, lnb_ref, y_ref, state_ref, yrow_ref):
    n = RWKV_HEAD

    @pl.when(pl.program_id(0) == 0)
    def _():
        state_ref[...] = jnp.zeros_like(state_ref)

    def step(t, carry):
        r, w, k, kkr, a = r_ref[t], w_ref[t], k_ref[t], kk_ref[t], a_ref[t]
        kk = kkr * lax.rsqrt(jnp.sum(kkr * kkr, axis=0, keepdims=True) + 1e-6)
        nkk = -kk
        kka = kk * a

        def vloop(vb, c):
            for j in range(RWKV_VUNROLL):
                vi = vb * RWKV_VUNROLL + j
                sv = state_ref[vi]
                sa = jnp.sum(sv * nkk, axis=0, keepdims=True)
                vrow = v_ref[t, pl.ds(vi, 1), :]
                sn = sv * w + sa * kka + vrow * k
                state_ref[vi] = sn
                yrow_ref[pl.ds(vi, 1), :] = jnp.sum(sn * r, axis=0, keepdims=True)
            return c

        lax.fori_loop(0, n // RWKV_VUNROLL, vloop, 0)
        y = yrow_ref[...]
        mu = jnp.mean(y, axis=0, keepdims=True)
        var = jnp.mean(jnp.square(y - mu), axis=0, keepdims=True)
        yn = (y - mu) * lax.rsqrt(var + RWKV_GN_EPS) * lng_ref[...] + lnb_ref[...]
        bonus = jnp.sum(r * k * rk_ref[...], axis=0, keepdims=True)
        y_ref[t] = yn + bonus * v_ref[t]
        return carry

    lax.fori_loop(0, r_ref.shape[0], step, 0)


def _rwkv_scan(r, dec, k, v, kk, a, r_k, ln_g, ln_b, *, tc=32):
    B, S, D = r.shape
    H, n = RWKV_HEADS, RWKV_HEAD
    lanes = B * H
    to_scan = lambda t: jnp.transpose(t.reshape(B, S, H, n), (1, 3, 0, 2)).reshape(S, n, lanes)
    per_head = lambda p: jnp.tile(p.T, (1, B))
    blk = pl.BlockSpec((tc, n, lanes), lambda i: (i, 0, 0))
    cst = pl.BlockSpec((n, lanes), lambda i: (0, 0))
    y = pl.pallas_call(
        _rwkv_scan_kernel,
        out_shape=jax.ShapeDtypeStruct((S, n, lanes), F32),
        grid=(S // tc,),
        in_specs=[blk] * 6 + [cst] * 3,
        out_specs=blk,
        scratch_shapes=[pltpu.VMEM((n, n, lanes), F32), pltpu.VMEM((n, lanes), F32)],
        compiler_params=_cparams(("arbitrary",)),
        name="rwkv_scan",
    )(to_scan(r), to_scan(dec), to_scan(k), to_scan(v), to_scan(kk), to_scan(a),
      per_head(r_k), per_head(ln_g), per_head(ln_b))
    return jnp.transpose(y.reshape(S, n, B, H), (2, 0, 3, 1)).reshape(B, S, D)


def _rwkv_mixer(x, g, sc, sh, gt, mu, w_r, w_k, w_v, w_o, w0, w1, w2, a0, a1, a2, g1, g2, k_k, k_a, r_k, ln_g, ln_b):
    B, S, D = x.shape
    x2d = x.reshape(B * S, D)
    tm = min(256, S)
    r, dec, k, v, kk, a, gg = _rwkv_proj(x2d, g, sc, sh, mu, w_r, w_k, w_v, w1, w2, a1, a2, g1, g2, w0, a0, k_k, k_a,
                                         rows_per_batch=S, tm=tm)
    sh3 = lambda t: t.reshape(B, S, D)
    y = _rwkv_scan(sh3(r), sh3(dec), sh3(k), sh3(v), sh3(kk), sh3(a), r_k, ln_g, ln_b, tc=min(32, S))
    out = _mm_res(y.reshape(B * S, D), w_o.astype(BF16), x2d, gt, rows_per_batch=S, mul=gg, tm=min(512, S))
    return out.reshape(B, S, D)


def _rwkv_mixer_test(x, g, sc, sh, p):
    gt = jnp.ones_like(sc)
    return _rwkv_mixer(x, g, sc, sh, gt, p['mu'], p['w_r'], p['w_k'], p['w_v'], p['w_o'], p['w0'], p['w1'], p['w2'],
                       p['a0'], p['a1'], p['a2'], p['g1'], p['g2'], p['k_k'], p['k_a'], p['r_k'], p['ln_g'],
                       p['ln_b']) - x


def _topk_rows(s, k, rows=None):
    if rows is None:
        rows = lax.broadcasted_iota(jnp.int32, s.shape, 0)
    n = jnp.iinfo(jnp.int32).max
    vals, ids = [], []
    for _ in range(k):
        m = jnp.max(s, axis=0, keepdims=True)
        idx = jnp.min(jnp.where(s == m, rows, n), axis=0, keepdims=True)
        vals.append(m)
        ids.append(idx)
        s = jnp.where(rows == idx, -jnp.inf, s)
    return jnp.concatenate(vals, axis=0), jnp.concatenate(ids, axis=0)


def _take_rows(table, pos):
    out = jnp.zeros(pos.shape, table.dtype)
    for m in range(table.shape[0]):
        out = jnp.where(pos == m, table[m:m + 1, :], out)
    return out


_PEER_CAND = [(i, PEER_TOPK // (i + 1)) for i in range(PEER_TOPK)]
_PEER_NCAND = sum(n for _, n in _PEER_CAND)
_PEER_NCAND_PAD = SUBLANES * pl.cdiv(_PEER_NCAND, SUBLANES)


def _peer_cand_codes():
    codes = [i * PEER_TOPK + j for i, n in _PEER_CAND for j in range(n)]
    codes += [PEER_TOPK * PEER_TOPK + p for p in range(_PEER_NCAND_PAD - _PEER_NCAND)]
    return jnp.broadcast_to(jnp.asarray(codes, jnp.int32)[:, None], (_PEER_NCAND_PAD, LANES))


def _peer_route_kernel(q_ref, keys_ref, codes_ref, idx_o, gate_o):
    K, half = PEER_TOPK, PEER_DQ // 2

    tm = q_ref.shape[0]
    G = 2
    codes = jnp.concatenate([codes_ref[...]] * (G * tm // LANES), axis=1)
    pad = jnp.full((_PEER_NCAND_PAD - _PEER_NCAND, tm), -jnp.inf, F32)

    def group(hg, carry):
        ss = []
        for dh in range(G):
            for p in range(2):
                c = pl.multiple_of((2 * (G * hg + dh) + p) * half, half)
                ss.append(_bdot_nt(keys_ref[G * hg + dh, p], q_ref[:, pl.ds(c, half)]))
        vals, ids = _topk_rows(jnp.concatenate(ss, axis=1), K)
        part = lambda t, j: t[:, j * tm:(j + 1) * tm]
        cands = []
        for dh in range(G):
            va, vb = part(vals, 2 * dh), part(vals, 2 * dh + 1)
            cands.append(jnp.concatenate([va[i:i + 1, :] + vb[:n, :] for i, n in _PEER_CAND] + [pad], axis=0))
        best, pos = _topk_rows(jnp.concatenate(cands, axis=1), K, codes)
        for dh in range(G):
            ia, ib = part(ids, 2 * dh), part(ids, 2 * dh + 1)
            ps, bs = part(pos, dh), part(best, dh)
            expert = _take_rows(ia, ps // K) * PEER_KEYS + _take_rows(ib, ps % K)
            e = jnp.exp(bs - bs[0:1, :])
            r0 = pl.multiple_of((G * hg + dh) * K, K)
            idx_o[pl.ds(r0, K), :] = expert
            gate_o[pl.ds(r0, K), :] = e / jnp.sum(e, axis=0, keepdims=True)
        return carry

    lax.fori_loop(0, PEER_HEADS // G, group, 0)


def _peer_route(q, sub_keys, *, tm=256):
    t, n = q.shape
    blk = pl.BlockSpec((PEER_SEL, tm), lambda i: (0, i))
    return pl.pallas_call(
        _peer_route_kernel,
        out_shape=[jax.ShapeDtypeStruct((PEER_SEL, t), jnp.int32), jax.ShapeDtypeStruct((PEER_SEL, t), F32)],
        grid=(t // tm,),
        in_specs=[pl.BlockSpec((tm, n), lambda i: (i, 0)),
                  pl.BlockSpec(sub_keys.shape, lambda i: (0, 0, 0, 0)),
                  pl.BlockSpec((_PEER_NCAND_PAD, LANES), lambda i: (0, 0))],
        out_specs=[blk, blk],
        compiler_params=_cparams(("parallel",)),
        name="peer_route",
    )(q, sub_keys.astype(BF16), _peer_cand_codes())


def _pack_kernel(u_ref, v_ref, uv_o, u2_o, v2_o):
    half = u_ref.shape[1] // 2
    high = jnp.int32(-65536)
    bits = lambda x: lax.bitcast_convert_type(x.astype(BF16).astype(F32), jnp.int32)
    low = lambda b: lax.shift_right_logical(b, 16)
    ub, vb = bits(u_ref[...]), bits(v_ref[...])
    uv_o[...] = (vb & high) | low(ub)
    u2_o[...] = (ub[:, half:] & high) | low(ub[:, :half])
    v2_o[...] = (vb[:, half:] & high) | low(vb[:, :half])


def _pack_tables(u, v, *, tm=512):
    e, d = u.shape
    tile = lambda w: pl.BlockSpec((tm, w), lambda i: (i, 0))
    uv, u2, v2 = pl.pallas_call(
        _pack_kernel,
        out_shape=[jax.ShapeDtypeStruct((e, d), jnp.int32), jax.ShapeDtypeStruct((e, d // 2), jnp.int32),
                   jax.ShapeDtypeStruct((e, d // 2), jnp.int32)],
        grid=(e // tm,),
        in_specs=[tile(d), tile(d)],
        out_specs=[tile(d), tile(d // 2), tile(d // 2)],
        compiler_params=_cparams(("parallel",)),
        name="peer_pack",
    )(u, v)
    return uv.reshape(e, d // LANES, LANES), u2, v2


PEER_NBUF = 8


def _gelu(x):
    return 0.5 * x * (1.0 + lax.erf(x * (2.0 ** -0.5)))


def _peer_eval(packed, x, gate):
    u = lax.bitcast_convert_type(packed << 16, F32)
    v = lax.bitcast_convert_type(packed & jnp.int32(-65536), F32)
    act = jnp.sum(jnp.sum(u * x[None], axis=1), axis=1, keepdims=True)
    wgt = gate * _gelu(act)
    return jnp.sum(v * wgt[:, :, None], axis=0)


def _gate_column(gate_ref, t):
    lane = lax.broadcasted_iota(jnp.int32, gate_ref.shape, 1)
    return jnp.sum(jnp.where(lane == t, gate_ref[...], 0.0), axis=1, keepdims=True)


def _peer_expert_kernel(idx_hbm, gate_ref, h_ref, xres_ref, gt_ref, uv_hbm, *rest, tok0):
    o_ref, idx_smem, buf, sem_idx, sem = rest[-5:]
    tb = h_ref.shape[0]
    nsel = PEER_SEL
    base = (pl.program_id(0) * tb + tok0) * nsel
    cp = pltpu.make_async_copy(idx_hbm.at[pl.ds(base, tb * nsel)], idx_smem, sem_idx)
    cp.start()
    cp.wait()

    def issue(t, slot):
        for k in range(nsel):
            e = idx_smem[t * nsel + k]
            pltpu.make_async_copy(uv_hbm.at[e], buf.at[slot, k], sem.at[slot]).start()

    def wait(slot):
        pltpu.make_async_copy(uv_hbm.at[pl.ds(0, nsel)], buf.at[slot], sem.at[slot]).wait()

    for t0 in range(PEER_NBUF - 1):
        issue(t0, t0)

    def body(t, carry):
        slot = t % PEER_NBUF
        nxt = t + PEER_NBUF - 1

        @pl.when(nxt < tb)
        def _():
            issue(nxt, nxt % PEER_NBUF)

        wait(slot)
        out = _peer_eval(buf[slot], h_ref[t], _gate_column(gate_ref, t))
        o_ref[t] = xres_ref[t] + gt_ref[0] * out
        return carry

    lax.fori_loop(0, tb, body, 0)


def _peer_expert(idx_flat, gate_t, h3, xres3, gt3, uv, *, rows_per_batch, tok0, ntok, hx0, tb=128, after=None):
    _, c, _ = h3.shape
    tpb = rows_per_batch // tb
    b0 = tok0 // tb
    hb0 = (tok0 - hx0) // tb
    tok = pl.BlockSpec((tb, c, LANES), lambda i: (i + hb0, 0, 0))
    return pl.pallas_call(
        functools.partial(_peer_expert_kernel, tok0=tok0),
        out_shape=jax.ShapeDtypeStruct((ntok, c, LANES), F32),
        grid=(ntok // tb,),
        in_specs=[
            pl.BlockSpec(memory_space=pl.ANY),
            pl.BlockSpec((PEER_SEL, tb), lambda i: (0, i + b0)),
            tok, tok,
            pl.BlockSpec((1, c, LANES), lambda i: ((i + b0) // tpb, 0, 0)),
            pl.BlockSpec(memory_space=pl.ANY),
        ] + ([] if after is None else [pl.BlockSpec(memory_space=pl.ANY)]),
        out_specs=pl.BlockSpec((tb, c, LANES), lambda i: (i, 0, 0)),
        scratch_shapes=[
            pltpu.SMEM((tb * PEER_SEL,), jnp.int32),
            pltpu.VMEM((PEER_NBUF, PEER_SEL, c, LANES), jnp.int32),
            pltpu.SemaphoreType.DMA,
            pltpu.SemaphoreType.DMA((PEER_NBUF,)),
        ],
        compiler_params=_cparams(("arbitrary",)),
        name="peer_expert",
    )(idx_flat, gate_t, h3, xres3, gt3, uv, *(() if after is None else (after,)))


SC_CORES = 2
SC_SUBCORES = 16
SC_WORKERS = SC_CORES * SC_SUBCORES
SC_LANES = 16
SC_TOKENS = 8
SC_GATHER_ROWS = 32
SC_CHUNKS = 8
_SC_PARAMS = pltpu.CompilerParams(needs_layout_passes=False)


def _sc_unpack(w):
    return (lax.bitcast_convert_type(w << 16, F32), lax.bitcast_convert_type(w & jnp.int32(-65536), F32))


def _sc_mesh():
    return plsc.VectorSubcoreMesh(core_axis_name="core", subcore_axis_name="subcore")


def _sc_act(u2, idx_flat, h_flat, *, tok0, ntok, d):
    nsel, L, G, CG, TBK = PEER_SEL, SC_LANES, SC_GATHER_ROWS, SC_CHUNKS, SC_TOKENS
    half = d // 2
    tpw = ntok // SC_WORKERS
    ng = nsel // G
    ngt = TBK * ng

    @functools.partial(
        pl.kernel, out_type=jax.ShapeDtypeStruct((ntok * nsel,), F32), mesh=_sc_mesh(),
        scratch_types=[pltpu.VMEM((TBK * nsel,), jnp.int32), pltpu.VMEM((TBK * d,), F32),
                       pltpu.VMEM((2, G, half), jnp.int32), pltpu.VMEM((TBK * nsel,), F32),
                       pltpu.VMEM((G * L,), F32), pltpu.SemaphoreType.DMA((2,))],
        compiler_params=_SC_PARAMS, name="peer_sc_act")
    def k(u_hbm, i_hbm, x_hbm, o_hbm, idx_v, x_v, buf, act_v, acc_v, sem):
        wid = lax.axis_index("core") * SC_SUBCORES + lax.axis_index("subcore")
        lanes = lax.iota(jnp.int32, L)

        def gather(gi, b):
            return pltpu.make_async_copy(u_hbm.at[idx_v.at[pl.ds(gi * G, G)]], buf.at[b], sem.at[b])

        @pl.loop(0, tpw // TBK)
        def _(bi):
            tl = wid * tpw + bi * TBK
            pltpu.sync_copy(i_hbm.at[pl.ds((tok0 + tl) * nsel, TBK * nsel)], idx_v)
            pltpu.sync_copy(x_hbm.at[pl.ds((tok0 + tl) * d, TBK * d)], x_v)
            gather(0, 0).start()

            @pl.loop(0, ngt, step=2)
            def _(g0):
                for b in range(2):
                    gi = g0 + b

                    @pl.when(gi + 1 < ngt)
                    def _():
                        gather(gi + 1, 1 - b).start()

                    gather(gi, b).wait()
                    xbase = (gi // ng) * d
                    for jg in range(half // (CG * L)):
                        xl = [x_v[pl.ds(pl.multiple_of(xbase + (jg * CG + c) * L, L), L)] for c in range(CG)]
                        xh = [x_v[pl.ds(pl.multiple_of(xbase + half + (jg * CG + c) * L, L), L)] for c in range(CG)]

                        def row(r, carry):
                            ps = []
                            for c in range(CG):
                                lo, hi = _sc_unpack(buf[b, r, pl.ds((jg * CG + c) * L, L)])
                                ps.append(lo * xl[c] + hi * xh[c])
                            while len(ps) > 1:
                                ps = [ps[i] + ps[i + 1] for i in range(0, len(ps), 2)]
                            off = pl.multiple_of(r * L, L)
                            if jg == 0:
                                acc_v[pl.ds(off, L)] = ps[0]
                            else:
                                acc_v[pl.ds(off, L)] = acc_v[pl.ds(off, L)] + ps[0]
                            return carry

                        plsc.parallel_loop(0, G, carry=jnp.int32(0))(row)
                    for part in range(G // L):
                        vec = jnp.zeros((L,), F32)
                        for r in range(L):
                            vec = jnp.where(lanes == r, jnp.sum(acc_v[pl.ds((part * L + r) * L, L)]), vec)
                        act_v[pl.ds(pl.multiple_of(gi * G + part * L, L), L)] = vec

            pltpu.sync_copy(act_v, o_hbm.at[pl.ds(tl * nsel, TBK * nsel)])

    return k(u2, idx_flat, h_flat)


def _sc_out(v2, idx_flat, w_flat, xres_flat, gt_flat, *, tok0, ntok, d, rows_per_batch):
    nsel, L, G, CG, TBK = PEER_SEL, SC_LANES, SC_GATHER_ROWS, SC_CHUNKS, SC_TOKENS
    half = d // 2
    tpw = ntok // SC_WORKERS
    ng = nsel // G
    ngt = TBK * ng

    @functools.partial(
        pl.kernel, out_type=jax.ShapeDtypeStruct((ntok * d,), F32), mesh=_sc_mesh(),
        scratch_types=[pltpu.VMEM((TBK * nsel,), jnp.int32), pltpu.VMEM((TBK * nsel,), F32),
                       pltpu.VMEM((2, G, half), jnp.int32), pltpu.VMEM((TBK * d,), F32),
                       pltpu.VMEM((TBK * d,), F32), pltpu.VMEM((d,), F32), pltpu.SemaphoreType.DMA((2,))],
        compiler_params=_SC_PARAMS, name="peer_sc_out")
    def k(v_hbm, i_hbm, w_hbm, xr_hbm, gt_hbm, o_hbm, idx_v, w_v, buf, out_v, xr_v, gt_v, sem):
        wid = lax.axis_index("core") * SC_SUBCORES + lax.axis_index("subcore")

        def gather(gi, b):
            return pltpu.make_async_copy(v_hbm.at[idx_v.at[pl.ds(gi * G, G)]], buf.at[b], sem.at[b])

        @pl.loop(0, tpw // TBK)
        def _(bi):
            tl = wid * tpw + bi * TBK
            pltpu.sync_copy(i_hbm.at[pl.ds((tok0 + tl) * nsel, TBK * nsel)], idx_v)
            pltpu.sync_copy(w_hbm.at[pl.ds(tl * nsel, TBK * nsel)], w_v)
            gather(0, 0).start()
            pltpu.sync_copy(xr_hbm.at[pl.ds((tok0 + tl) * d, TBK * d)], xr_v)
            pltpu.sync_copy(gt_hbm.at[pl.ds(((tok0 + tl) // rows_per_batch) * d, d)], gt_v)

            @pl.loop(0, TBK * d // L)
            def _(i):
                out_v[pl.ds(pl.multiple_of(i * L, L), L)] = jnp.zeros((L,), F32)

            @pl.loop(0, ngt, step=2)
            def _(g0):
                for b in range(2):
                    gi = g0 + b

                    @pl.when(gi + 1 < ngt)
                    def _():
                        gather(gi + 1, 1 - b).start()

                    gather(gi, b).wait()
                    obase = (gi // ng) * d
                    for jg in range(half // (CG * L)):
                        lo0 = obase + jg * CG * L
                        hi0 = lo0 + half
                        accs = tuple(out_v[pl.ds(pl.multiple_of(lo0 + c * L, L), L)] for c in range(CG)) + \
                            tuple(out_v[pl.ds(pl.multiple_of(hi0 + c * L, L), L)] for c in range(CG))

                        def row(r, accs):
                            wk = plsc.load_gather(w_v, [jnp.full((L,), gi * G + r, jnp.int32)])
                            new = list(accs)
                            for c in range(CG):
                                lo, hi = _sc_unpack(buf[b, r, pl.ds((jg * CG + c) * L, L)])
                                new[c] = new[c] + wk * lo
                                new[CG + c] = new[CG + c] + wk * hi
                            return tuple(new)

                        accs = lax.fori_loop(0, G, row, accs)
                        for c in range(CG):
                            out_v[pl.ds(pl.multiple_of(lo0 + c * L, L), L)] = accs[c]
                            out_v[pl.ds(pl.multiple_of(hi0 + c * L, L), L)] = accs[CG + c]

            @pl.loop(0, TBK * d // L)
            def _(i):
                off = pl.multiple_of(i * L, L)
                goff = pl.multiple_of((i % (d // L)) * L, L)
                out_v[pl.ds(off, L)] = xr_v[pl.ds(off, L)] + gt_v[pl.ds(goff, L)] * out_v[pl.ds(off, L)]

            pltpu.sync_copy(out_v, o_hbm.at[pl.ds(tl * d, TBK * d)])

    return k(v2, idx_flat, w_flat, xres_flat, gt_flat)


def _peer_weight_kernel(act_ref, gate_ref, *rest):
    rest[-1][...] = gate_ref[...] * _gelu(act_ref[...])


def _peer_weight(act, gate_tok, *, tok0, tm=1024, after=None):
    n, k = act.shape
    tm = math.gcd(math.gcd(n, tok0), tm)
    assert tm % SUBLANES == 0
    b0 = tok0 // tm
    return pl.pallas_call(
        _peer_weight_kernel,
        out_shape=jax.ShapeDtypeStruct((n, k), F32),
        grid=(n // tm,),
        in_specs=[pl.BlockSpec((tm, k), lambda i: (i, 0)), pl.BlockSpec((tm, k), lambda i: (i + b0, 0))]
        + ([] if after is None else [pl.BlockSpec(memory_space=pl.ANY)]),
        out_specs=pl.BlockSpec((tm, k), lambda i: (i, 0)),
        compiler_params=_cparams(("parallel",)),
        name="peer_weight",
    )(act, gate_tok, *(() if after is None else (after,)))


PEER_SC_SHARE = (19, 32)
PEER_SC_CALLS = 2
PEER_TC_SPLIT = (1, 1, 2)


def _peer_experts(idx_flat, gate_t, h, xres, gt, uv, u2, v2, *, rows_per_batch):
    t, d = h.shape
    c = d // LANES
    tb = min(128, rows_per_batch)
    unit = SC_WORKERS * SC_TOKENS * PEER_SC_CALLS
    n_sc = t * PEER_SC_SHARE[0] // PEER_SC_SHARE[1] // unit * unit
    ch = n_sc // PEER_SC_CALLS
    h3, xres3 = h[n_sc:].reshape(t - n_sc, c, LANES), xres[n_sc:].reshape(t - n_sc, c, LANES)
    gt3 = gt.reshape(-1, c, LANES)
    blocks = (t - n_sc) // tb
    split = PEER_TC_SPLIT if n_sc and blocks >= sum(PEER_TC_SPLIT) else (1,) if blocks else ()
    cum = [sum(split[:j]) for j in range(len(split) + 1)]
    bounds = [n_sc + (blocks * cj // max(cum[-1], 1)) * tb for cj in cum]

    def gather_call(j, after):
        lo, hi = bounds[j], bounds[j + 1]
        return _peer_expert(idx_flat, gate_t, h3, xres3, gt3, uv, rows_per_batch=rows_per_batch, tok0=lo, ntok=hi - lo,
                            hx0=n_sc, tb=tb, after=after)

    sc_outs, tc_outs, last = [], [], None
    if n_sc:
        gate_tok = gate_t.T
        h_flat, xres_flat, gt_flat = h.reshape(-1), xres.reshape(-1), gt.reshape(-1)
        acts = [_sc_act(u2, idx_flat, h_flat, tok0=i * ch, ntok=ch, d=d) for i in range(PEER_SC_CALLS)]
        for i, a in enumerate(acts):
            if len(tc_outs) < len(split):
                tc_outs.append(gather_call(len(tc_outs), last))
                last = tc_outs[-1]
            last = _peer_weight(a.reshape(ch, PEER_SEL), gate_tok, tok0=i * ch, after=last)
            sc_outs.append(_sc_out(v2, idx_flat, last.reshape(-1), xres_flat, gt_flat, tok0=i * ch, ntok=ch, d=d,
                                   rows_per_batch=rows_per_batch).reshape(ch, d))
    while len(tc_outs) < len(split):
        tc_outs.append(gather_call(len(tc_outs), last))
        last = tc_outs[-1]
    return jnp.concatenate(sc_outs + [o.reshape(-1, d) for o in tc_outs], axis=0)


def _peer_ffn(x2d, g, sc, sh, gt, w_q, sub_keys, u, v, *, rows_per_batch):
    tm = min(512, rows_per_batch)
    q, h = _nm_matmul(x2d, g, sc, sh, w_q, rows_per_batch=rows_per_batch, tm=tm, emit_h=True)
    idx_t, gate_t = _peer_route(q, sub_keys, tm=min(256, rows_per_batch))
    return _peer_experts(idx_t.T.reshape(-1), gate_t, h, x2d, gt, *_pack_tables(u, v), rows_per_batch=rows_per_batch)


def _peer_test(x, g, sc, sh, w_q, sub_keys, u, v):
    B, S, D = x.shape
    x2d = x.reshape(B * S, D)
    return (_peer_ffn(x2d, g, sc, sh, jnp.ones_like(sc), w_q.astype(BF16), sub_keys, u, v,
                      rows_per_batch=S) - x2d).reshape(B, S, D)


def _final_norm_kernel(x_ref, g_ref, o_ref):
    x = x_ref[...]
    ms = jnp.mean(x * x, axis=-1, keepdims=True)
    o_ref[...] = x * lax.rsqrt(ms + NORM_EPS) * g_ref[...]


def _final_norm(x2d, g, *, tm=512):
    m, d = x2d.shape
    tile = pl.BlockSpec((tm, d), lambda i: (i, 0))
    return pl.pallas_call(
        _final_norm_kernel,
        out_shape=jax.ShapeDtypeStruct((m, d), F32),
        grid=(m // tm,),
        in_specs=[tile, pl.BlockSpec((1, d), lambda i: (0, 0))],
        out_specs=tile,
        compiler_params=_cparams(("parallel",)),
        name="final_norm",
    )(x2d, g.reshape(1, d))


def kernel(x, c, ada_w, ada_b, norm_mix_g, norm_ffn_g, final_norm_g, ret_w_in, ret_w_out, ret_gn_g, ret_gn_b, gdn_w_in, gdn_conv_w, gdn_a_log, gdn_dt_bias, gdn_norm_g, gdn_w_out, rwkv_mu, rwkv_w_r, rwkv_w_k, rwkv_w_v, rwkv_w_o, rwkv_w0, rwkv_w1, rwkv_w2, rwkv_a0, rwkv_a1, rwkv_a2, rwkv_g1, rwkv_g2, rwkv_k_k, rwkv_k_a, rwkv_r_k, rwkv_ln_g, rwkv_ln_b, peer_w_q, peer_sub_keys, peer_u, peer_v):
    B, S, D = x.shape
    T = B * S
    depth = ada_w.shape[0]
    bf = lambda t: t.astype(BF16)
    mod = _adaln(c, ada_w, ada_b)
    x2d = x.reshape(T, D)
    for layer in range(depth):
        sh_m, sc_m, gt_m, sh_f, sc_f, gt_f = [mod[layer, :, i * D:(i + 1) * D] for i in range(6)]
        g_mix = norm_mix_g[layer]
        kind, j = layer % 3, layer // 3
        if kind == 0:
            proj = _nm_matmul(x2d, g_mix, sc_m, sh_m, bf(ret_w_in[j]), rows_per_batch=S, tn=2048)
            o = _retention_scan(proj.reshape(B, S, -1), ret_gn_g[j], ret_gn_b[j])
            x2d = _mm_res(o.reshape(T, -1), bf(ret_w_out[j]), x2d, gt_m, rows_per_batch=S)
        elif kind == 1:
            w = gdn_w_in[j]
            wide = GDN_QKV + GDN_HEADS * GDN_DV
            proj = _nm_matmul(x2d, g_mix, sc_m, sh_m, bf(w[:, :wide]), rows_per_batch=S, tn=2048)
            proj_ab = _nm_matmul(x2d, g_mix, sc_m, sh_m, bf(_pad_cols(w[:, wide:], LANES)), rows_per_batch=S)
            o = _gdn_scan(proj.reshape(B, S, -1), proj_ab.reshape(B, S, -1), gdn_conv_w[j], gdn_a_log[j],
                          gdn_dt_bias[j], gdn_norm_g[j])
            x2d = _mm_res(o.reshape(T, -1), bf(gdn_w_out[j]), x2d, gt_m, rows_per_batch=S)
        else:
            x2d = _rwkv_mixer(x2d.reshape(B, S, D), g_mix, sc_m, sh_m, gt_m, rwkv_mu[j], rwkv_w_r[j], rwkv_w_k[j],
                              rwkv_w_v[j], rwkv_w_o[j], rwkv_w0[j], rwkv_w1[j], rwkv_w2[j], rwkv_a0[j], rwkv_a1[j],
                              rwkv_a2[j], rwkv_g1[j], rwkv_g2[j], rwkv_k_k[j], rwkv_k_a[j], rwkv_r_k[j],
                              rwkv_ln_g[j], rwkv_ln_b[j]).reshape(T, D)
        x2d = _peer_ffn(x2d, norm_ffn_g[layer], sc_f, sh_f, gt_f, bf(peer_w_q[layer]), peer_sub_keys[layer],
                        peer_u[layer], peer_v[layer], rows_per_batch=S)
    return _final_norm(x2d, final_norm_g).reshape(B, S, D)
```

```python
import functools
import math

import jax
import jax.numpy as jnp
from jax import lax
from jax.experimental import pallas as pl
from jax.experimental.pallas import tpu as pltpu
from jax.experimental.pallas import tpu_sc as plsc

F32 = jnp.float32
BF16 = jnp.bfloat16

D_MODEL = 1024
NORM_EPS = 1e-6

RET_HEADS = 4
RET_DK = D_MODEL // RET_HEADS
RET_DV = 2 * D_MODEL // RET_HEADS
RET_CHUNK = 128
RET_ROPE_BASE = 10000.0
RET_GN_EPS = 1e-5

GDN_HEADS = 8
GDN_DK = D_MODEL // GDN_HEADS
GDN_DV = D_MODEL // GDN_HEADS
GDN_CONV = 4
GDN_CHUNK = 64
GDN_QKV = GDN_HEADS * (2 * GDN_DK + GDN_DV)

RWKV_HEAD = 64
RWKV_HEADS = D_MODEL // RWKV_HEAD
RWKV_GN_EPS = 64e-5

PEER_KEYS = 128
PEER_HEADS = 8
PEER_DQ = 256
PEER_TOPK = 16
PEER_SEL = PEER_HEADS * PEER_TOPK

LANES = 128
SUBLANES = 8
VMEM_LIMIT = 56 * 1024 * 1024


def _cparams(sem):
    return pltpu.CompilerParams(dimension_semantics=sem, vmem_limit_bytes=VMEM_LIMIT)


def _bdot(a, b):
    return jnp.dot(a.astype(BF16), b.astype(BF16), preferred_element_type=F32)


def _bdot_nt(a, b):
    return lax.dot_general(a.astype(BF16), b.astype(BF16), (((1,), (1,)), ((), ())),
                           preferred_element_type=F32)


def _bdot_tn(a, b):
    return lax.dot_general(a.astype(BF16), b.astype(BF16), (((0,), (0,)), ((), ())),
                           preferred_element_type=F32)


def _sigmoid(x):
    return 1.0 / (1.0 + jnp.exp(-x))


def _silu(x):
    return x * _sigmoid(x)


def _softplus(x):
    return jnp.maximum(x, 0.0) + jnp.log1p(jnp.exp(-jnp.abs(x)))


def _norm_mod(x, g, sc, sh):
    ms = jnp.mean(x * x, axis=-1, keepdims=True)
    return (x * lax.rsqrt(ms + NORM_EPS) * g) * (1.0 + sc) + sh


def _adaln_kernel(c_ref, w_ref, b_ref, o_ref):
    cond = _silu(c_ref[...])
    o_ref[0] = _bdot(cond, w_ref[0]) + b_ref[0]


def _adaln(c, ada_w, ada_b):
    depth, d, n = ada_w.shape
    b = c.shape[0]
    tn = 1024
    return pl.pallas_call(
        _adaln_kernel,
        out_shape=jax.ShapeDtypeStruct((depth, b, n), F32),
        grid=(depth, n // tn),
        in_specs=[
            pl.BlockSpec((b, d), lambda l, j: (0, 0)),
            pl.BlockSpec((1, d, tn), lambda l, j: (l, 0, j)),
            pl.BlockSpec((1, 1, tn), lambda l, j: (l, 0, j)),
        ],
        out_specs=pl.BlockSpec((1, b, tn), lambda l, j: (l, 0, j)),
        compiler_params=_cparams(("parallel", "parallel")),
        name="adaln",
    )(c, ada_w, ada_b.reshape(depth, 1, n))


def _nm_matmul_kernel(x_ref, g_ref, sc_ref, sh_ref, w_ref, o_ref, *h_ref):
    h = _norm_mod(x_ref[...], g_ref[...], sc_ref[0], sh_ref[0])
    o_ref[...] = jnp.dot(h.astype(BF16), w_ref[...], preferred_element_type=F32).astype(o_ref.dtype)
    if h_ref:
        h_ref[0][...] = h


def _nm_matmul(x2d, g, sc, sh, w, *, rows_per_batch, tm=512, tn=None, emit_h=False):
    m, d = x2d.shape
    n = w.shape[1]
    nb = sc.shape[0]
    if tn is None:
        tn = n
    tpb = rows_per_batch // tm
    out_shape = [jax.ShapeDtypeStruct((m, n), F32)]
    out_specs = [pl.BlockSpec((tm, tn), lambda j, i: (i, j))]
    if emit_h:
        assert tn == n
        out_shape.append(jax.ShapeDtypeStruct((m, d), F32))
        out_specs.append(pl.BlockSpec((tm, d), lambda j, i: (i, 0)))
    res = pl.pallas_call(
        _nm_matmul_kernel,
        out_shape=out_shape,
        grid=(n // tn, m // tm),
        in_specs=[
            pl.BlockSpec((tm, d), lambda j, i: (i, 0)),
            pl.BlockSpec((1, d), lambda j, i: (0, 0)),
            pl.BlockSpec((1, 1, d), lambda j, i: (i // tpb, 0, 0)),
            pl.BlockSpec((1, 1, d), lambda j, i: (i // tpb, 0, 0)),
            pl.BlockSpec((d, tn), lambda j, i: (0, j)),
        ],
        out_specs=out_specs,
        compiler_params=_cparams(("parallel", "parallel")),
        name="norm_mod_matmul",
    )(x2d, g.reshape(1, d), sc.reshape(nb, 1, d), sh.reshape(nb, 1, d), w)
    return res if emit_h else res[0]


def _mm_res_kernel(*refs, has_mul):
    if has_mul:
        a_ref, m_ref, w_ref, r_ref, gt_ref, o_ref = refs
        a = a_ref[...] * m_ref[...]
    else:
        a_ref, w_ref, r_ref, gt_ref, o_ref = refs
        a = a_ref[...]
    y = jnp.dot(a.astype(BF16), w_ref[...], preferred_element_type=F32)
    o_ref[...] = r_ref[...] + gt_ref[0] * y


def _mm_res(a, w, res, gt, *, rows_per_batch, mul=None, tm=512):
    m, k = a.shape
    n = w.shape[1]
    nb = gt.shape[0]
    tpb = rows_per_batch // tm
    ins = [a]
    specs = [pl.BlockSpec((tm, k), lambda i: (i, 0))]
    if mul is not None:
        ins.append(mul)
        specs.append(pl.BlockSpec((tm, k), lambda i: (i, 0)))
    ins += [w, res, gt.reshape(nb, 1, n)]
    specs += [
        pl.BlockSpec((k, n), lambda i: (0, 0)),
        pl.BlockSpec((tm, n), lambda i: (i, 0)),
        pl.BlockSpec((1, 1, n), lambda i: (i // tpb, 0, 0)),
    ]
    return pl.pallas_call(
        functools.partial(_mm_res_kernel, has_mul=mul is not None),
        out_shape=jax.ShapeDtypeStruct((m, n), F32),
        grid=(m // tm,),
        in_specs=specs,
        out_specs=pl.BlockSpec((tm, n), lambda i: (i, 0)),
        compiler_params=_cparams(("parallel",)),
        name="matmul_residual",
    )(*ins)


def _ret_kernel(q_ref, k_ref, v_ref, gate_ref, cos_ref, sin_ref, dintra_ref, dq_ref, dk_ref, dchunk_ref,
                gng_ref, gnb_ref, o_ref, state_ref):
    H, dk, dv = RET_HEADS, RET_DK, RET_DV
    half = dk // 2

    @pl.when(pl.program_id(1) == 0)
    def _():
        state_ref[...] = jnp.zeros_like(state_ref)

    cos = cos_ref[...]
    sin = sin_ref[...]

    def rot(ref, h):
        x1 = ref[0, :, h * dk:h * dk + half]
        x2 = ref[0, :, h * dk + half:(h + 1) * dk]
        return jnp.concatenate([x1 * cos - x2 * sin, x1 * sin + x2 * cos], axis=-1)

    for h in range(H):
        q = rot(q_ref, h)
        k = rot(k_ref, h) * (dk ** -0.5)
        v = v_ref[0, :, h * dv:(h + 1) * dv]
        scores = _bdot_nt(q, k) * dintra_ref[h]
        st = state_ref[h]
        o = _bdot(scores, v) + _bdot(q, st) * dq_ref[h]
        state_ref[h] = st * dchunk_ref[h] + _bdot_tn(k * dk_ref[h], v)
        mu = jnp.mean(o, axis=-1, keepdims=True)
        var = jnp.mean(jnp.square(o - mu), axis=-1, keepdims=True)
        on = (o - mu) * lax.rsqrt(var + RET_GN_EPS) * gng_ref[h] + gnb_ref[h]
        g = gate_ref[0, :, h * dv:(h + 1) * dv]
        o_ref[0, :, h * dv:(h + 1) * dv] = (_silu(g) * on).astype(o_ref.dtype)


def _retention_scan(proj, gn_g, gn_b):
    B, S, _ = proj.shape
    H, dk, dv, C = RET_HEADS, RET_DK, RET_DV, RET_CHUNK
    half = dk // 2
    N = S // C
    inv_freq = RET_ROPE_BASE ** (-jnp.arange(half, dtype=F32) / half)
    ang = jnp.arange(S, dtype=F32)[:, None] * inv_freq[None, :]
    cos, sin = jnp.cos(ang), jnp.sin(ang)
    log_gamma = jnp.log1p(-jnp.exp2(-5.0 - jnp.arange(H, dtype=F32)))
    idx = jnp.arange(C, dtype=F32)
    diff = idx[:, None] - idx[None, :]
    causal = diff >= 0
    d_intra = jnp.where(causal[None], jnp.exp(jnp.where(causal, diff, 0.0)[None] * log_gamma[:, None, None]), 0.0)
    d_q = jnp.exp((idx[None, :] + 1.0) * log_gamma[:, None])[:, :, None]
    d_k = jnp.exp((C - 1.0 - idx)[None, :] * log_gamma[:, None])[:, :, None]
    d_chunk = jnp.exp(C * log_gamma)[:, None, None]
    qw, vw = H * dk, H * dv
    return pl.pallas_call(
        _ret_kernel,
        out_shape=jax.ShapeDtypeStruct((B, S, vw), BF16),
        grid=(B, N),
        in_specs=[
            pl.BlockSpec((1, C, qw), lambda b, n: (b, n, 0)),
            pl.BlockSpec((1, C, qw), lambda b, n: (b, n, 1)),
            pl.BlockSpec((1, C, vw), lambda b, n: (b, n, 1)),
            pl.BlockSpec((1, C, vw), lambda b, n: (b, n, 2)),
            pl.BlockSpec((C, half), lambda b, n: (n, 0)),
            pl.BlockSpec((C, half), lambda b, n: (n, 0)),
            pl.BlockSpec((H, C, C), lambda b, n: (0, 0, 0)),
            pl.BlockSpec((H, C, 1), lambda b, n: (0, 0, 0)),
            pl.BlockSpec((H, C, 1), lambda b, n: (0, 0, 0)),
            pl.BlockSpec((H, 1, 1), lambda b, n: (0, 0, 0)),
            pl.BlockSpec((H, 1, dv), lambda b, n: (0, 0, 0)),
            pl.BlockSpec((H, 1, dv), lambda b, n: (0, 0, 0)),
        ],
        out_specs=pl.BlockSpec((1, C, vw), lambda b, n: (b, n, 0)),
        scratch_shapes=[pltpu.VMEM((H, dk, dv), F32)],
        compiler_params=_cparams(("parallel", "arbitrary")),
        name="retention_scan",
    )(proj, proj, proj, proj, cos, sin, d_intra, d_q, d_k, d_chunk,
      gn_g.reshape(H, 1, dv), gn_b.reshape(H, 1, dv))


def _shift_rows(cur, prev8, s):
    rows = lax.broadcasted_iota(jnp.int32, cur.shape, 0)
    rolled = pltpu.roll(cur, s, axis=0)
    head = pltpu.roll(prev8, s, axis=0)
    head = jnp.concatenate([head, jnp.zeros((cur.shape[0] - SUBLANES, cur.shape[1]), cur.dtype)], axis=0)
    return jnp.where(rows < s, head, rolled)


def _cumsum_rows(x):
    rows = lax.broadcasted_iota(jnp.int32, x.shape, 0)
    s = 1
    while s < x.shape[0]:
        x = x + jnp.where(rows >= s, pltpu.roll(x, s, axis=0), 0.0)
        s *= 2
    return x


def _gdn_kernel(qkv_ref, gate_ref, ab_ref, cw_ref, alog_ref, dtb_ref, ng_ref, o_ref, state_ref, prev_ref):
    C, dk, H = GDN_CHUNK, GDN_DK, GDN_HEADS

    @pl.when(pl.program_id(1) == 0)
    def _():
        state_ref[...] = jnp.zeros_like(state_ref)
        prev_ref[...] = jnp.zeros_like(prev_ref)

    def conv_silu(col):
        cur = qkv_ref[0, :, col * dk:(col + 1) * dk]
        prev8 = prev_ref[:, col * dk:(col + 1) * dk]
        cw = cw_ref[:, col * dk:(col + 1) * dk]
        acc = cur * cw[GDN_CONV - 1:GDN_CONV]
        for s in range(1, GDN_CONV):
            acc = acc + _shift_rows(cur, prev8, s) * cw[GDN_CONV - 1 - s:GDN_CONV - s]
        prev_ref[:, col * dk:(col + 1) * dk] = cur[C - SUBLANES:]
        return _silu(acc)

    ab = ab_ref[0]
    g_all = -jnp.exp(alog_ref[...]) * _softplus(ab + dtb_ref[...])
    beta_all = _sigmoid(ab)
    ri = lax.broadcasted_iota(jnp.int32, (C, C), 0)
    ci = lax.broadcasted_iota(jnp.int32, (C, C), 1)
    incl = ri >= ci

    bdot_nt = lambda x, y: jnp.einsum('hid,hjd->hij', x.astype(BF16), y.astype(BF16), preferred_element_type=F32)
    bdot = lambda x, y: jnp.einsum('hij,hjk->hik', x, y, preferred_element_type=F32)
    stack = lambda xs: jnp.stack(xs, axis=0)

    def hdot(x, y):
        xh, yh = x.astype(BF16), y.astype(BF16)
        xl, yl = (x - xh.astype(F32)).astype(BF16), (y - yh.astype(F32)).astype(BF16)
        return bdot(xh, yh) + (bdot(xh, yl) + bdot(xl, yh))

    qs, ks, vs, betas, cums = [], [], [], [], []
    for h in range(H):
        q = conv_silu(h)
        k = conv_silu(H + h)
        qs.append(q * lax.rsqrt(jnp.sum(q * q, axis=-1, keepdims=True) + 1e-6) * (dk ** -0.5))
        ks.append(k * lax.rsqrt(jnp.sum(k * k, axis=-1, keepdims=True) + 1e-6))
        vs.append(conv_silu(2 * H + h))
        betas.append(beta_all[:, H + h:H + h + 1])
        cums.append(_cumsum_rows(jnp.broadcast_to(g_all[:, h:h + 1], (C, LANES))))
    q, k, v, beta, cum = stack(qs), stack(ks), stack(vs), stack(betas), stack(cums)
    cum_c = cum[:, :, :1]
    cum_last = cum[:, C - 1:C, :1]
    cum_r = stack([cums[h].T[:C, :] for h in range(H)])
    decay = jnp.where(incl, jnp.exp(jnp.where(incl, cum[:, :, :C] - cum_r, 0.0)), 0.0)

    L = jnp.where(ri > ci, bdot_nt(k, k) * decay, 0.0) * beta
    rhs = jnp.concatenate([k * (beta * jnp.exp(cum_c)), v * beta], axis=-1)
    eye = (ri == ci).astype(F32)
    p = -L
    inv = eye + p
    s = 2
    while s < C:
        p = hdot(p, p)
        inv = inv + hdot(inv, p)
        s *= 2
    sol = hdot(inv, rhs)
    a_qk = bdot_nt(q, k) * decay
    q_dec = q * jnp.exp(cum_c)
    k_dec = k * jnp.exp(cum_last - cum_c)
    e_last = jnp.exp(cum_last)

    for h in range(H):
        st = state_ref[h]
        u = sol[h, :, dk:] - _bdot(sol[h, :, :dk], st)
        o = _bdot(q_dec[h], st) + _bdot(a_qk[h], u)
        state_ref[h] = st * e_last[h] + _bdot_tn(k_dec[h], u)
        ms = jnp.mean(o * o, axis=-1, keepdims=True)
        o = o * lax.rsqrt(ms + NORM_EPS) * ng_ref[...]
        o_ref[0, :, h * dk:(h + 1) * dk] = (o * _silu(gate_ref[0, :, h * dk:(h + 1) * dk])).astype(o_ref.dtype)


def _gdn_scan(proj, proj_ab, conv_w, a_log, dt_bias, norm_g):
    B, S, _ = proj.shape
    H, dk, C = GDN_HEADS, GDN_DK, GDN_CHUNK
    N = S // C
    pad = lambda t: jnp.pad(t.astype(F32), (0, LANES - H)).reshape(1, LANES)
    row = pl.BlockSpec((1, LANES), lambda b, n: (0, 0))
    return pl.pallas_call(
        _gdn_kernel,
        out_shape=jax.ShapeDtypeStruct((B, S, H * dk), BF16),
        grid=(B, N),
        in_specs=[pl.BlockSpec((1, C, 3 * H * dk), lambda b, n: (b, n, 0)),
                  pl.BlockSpec((1, C, H * dk), lambda b, n: (b, n, 3)),
                  pl.BlockSpec((1, C, LANES), lambda b, n: (b, n, 0)),
                  pl.BlockSpec((GDN_CONV, 3 * H * dk), lambda b, n: (0, 0)),
                  row, row, row],
        out_specs=pl.BlockSpec((1, C, H * dk), lambda b, n: (b, n, 0)),
        scratch_shapes=[pltpu.VMEM((H, dk, dk), F32), pltpu.VMEM((SUBLANES, 3 * H * dk), F32)],
        compiler_params=_cparams(("parallel", "arbitrary")),
        name="gdn_scan",
    )(proj, proj, proj_ab, conv_w, pad(a_log), pad(dt_bias), norm_g.reshape(1, dk))


def _rwkv_proj_kernel(x_ref, xp_ref, g_ref, sc_ref, sh_ref, mu_ref, wr_ref, wk_ref, wv_ref, w1_ref, w2_ref,
                      a1_ref, a2_ref, g1_ref, g2_ref, w0_ref, a0_ref, kk_ref, ka_ref,
                      r_o, dec_o, k_o, v_o, kk_o, a_o, g_o, *, tiles_per_seq):
    h = _norm_mod(x_ref[...], g_ref[...], sc_ref[0], sh_ref[0])
    hp8 = _norm_mod(xp_ref[...], g_ref[...], sc_ref[0], sh_ref[0])
    seq_start = pl.program_id(0) % tiles_per_seq == 0
    first = jnp.where(seq_start, 0.0, hp8[SUBLANES - 1:SUBLANES, :])
    rows = lax.broadcasted_iota(jnp.int32, h.shape, 0)
    xx = jnp.where(rows == 0, first, pltpu.roll(h, 1, axis=0)) - h
    mix = lambda j: h + xx * mu_ref[j:j + 1, :]
    r = _bdot(mix(0), wr_ref[...])
    lw = w0_ref[...] + _bdot(jnp.tanh(_bdot(mix(1), w1_ref[...])), w2_ref[...])
    k = _bdot(mix(2), wk_ref[...])
    v = _bdot(mix(3), wv_ref[...])
    a = _sigmoid(a0_ref[...] + _bdot(_bdot(mix(4), a1_ref[...]), a2_ref[...]))
    g = _bdot(_sigmoid(_bdot(mix(5), g1_ref[...])), g2_ref[...])
    w = -_softplus(-lw) - 0.5
    r_o[...] = r
    dec_o[...] = jnp.exp(-jnp.exp(w))
    k_o[...] = k * (1.0 + (a - 1.0) * ka_ref[...])
    v_o[...] = v
    kk_o[...] = k * kk_ref[...]
    a_o[...] = a
    g_o[...] = g


def _pad_cols(w, n):
    return jnp.pad(w, ((0, 0), (0, n - w.shape[1])))


def _pad_rows(w, n):
    return jnp.pad(w, ((0, n - w.shape[0]), (0, 0)))


def _rwkv_proj(x2d, g, sc, sh, mu, w_r, w_k, w_v, w1, w2, a1, a2, g1, g2, w0, a0, k_k, k_a, *, rows_per_batch, tm=256):
    m, d = x2d.shape
    nb = sc.shape[0]
    tpb = rows_per_batch // tm
    lora_w = LANES * pl.cdiv(w1.shape[1], LANES)
    lora_g = LANES * pl.cdiv(g1.shape[1], LANES)
    bf = lambda t: t.astype(BF16)
    full = lambda a: pl.BlockSpec(a.shape, lambda i: (0,) * a.ndim)
    row = lambda t: t.reshape(1, d)
    ws = [bf(w_r), bf(w_k), bf(w_v), bf(_pad_cols(w1, lora_w)), bf(_pad_rows(w2, lora_w)),
          bf(_pad_cols(a1, lora_w)), bf(_pad_rows(a2, lora_w)), bf(_pad_cols(g1, lora_g)), bf(_pad_rows(g2, lora_g)),
          row(w0), row(a0), row(k_k), row(k_a)]
    tile = pl.BlockSpec((tm, d), lambda i: (i, 0))
    return pl.pallas_call(
        functools.partial(_rwkv_proj_kernel, tiles_per_seq=tpb),
        out_shape=[jax.ShapeDtypeStruct((m, d), F32)] * 7,
        grid=(m // tm,),
        in_specs=[
            tile,
            pl.BlockSpec((SUBLANES, d), lambda i: (jnp.maximum(i * (tm // SUBLANES) - 1, 0), 0)),
            pl.BlockSpec((1, d), lambda i: (0, 0)),
            pl.BlockSpec((1, 1, d), lambda i: (i // tpb, 0, 0)),
            pl.BlockSpec((1, 1, d), lambda i: (i // tpb, 0, 0)),
            full(mu),
        ] + [full(w) for w in ws],
        out_specs=[tile] * 7,
        compiler_params=_cparams(("parallel",)),
        name="rwkv_proj",
    )(x2d, x2d, g.reshape(1, d), sc.reshape(nb, 1, d), sh.reshape(nb, 1, d), mu, *ws)


RWKV_VUNROLL = 64


def _rwkv_scan_kernel(r_ref, w_ref, k_ref, v_ref, kk_ref, a_ref, rk_ref, lng_ref, lnb_ref, y_ref, state_ref, yrow_ref):
    n = RWKV_HEAD

    @pl.when(pl.program_id(0) == 0)
    def _():
        state_ref[...] = jnp.zeros_like(state_ref)

    def step(t, carry):
        r, w, k, kkr, a = r_ref[t], w_ref[t], k_ref[t], kk_ref[t], a_ref[t]
        kk = kkr * lax.rsqrt(jnp.sum(kkr * kkr, axis=0, keepdims=True) + 1e-6)
        nkk = -kk
        kka = kk * a

        def vloop(vb, c):
            for j in range(RWKV_VUNROLL):
                vi = vb * RWKV_VUNROLL + j
                sv = state_ref[vi]
                sa = jnp.sum(sv * nkk, axis=0, keepdims=True)
                vrow = v_ref[t, pl.ds(vi, 1), :]
                sn = sv * w + sa * kka + vrow * k
                state_ref[vi] = sn
                yrow_ref[pl.ds(vi, 1), :] = jnp.sum(sn * r, axis=0, keepdims=True)
            return c

        lax.fori_loop(0, n // RWKV_VUNROLL, vloop, 0)
        y = yrow_ref[...]
        mu = jnp.mean(y, axis=0, keepdims=True)
        var = jnp.mean(jnp.square(y - mu), axis=0, keepdims=True)
        yn = (y - mu) * lax.rsqrt(var + RWKV_GN_EPS) * lng_ref[...] + lnb_ref[...]
        bonus = jnp.sum(r * k * rk_ref[...], axis=0, keepdims=True)
        y_ref[t] = yn + bonus * v_ref[t]
        return carry

    lax.fori_loop(0, r_ref.shape[0], step, 0)


def _rwkv_scan(r, dec, k, v, kk, a, r_k, ln_g, ln_b, *, tc=32):
    B, S, D = r.shape
    H, n = RWKV_HEADS, RWKV_HEAD
    lanes = B * H
    to_scan = lambda t: jnp.transpose(t.reshape(B, S, H, n), (1, 3, 0, 2)).reshape(S, n, lanes)
    per_head = lambda p: jnp.tile(p.T, (1, B))
    blk = pl.BlockSpec((tc, n, lanes), lambda i: (i, 0, 0))
    cst = pl.BlockSpec((n, lanes), lambda i: (0, 0))
    y = pl.pallas_call(
        _rwkv_scan_kernel,
        out_shape=jax.ShapeDtypeStruct((S, n, lanes), F32),
        grid=(S // tc,),
        in_specs=[blk] * 6 + [cst] * 3,
        out_specs=blk,
        scratch_shapes=[pltpu.VMEM((n, n, lanes), F32), pltpu.VMEM((n, lanes), F32)],
        compiler_params=_cparams(("arbitrary",)),
        name="rwkv_scan",
    )(to_scan(r), to_scan(dec), to_scan(k), to_scan(v), to_scan(kk), to_scan(a),
      per_head(r_k), per_head(ln_g), per_head(ln_b))
    return jnp.transpose(y.reshape(S, n, B, H), (2, 0, 3, 1)).reshape(B, S, D)


def _rwkv_mixer(x, g, sc, sh, gt, mu, w_r, w_k, w_v, w_o, w0, w1, w2, a0, a1, a2, g1, g2, k_k, k_a, r_k, ln_g, ln_b):
    B, S, D = x.shape
    x2d = x.reshape(B * S, D)
    tm = min(256, S)
    r, dec, k, v, kk, a, gg = _rwkv_proj(x2d, g, sc, sh, mu, w_r, w_k, w_v, w1, w2, a1, a2, g1, g2, w0, a0, k_k, k_a,
                                         rows_per_batch=S, tm=tm)
    sh3 = lambda t: t.reshape(B, S, D)
    y = _rwkv_scan(sh3(r), sh3(dec), sh3(k), sh3(v), sh3(kk), sh3(a), r_k, ln_g, ln_b, tc=min(32, S))
    out = _mm_res(y.reshape(B * S, D), w_o.astype(BF16), x2d, gt, rows_per_batch=S, mul=gg, tm=min(512, S))
    return out.reshape(B, S, D)


def _rwkv_mixer_test(x, g, sc, sh, p):
    gt = jnp.ones_like(sc)
    return _rwkv_mixer(x, g, sc, sh, gt, p['mu'], p['w_r'], p['w_k'], p['w_v'], p['w_o'], p['w0'], p['w1'], p['w2'],
                       p['a0'], p['a1'], p['a2'], p['g1'], p['g2'], p['k_k'], p['k_a'], p['r_k'], p['ln_g'],
                       p['ln_b']) - x


def _topk_rows(s, k, rows=None):
    if rows is None:
        rows = lax.broadcasted_iota(jnp.int32, s.shape, 0)
    n = jnp.iinfo(jnp.int32).max
    vals, ids = [], []
    for _ in range(k):
        m = jnp.max(s, axis=0, keepdims=True)
        idx = jnp.min(jnp.where(s == m, rows, n), axis=0, keepdims=True)
        vals.append(m)
        ids.append(idx)
        s = jnp.where(rows == idx, -jnp.inf, s)
    return jnp.concatenate(vals, axis=0), jnp.concatenate(ids, axis=0)


def _take_rows(table, pos):
    out = jnp.zeros(pos.shape, table.dtype)
    for m in range(table.shape[0]):
        out = jnp.where(pos == m, table[m:m + 1, :], out)
    return out


_PEER_CAND = [(i, PEER_TOPK // (i + 1)) for i in range(PEER_TOPK)]
_PEER_NCAND = sum(n for _, n in _PEER_CAND)
_PEER_NCAND_PAD = SUBLANES * pl.cdiv(_PEER_NCAND, SUBLANES)


def _peer_cand_codes():
    codes = [i * PEER_TOPK + j for i, n in _PEER_CAND for j in range(n)]
    codes += [PEER_TOPK * PEER_TOPK + p for p in range(_PEER_NCAND_PAD - _PEER_NCAND)]
    return jnp.broadcast_to(jnp.asarray(codes, jnp.int32)[:, None], (_PEER_NCAND_PAD, LANES))


def _peer_route_kernel(q_ref, keys_ref, codes_ref, idx_o, gate_o, idx_tok_o, gate_tok_o):
    K, half = PEER_TOPK, PEER_DQ // 2

    tm = q_ref.shape[0]
    G = 2
    codes = jnp.concatenate([codes_ref[...]] * (G * tm // LANES), axis=1)
    pad = jnp.full((_PEER_NCAND_PAD - _PEER_NCAND, tm), -jnp.inf, F32)

    def group(hg, carry):
        ss = []
        for dh in range(G):
            for p in range(2):
                c = pl.multiple_of((2 * (G * hg + dh) + p) * half, half)
                ss.append(_bdot_nt(keys_ref[G * hg + dh, p], q_ref[:, pl.ds(c, half)]))
        vals, ids = _topk_rows(jnp.concatenate(ss, axis=1), K)
        part = lambda t, j: t[:, j * tm:(j + 1) * tm]
        cands = []
        for dh in range(G):
            va, vb = part(vals, 2 * dh), part(vals, 2 * dh + 1)
            cands.append(jnp.concatenate([va[i:i + 1, :] + vb[:n, :] for i, n in _PEER_CAND] + [pad], axis=0))
        best, pos = _topk_rows(jnp.concatenate(cands, axis=1), K, codes)
        for dh in range(G):
            ia, ib = part(ids, 2 * dh), part(ids, 2 * dh + 1)
            ps, bs = part(pos, dh), part(best, dh)
            expert = _take_rows(ia, ps // K) * PEER_KEYS + _take_rows(ib, ps % K)
            e = jnp.exp(bs - bs[0:1, :])
            r0 = pl.multiple_of((G * hg + dh) * K, K)
            idx_o[pl.ds(r0, K), :] = expert
            gate_o[pl.ds(r0, K), :] = e / jnp.sum(e, axis=0, keepdims=True)
        return carry

    lax.fori_loop(0, PEER_HEADS // G, group, 0)
    idx_tok_o[...] = idx_o[...].T
    gate_tok_o[...] = gate_o[...].T


def _peer_route(q, sub_keys, *, tm=256):
    t, n = q.shape
    blk = pl.BlockSpec((PEER_SEL, tm), lambda i: (0, i))
    tok = pl.BlockSpec((tm, PEER_SEL), lambda i: (i, 0))
    return pl.pallas_call(
        _peer_route_kernel,
        out_shape=[jax.ShapeDtypeStruct((PEER_SEL, t), jnp.int32), jax.ShapeDtypeStruct((PEER_SEL, t), F32),
                   jax.ShapeDtypeStruct((t, PEER_SEL), jnp.int32), jax.ShapeDtypeStruct((t, PEER_SEL), F32)],
        grid=(t // tm,),
        in_specs=[pl.BlockSpec((tm, n), lambda i: (i, 0)),
                  pl.BlockSpec(sub_keys.shape, lambda i: (0, 0, 0, 0)),
                  pl.BlockSpec((_PEER_NCAND_PAD, LANES), lambda i: (0, 0))],
        out_specs=[blk, blk, tok, tok],
        compiler_params=_cparams(("parallel",)),
        name="peer_route",
    )(q, sub_keys.astype(BF16), _peer_cand_codes())


def _pack_kernel(u_ref, v_ref, uv_o, u2_o, v2_o):
    half = u_ref.shape[1] // 2
    high = jnp.int32(-65536)
    bits = lambda x: lax.bitcast_convert_type(x.astype(BF16).astype(F32), jnp.int32)
    low = lambda b: lax.shift_right_logical(b, 16)
    ub, vb = bits(u_ref[...]), bits(v_ref[...])
    uv_o[...] = (vb & high) | low(ub)
    u2_o[...] = (ub[:, half:] & high) | low(ub[:, :half])
    v2_o[...] = (vb[:, half:] & high) | low(vb[:, :half])


def _pack_tables(u, v, *, tm=512):
    e, d = u.shape
    tile = lambda w: pl.BlockSpec((tm, w), lambda i: (i, 0))
    uv, u2, v2 = pl.pallas_call(
        _pack_kernel,
        out_shape=[jax.ShapeDtypeStruct((e, d), jnp.int32), jax.ShapeDtypeStruct((e, d // 2), jnp.int32),
                   jax.ShapeDtypeStruct((e, d // 2), jnp.int32)],
        grid=(e // tm,),
        in_specs=[tile(d), tile(d)],
        out_specs=[tile(d), tile(d // 2), tile(d // 2)],
        compiler_params=_cparams(("parallel",)),
        name="peer_pack",
    )(u, v)
    return uv.reshape(e, d // LANES, LANES), u2, v2


PEER_NBUF = 8


def _gelu(x):
    return 0.5 * x * (1.0 + lax.erf(x * (2.0 ** -0.5)))


def _peer_eval(packed, x, gate):
    u = lax.bitcast_convert_type(packed << 16, F32)
    v = lax.bitcast_convert_type(packed & jnp.int32(-65536), F32)
    act = jnp.sum(jnp.sum(u * x[None], axis=1), axis=1, keepdims=True)
    wgt = gate * _gelu(act)
    return jnp.sum(v * wgt[:, :, None], axis=0)


def _gate_column(gate_ref, t):
    lane = lax.broadcasted_iota(jnp.int32, gate_ref.shape, 1)
    return jnp.sum(jnp.where(lane == t, gate_ref[...], 0.0), axis=1, keepdims=True)


def _peer_expert_kernel(idx_hbm, gate_ref, h_ref, xres_ref, gt_ref, uv_hbm, *rest, tok0):
    o_ref, idx_smem, buf, sem_idx, sem = rest[-5:]
    tb = h_ref.shape[0]
    nsel = PEER_SEL
    base = (pl.program_id(0) * tb + tok0) * nsel
    cp = pltpu.make_async_copy(idx_hbm.at[pl.ds(base, tb * nsel)], idx_smem, sem_idx)
    cp.start()
    cp.wait()

    def issue(t, slot):
        for k in range(nsel):
            e = idx_smem[t * nsel + k]
            pltpu.make_async_copy(uv_hbm.at[e], buf.at[slot, k], sem.at[slot]).start()

    def wait(slot):
        pltpu.make_async_copy(uv_hbm.at[pl.ds(0, nsel)], buf.at[slot], sem.at[slot]).wait()

    for t0 in range(PEER_NBUF - 1):
        issue(t0, t0)

    def body(t, carry):
        slot = t % PEER_NBUF
        nxt = t + PEER_NBUF - 1

        @pl.when(nxt < tb)
        def _():
            issue(nxt, nxt % PEER_NBUF)

        wait(slot)
        out = _peer_eval(buf[slot], h_ref[t], _gate_column(gate_ref, t))
        o_ref[t] = xres_ref[t] + gt_ref[0] * out
        return carry

    lax.fori_loop(0, tb, body, 0)


def _peer_expert(idx_flat, gate_t, h3, xres3, gt3, uv, *, rows_per_batch, tok0, ntok, hx0, tb=128, after=None):
    _, c, _ = h3.shape
    tpb = rows_per_batch // tb
    b0 = tok0 // tb
    hb0 = (tok0 - hx0) // tb
    tok = pl.BlockSpec((tb, c, LANES), lambda i: (i + hb0, 0, 0))
    return pl.pallas_call(
        functools.partial(_peer_expert_kernel, tok0=tok0),
        out_shape=jax.ShapeDtypeStruct((ntok, c, LANES), F32),
        grid=(ntok // tb,),
        in_specs=[
            pl.BlockSpec(memory_space=pl.ANY),
            pl.BlockSpec((PEER_SEL, tb), lambda i: (0, i + b0)),
            tok, tok,
            pl.BlockSpec((1, c, LANES), lambda i: ((i + b0) // tpb, 0, 0)),
            pl.BlockSpec(memory_space=pl.ANY),
        ] + ([] if after is None else [pl.BlockSpec(memory_space=pl.ANY)]),
        out_specs=pl.BlockSpec((tb, c, LANES), lambda i: (i, 0, 0)),
        scratch_shapes=[
            pltpu.SMEM((tb * PEER_SEL,), jnp.int32),
            pltpu.VMEM((PEER_NBUF, PEER_SEL, c, LANES), jnp.int32),
            pltpu.SemaphoreType.DMA,
            pltpu.SemaphoreType.DMA((PEER_NBUF,)),
        ],
        compiler_params=_cparams(("arbitrary",)),
        name="peer_expert",
    )(idx_flat, gate_t, h3, xres3, gt3, uv, *(() if after is None else (after,)))


SC_CORES = 2
SC_SUBCORES = 16
SC_WORKERS = SC_CORES * SC_SUBCORES
SC_LANES = 16
SC_TOKENS = 8
SC_GATHER_ROWS = 32
SC_CHUNKS = 8
_SC_PARAMS = pltpu.CompilerParams(needs_layout_passes=False)


def _sc_unpack(w):
    return (lax.bitcast_convert_type(w << 16, F32), lax.bitcast_convert_type(w & jnp.int32(-65536), F32))


def _sc_mesh():
    return plsc.VectorSubcoreMesh(core_axis_name="core", subcore_axis_name="subcore")


def _sc_act(u2, idx_flat, h_flat, *, tok0, ntok, d):
    nsel, L, G, CG, TBK = PEER_SEL, SC_LANES, SC_GATHER_ROWS, SC_CHUNKS, SC_TOKENS
    half = d // 2
    tpw = ntok // SC_WORKERS
    ng = nsel // G
    ngt = TBK * ng

    @functools.partial(
        pl.kernel, out_type=jax.ShapeDtypeStruct((ntok * nsel,), F32), mesh=_sc_mesh(),
        scratch_types=[pltpu.VMEM((TBK * nsel,), jnp.int32), pltpu.VMEM((TBK * d,), F32),
                       pltpu.VMEM((2, G, half), jnp.int32), pltpu.VMEM((TBK * nsel,), F32),
                       pltpu.VMEM((G * L,), F32), pltpu.SemaphoreType.DMA((2,))],
        compiler_params=_SC_PARAMS, name="peer_sc_act")
    def k(u_hbm, i_hbm, x_hbm, o_hbm, idx_v, x_v, buf, act_v, acc_v, sem):
        wid = lax.axis_index("core") * SC_SUBCORES + lax.axis_index("subcore")
        lanes = lax.iota(jnp.int32, L)

        def gather(gi, b):
            return pltpu.make_async_copy(u_hbm.at[idx_v.at[pl.ds(gi * G, G)]], buf.at[b], sem.at[b])

        @pl.loop(0, tpw // TBK)
        def _(bi):
            tl = wid * tpw + bi * TBK
            pltpu.sync_copy(i_hbm.at[pl.ds((tok0 + tl) * nsel, TBK * nsel)], idx_v)
            pltpu.sync_copy(x_hbm.at[pl.ds((tok0 + tl) * d, TBK * d)], x_v)
            gather(0, 0).start()

            @pl.loop(0, ngt, step=2)
            def _(g0):
                for b in range(2):
                    gi = g0 + b

                    @pl.when(gi + 1 < ngt)
                    def _():
                        gather(gi + 1, 1 - b).start()

                    gather(gi, b).wait()
                    xbase = (gi // ng) * d
                    for jg in range(half // (CG * L)):
                        xl = [x_v[pl.ds(pl.multiple_of(xbase + (jg * CG + c) * L, L), L)] for c in range(CG)]
                        xh = [x_v[pl.ds(pl.multiple_of(xbase + half + (jg * CG + c) * L, L), L)] for c in range(CG)]

                        def row(r, carry):
                            ps = []
                            for c in range(CG):
                                lo, hi = _sc_unpack(buf[b, r, pl.ds((jg * CG + c) * L, L)])
                                ps.append(lo * xl[c] + hi * xh[c])
                            while len(ps) > 1:
                                ps = [ps[i] + ps[i + 1] for i in range(0, len(ps), 2)]
                            off = pl.multiple_of(r * L, L)
                            if jg == 0:
                                acc_v[pl.ds(off, L)] = ps[0]
                            else:
                                acc_v[pl.ds(off, L)] = acc_v[pl.ds(off, L)] + ps[0]
                            return carry

                        plsc.parallel_loop(0, G, carry=jnp.int32(0))(row)
                    for part in range(G // L):
                        vec = jnp.zeros((L,), F32)
                        for r in range(L):
                            vec = jnp.where(lanes == r, jnp.sum(acc_v[pl.ds((part * L + r) * L, L)]), vec)
                        act_v[pl.ds(pl.multiple_of(gi * G + part * L, L), L)] = vec

            pltpu.sync_copy(act_v, o_hbm.at[pl.ds(tl * nsel, TBK * nsel)])

    return k(u2, idx_flat, h_flat)


def _sc_out(v2, idx_flat, w_flat, xres_flat, gt_flat, *, tok0, ntok, d, rows_per_batch):
    nsel, L, G, CG, TBK = PEER_SEL, SC_LANES, SC_GATHER_ROWS, SC_CHUNKS, SC_TOKENS
    half = d // 2
    tpw = ntok // SC_WORKERS
    ng = nsel // G
    ngt = TBK * ng

    @functools.partial(
        pl.kernel, out_type=jax.ShapeDtypeStruct((ntok * d,), F32), mesh=_sc_mesh(),
        scratch_types=[pltpu.VMEM((TBK * nsel,), jnp.int32), pltpu.VMEM((TBK * nsel,), F32),
                       pltpu.VMEM((2, G, half), jnp.int32), pltpu.VMEM((TBK * d,), F32),
                       pltpu.VMEM((TBK * d,), F32), pltpu.VMEM((d,), F32), pltpu.SemaphoreType.DMA((2,))],
        compiler_params=_SC_PARAMS, name="peer_sc_out")
    def k(v_hbm, i_hbm, w_hbm, xr_hbm, gt_hbm, o_hbm, idx_v, w_v, buf, out_v, xr_v, gt_v, sem):
        wid = lax.axis_index("core") * SC_SUBCORES + lax.axis_index("subcore")

        def gather(gi, b):
            return pltpu.make_async_copy(v_hbm.at[idx_v.at[pl.ds(gi * G, G)]], buf.at[b], sem.at[b])

        @pl.loop(0, tpw // TBK)
        def _(bi):
            tl = wid * tpw + bi * TBK
            pltpu.sync_copy(i_hbm.at[pl.ds((tok0 + tl) * nsel, TBK * nsel)], idx_v)
            pltpu.sync_copy(w_hbm.at[pl.ds(tl * nsel, TBK * nsel)], w_v)
            gather(0, 0).start()
            pltpu.sync_copy(xr_hbm.at[pl.ds((tok0 + tl) * d, TBK * d)], xr_v)
            pltpu.sync_copy(gt_hbm.at[pl.ds(((tok0 + tl) // rows_per_batch) * d, d)], gt_v)

            @pl.loop(0, TBK * d // L)
            def _(i):
                out_v[pl.ds(pl.multiple_of(i * L, L), L)] = jnp.zeros((L,), F32)

            @pl.loop(0, ngt, step=2)
            def _(g0):
                for b in range(2):
                    gi = g0 + b

                    @pl.when(gi + 1 < ngt)
                    def _():
                        gather(gi + 1, 1 - b).start()

                    gather(gi, b).wait()
                    obase = (gi // ng) * d
                    for jg in range(half // (CG * L)):
                        lo0 = obase + jg * CG * L
                        hi0 = lo0 + half
                        accs = tuple(out_v[pl.ds(pl.multiple_of(lo0 + c * L, L), L)] for c in range(CG)) + \
                            tuple(out_v[pl.ds(pl.multiple_of(hi0 + c * L, L), L)] for c in range(CG))

                        def row(r, accs):
                            wk = plsc.load_gather(w_v, [jnp.full((L,), gi * G + r, jnp.int32)])
                            new = list(accs)
                            for c in range(CG):
                                lo, hi = _sc_unpack(buf[b, r, pl.ds((jg * CG + c) * L, L)])
                                new[c] = new[c] + wk * lo
                                new[CG + c] = new[CG + c] + wk * hi
                            return tuple(new)

                        accs = lax.fori_loop(0, G, row, accs)
                        for c in range(CG):
                            out_v[pl.ds(pl.multiple_of(lo0 + c * L, L), L)] = accs[c]
                            out_v[pl.ds(pl.multiple_of(hi0 + c * L, L), L)] = accs[CG + c]

            @pl.loop(0, TBK * d // L)
            def _(i):
                off = pl.multiple_of(i * L, L)
                goff = pl.multiple_of((i % (d // L)) * L, L)
                out_v[pl.ds(off, L)] = xr_v[pl.ds(off, L)] + gt_v[pl.ds(goff, L)] * out_v[pl.ds(off, L)]

            pltpu.sync_copy(out_v, o_hbm.at[pl.ds(tl * d, TBK * d)])

    return k(v2, idx_flat, w_flat, xres_flat, gt_flat)


def _peer_weight_kernel(act_ref, gate_ref, *rest):
    rest[-1][...] = gate_ref[...] * _gelu(act_ref[...])


def _peer_weight(act, gate_tok, *, tok0, tm=1024, after=None):
    n, k = act.shape
    tm = math.gcd(math.gcd(n, tok0), tm)
    assert tm % SUBLANES == 0
    b0 = tok0 // tm
    return pl.pallas_call(
        _peer_weight_kernel,
        out_shape=jax.ShapeDtypeStruct((n, k), F32),
        grid=(n // tm,),
        in_specs=[pl.BlockSpec((tm, k), lambda i: (i, 0)), pl.BlockSpec((tm, k), lambda i: (i + b0, 0))]
        + ([] if after is None else [pl.BlockSpec(memory_space=pl.ANY)]),
        out_specs=pl.BlockSpec((tm, k), lambda i: (i, 0)),
        compiler_params=_cparams(("parallel",)),
        name="peer_weight",
    )(act, gate_tok, *(() if after is None else (after,)))


PEER_SC_SHARE = (19, 32)
PEER_SC_CALLS = 2
PEER_TC_SPLIT = (1, 1, 2)


def _peer_experts(idx_flat, gate_t, gate_tok, h, xres, gt, uv, u2, v2, *, rows_per_batch):
    t, d = h.shape
    c = d // LANES
    tb = min(128, rows_per_batch)
    unit = SC_WORKERS * SC_TOKENS * PEER_SC_CALLS
    n_sc = t * PEER_SC_SHARE[0] // PEER_SC_SHARE[1] // unit * unit
    ch = n_sc // PEER_SC_CALLS
    h3, xres3 = h[n_sc:].reshape(t - n_sc, c, LANES), xres[n_sc:].reshape(t - n_sc, c, LANES)
    gt3 = gt.reshape(-1, c, LANES)
    blocks = (t - n_sc) // tb
    split = PEER_TC_SPLIT if n_sc and blocks >= sum(PEER_TC_SPLIT) else (1,) if blocks else ()
    cum = [sum(split[:j]) for j in range(len(split) + 1)]
    bounds = [n_sc + (blocks * cj // max(cum[-1], 1)) * tb for cj in cum]

    def gather_call(j, after):
        lo, hi = bounds[j], bounds[j + 1]
        return _peer_expert(idx_flat, gate_t, h3, xres3, gt3, uv, rows_per_batch=rows_per_batch, tok0=lo, ntok=hi - lo,
                            hx0=n_sc, tb=tb, after=after)

    sc_outs, tc_outs, last = [], [], None
    if n_sc:
        h_flat, xres_flat, gt_flat = h.reshape(-1), xres.reshape(-1), gt.reshape(-1)
        acts = [_sc_act(u2, idx_flat, h_flat, tok0=i * ch, ntok=ch, d=d) for i in range(PEER_SC_CALLS)]
        for i, a in enumerate(acts):
            if len(tc_outs) < len(split):
                tc_outs.append(gather_call(len(tc_outs), last))
                last = tc_outs[-1]
            last = _peer_weight(a.reshape(ch, PEER_SEL), gate_tok, tok0=i * ch, after=last)
            sc_outs.append(_sc_out(v2, idx_flat, last.reshape(-1), xres_flat, gt_flat, tok0=i * ch, ntok=ch, d=d,
                                   rows_per_batch=rows_per_batch).reshape(ch, d))
    while len(tc_outs) < len(split):
        tc_outs.append(gather_call(len(tc_outs), last))
        last = tc_outs[-1]
    return jnp.concatenate(sc_outs + [o.reshape(-1, d) for o in tc_outs], axis=0)


def _peer_ffn(x2d, g, sc, sh, gt, w_q, sub_keys, u, v, *, rows_per_batch):
    tm = min(512, rows_per_batch)
    q, h = _nm_matmul(x2d, g, sc, sh, w_q, rows_per_batch=rows_per_batch, tm=tm, emit_h=True)
    _, gate_t, idx_tok, gate_tok = _peer_route(q, sub_keys, tm=min(256, rows_per_batch))
    return _peer_experts(idx_tok.reshape(-1), gate_t, gate_tok, h, x2d, gt, *_pack_tables(u, v),
                         rows_per_batch=rows_per_batch)


def _peer_test(x, g, sc, sh, w_q, sub_keys, u, v):
    B, S, D = x.shape
    x2d = x.reshape(B * S, D)
    return (_peer_ffn(x2d, g, sc, sh, jnp.ones_like(sc), w_q.astype(BF16), sub_keys, u, v,
                      rows_per_batch=S) - x2d).reshape(B, S, D)


def _final_norm_kernel(x_ref, g_ref, o_ref):
    x = x_ref[...]
    ms = jnp.mean(x * x, axis=-1, keepdims=True)
    o_ref[...] = x * lax.rsqrt(ms + NORM_EPS) * g_ref[...]


def _final_norm(x2d, g, *, tm=512):
    m, d = x2d.shape
    tile = pl.BlockSpec((tm, d), lambda i: (i, 0))
    return pl.pallas_call(
        _final_norm_kernel,
        out_shape=jax.ShapeDtypeStruct((m, d), F32),
        grid=(m // tm,),
        in_specs=[tile, pl.BlockSpec((1, d), lambda i: (0, 0))],
        out_specs=tile,
        compiler_params=_cparams(("parallel",)),
        name="final_norm",
    )(x2d, g.reshape(1, d))


def kernel(x, c, ada_w, ada_b, norm_mix_g, norm_ffn_g, final_norm_g, ret_w_in, ret_w_out, ret_gn_g, ret_gn_b, gdn_w_in, gdn_conv_w, gdn_a_log, gdn_dt_bias, gdn_norm_g, gdn_w_out, rwkv_mu, rwkv_w_r, rwkv_w_k, rwkv_w_v, rwkv_w_o, rwkv_w0, rwkv_w1, rwkv_w2, rwkv_a0, rwkv_a1, rwkv_a2, rwkv_g1, rwkv_g2, rwkv_k_k, rwkv_k_a, rwkv_r_k, rwkv_ln_g, rwkv_ln_b, peer_w_q, peer_sub_keys, peer_u, peer_v):
    B, S, D = x.shape
    T = B * S
    depth = ada_w.shape[0]
    bf = lambda t: t.astype(BF16)
    mod = _adaln(c, ada_w, ada_b)
    x2d = x.reshape(T, D)
    for layer in range(depth):
        sh_m, sc_m, gt_m, sh_f, sc_f, gt_f = [mod[layer, :, i * D:(i + 1) * D] for i in range(6)]
        g_mix = norm_mix_g[layer]
        kind, j = layer % 3, layer // 3
        if kind == 0:
            proj = _nm_matmul(x2d, g_mix, sc_m, sh_m, bf(ret_w_in[j]), rows_per_batch=S, tn=2048)
            o = _retention_scan(proj.reshape(B, S, -1), ret_gn_g[j], ret_gn_b[j])
            x2d = _mm_res(o.reshape(T, -1), bf(ret_w_out[j]), x2d, gt_m, rows_per_batch=S)
        elif kind == 1:
            w = gdn_w_in[j]
            wide = GDN_QKV + GDN_HEADS * GDN_DV
            proj = _nm_matmul(x2d, g_mix, sc_m, sh_m, bf(w[:, :wide]), rows_per_batch=S, tn=2048)
            proj_ab = _nm_matmul(x2d, g_mix, sc_m, sh_m, bf(_pad_cols(w[:, wide:], LANES)), rows_per_batch=S)
            o = _gdn_scan(proj.reshape(B, S, -1), proj_ab.reshape(B, S, -1), gdn_conv_w[j], gdn_a_log[j],
                          gdn_dt_bias[j], gdn_norm_g[j])
            x2d = _mm_res(o.reshape(T, -1), bf(gdn_w_out[j]), x2d, gt_m, rows_per_batch=S)
        else:
            x2d = _rwkv_mixer(x2d.reshape(B, S, D), g_mix, sc_m, sh_m, gt_m, rwkv_mu[j], rwkv_w_r[j], rwkv_w_k[j],
                              rwkv_w_v[j], rwkv_w_o[j], rwkv_w0[j], rwkv_w1[j], rwkv_w2[j], rwkv_a0[j], rwkv_a1[j],
                              rwkv_a2[j], rwkv_g1[j], rwkv_g2[j], rwkv_k_k[j], rwkv_k_a[j], rwkv_r_k[j],
                              rwkv_ln_g[j], rwkv_ln_b[j]).reshape(T, D)
        x2d = _peer_ffn(x2d, norm_ffn_g[layer], sc_f, sh_f, gt_f, bf(peer_w_q[layer]), peer_sub_keys[layer],
                        peer_u[layer], peer_v[layer], rows_per_batch=S)
    return _final_norm(x2d, final_norm_g).reshape(B, S, D)
```

```python
import functools
import math

import jax
import jax.numpy as jnp
from jax import lax
from jax.experimental import pallas as pl
from jax.experimental.pallas import tpu as pltpu
from jax.experimental.pallas import tpu_sc as plsc

F32 = jnp.float32
BF16 = jnp.bfloat16

D_MODEL = 1024
NORM_EPS = 1e-6

RET_HEADS = 4
RET_DK = D_MODEL // RET_HEADS
RET_DV = 2 * D_MODEL // RET_HEADS
RET_CHUNK = 128
RET_ROPE_BASE = 10000.0
RET_GN_EPS = 1e-5

GDN_HEADS = 8
GDN_DK = D_MODEL // GDN_HEADS
GDN_DV = D_MODEL // GDN_HEADS
GDN_CONV = 4
GDN_CHUNK = 64
GDN_QKV = GDN_HEADS * (2 * GDN_DK + GDN_DV)

RWKV_HEAD = 64
RWKV_HEADS = D_MODEL // RWKV_HEAD
RWKV_GN_EPS = 64e-5

PEER_KEYS = 128
PEER_HEADS = 8
PEER_DQ = 256
PEER_TOPK = 16
PEER_SEL = PEER_HEADS * PEER_TOPK

LANES = 128
SUBLANES = 8
VMEM_LIMIT = 56 * 1024 * 1024


def _cparams(sem):
    return pltpu.CompilerParams(dimension_semantics=sem, vmem_limit_bytes=VMEM_LIMIT)


def _bdot(a, b):
    return jnp.dot(a.astype(BF16), b.astype(BF16), preferred_element_type=F32)


def _bdot_nt(a, b):
    return lax.dot_general(a.astype(BF16), b.astype(BF16), (((1,), (1,)), ((), ())),
                           preferred_element_type=F32)


def _bdot_tn(a, b):
    return lax.dot_general(a.astype(BF16), b.astype(BF16), (((0,), (0,)), ((), ())),
                           preferred_element_type=F32)


def _sigmoid(x):
    return 1.0 / (1.0 + jnp.exp(-x))


def _silu(x):
    return x * _sigmoid(x)


def _softplus(x):
    return jnp.maximum(x, 0.0) + jnp.log1p(jnp.exp(-jnp.abs(x)))


def _norm_mod(x, g, sc, sh):
    ms = jnp.mean(x * x, axis=-1, keepdims=True)
    return (x * lax.rsqrt(ms + NORM_EPS) * g) * (1.0 + sc) + sh


def _adaln_kernel(c_ref, w_ref, b_ref, o_ref):
    cond = _silu(c_ref[...])
    o_ref[0] = _bdot(cond, w_ref[0]) + b_ref[0]


def _adaln(c, ada_w, ada_b):
    depth, d, n = ada_w.shape
    b = c.shape[0]
    tn = 1024
    return pl.pallas_call(
        _adaln_kernel,
        out_shape=jax.ShapeDtypeStruct((depth, b, n), F32),
        grid=(depth, n // tn),
        in_specs=[
            pl.BlockSpec((b, d), lambda l, j: (0, 0)),
            pl.BlockSpec((1, d, tn), lambda l, j: (l, 0, j)),
            pl.BlockSpec((1, 1, tn), lambda l, j: (l, 0, j)),
        ],
        out_specs=pl.BlockSpec((1, b, tn), lambda l, j: (l, 0, j)),
        compiler_params=_cparams(("parallel", "parallel")),
        name="adaln",
    )(c, ada_w, ada_b.reshape(depth, 1, n))


def _nm_matmul_kernel(x_ref, g_ref, sc_ref, sh_ref, w_ref, o_ref, *h_ref):
    h = _norm_mod(x_ref[...], g_ref[...], sc_ref[0], sh_ref[0])
    o_ref[...] = jnp.dot(h.astype(BF16), w_ref[...], preferred_element_type=F32).astype(o_ref.dtype)
    if h_ref:
        h_ref[0][...] = h


def _nm_matmul(x2d, g, sc, sh, w, *, rows_per_batch, tm=512, tn=None, emit_h=False):
    m, d = x2d.shape
    n = w.shape[1]
    nb = sc.shape[0]
    if tn is None:
        tn = n
    tpb = rows_per_batch // tm
    out_shape = [jax.ShapeDtypeStruct((m, n), F32)]
    out_specs = [pl.BlockSpec((tm, tn), lambda j, i: (i, j))]
    if emit_h:
        assert tn == n
        out_shape.append(jax.ShapeDtypeStruct((m, d), F32))
        out_specs.append(pl.BlockSpec((tm, d), lambda j, i: (i, 0)))
    res = pl.pallas_call(
        _nm_matmul_kernel,
        out_shape=out_shape,
        grid=(n // tn, m // tm),
        in_specs=[
            pl.BlockSpec((tm, d), lambda j, i: (i, 0)),
            pl.BlockSpec((1, d), lambda j, i: (0, 0)),
            pl.BlockSpec((1, 1, d), lambda j, i: (i // tpb, 0, 0)),
            pl.BlockSpec((1, 1, d), lambda j, i: (i // tpb, 0, 0)),
            pl.BlockSpec((d, tn), lambda j, i: (0, j)),
        ],
        out_specs=out_specs,
        compiler_params=_cparams(("parallel", "parallel")),
        name="norm_mod_matmul",
    )(x2d, g.reshape(1, d), sc.reshape(nb, 1, d), sh.reshape(nb, 1, d), w)
    return res if emit_h else res[0]


def _mm_res_kernel(*refs, has_mul):
    if has_mul:
        a_ref, m_ref, w_ref, r_ref, gt_ref, o_ref = refs
        a = a_ref[...] * m_ref[...]
    else:
        a_ref, w_ref, r_ref, gt_ref, o_ref = refs
        a = a_ref[...]
    y = jnp.dot(a.astype(BF16), w_ref[...], preferred_element_type=F32)
    o_ref[...] = r_ref[...] + gt_ref[0] * y


def _mm_res(a, w, res, gt, *, rows_per_batch, mul=None, tm=512):
    m, k = a.shape
    n = w.shape[1]
    nb = gt.shape[0]
    tpb = rows_per_batch // tm
    ins = [a]
    specs = [pl.BlockSpec((tm, k), lambda i: (i, 0))]
    if mul is not None:
        ins.append(mul)
        specs.append(pl.BlockSpec((tm, k), lambda i: (i, 0)))
    ins += [w, res, gt.reshape(nb, 1, n)]
    specs += [
        pl.BlockSpec((k, n), lambda i: (0, 0)),
        pl.BlockSpec((tm, n), lambda i: (i, 0)),
        pl.BlockSpec((1, 1, n), lambda i: (i // tpb, 0, 0)),
    ]
    return pl.pallas_call(
        functools.partial(_mm_res_kernel, has_mul=mul is not None),
        out_shape=jax.ShapeDtypeStruct((m, n), F32),
        grid=(m // tm,),
        in_specs=specs,
        out_specs=pl.BlockSpec((tm, n), lambda i: (i, 0)),
        compiler_params=_cparams(("parallel",)),
        name="matmul_residual",
    )(*ins)


def _ret_kernel(q_ref, k_ref, v_ref, gate_ref, cos_ref, sin_ref, dintra_ref, dq_ref, dk_ref, dchunk_ref,
                gng_ref, gnb_ref, o_ref, state_ref):
    H, dk, dv = RET_HEADS, RET_DK, RET_DV
    half = dk // 2

    @pl.when(pl.program_id(1) == 0)
    def _():
        state_ref[...] = jnp.zeros_like(state_ref)

    cos = cos_ref[...]
    sin = sin_ref[...]

    def rot(ref, h):
        x1 = ref[0, :, h * dk:h * dk + half]
        x2 = ref[0, :, h * dk + half:(h + 1) * dk]
        return jnp.concatenate([x1 * cos - x2 * sin, x1 * sin + x2 * cos], axis=-1)

    for h in range(H):
        q = rot(q_ref, h)
        k = rot(k_ref, h) * (dk ** -0.5)
        v = v_ref[0, :, h * dv:(h + 1) * dv]
        scores = _bdot_nt(q, k) * dintra_ref[h]
        st = state_ref[h]
        o = _bdot(scores, v) + _bdot(q, st) * dq_ref[h]
        state_ref[h] = st * dchunk_ref[h] + _bdot_tn(k * dk_ref[h], v)
        mu = jnp.mean(o, axis=-1, keepdims=True)
        var = jnp.mean(jnp.square(o - mu), axis=-1, keepdims=True)
        on = (o - mu) * lax.rsqrt(var + RET_GN_EPS) * gng_ref[h] + gnb_ref[h]
        g = gate_ref[0, :, h * dv:(h + 1) * dv]
        o_ref[0, :, h * dv:(h + 1) * dv] = (_silu(g) * on).astype(o_ref.dtype)


def _retention_scan(proj, gn_g, gn_b):
    B, S, _ = proj.shape
    H, dk, dv, C = RET_HEADS, RET_DK, RET_DV, RET_CHUNK
    half = dk // 2
    N = S // C
    inv_freq = RET_ROPE_BASE ** (-jnp.arange(half, dtype=F32) / half)
    ang = jnp.arange(S, dtype=F32)[:, None] * inv_freq[None, :]
    cos, sin = jnp.cos(ang), jnp.sin(ang)
    log_gamma = jnp.log1p(-jnp.exp2(-5.0 - jnp.arange(H, dtype=F32)))
    idx = jnp.arange(C, dtype=F32)
    diff = idx[:, None] - idx[None, :]
    causal = diff >= 0
    d_intra = jnp.where(causal[None], jnp.exp(jnp.where(causal, diff, 0.0)[None] * log_gamma[:, None, None]), 0.0)
    d_q = jnp.exp((idx[None, :] + 1.0) * log_gamma[:, None])[:, :, None]
    d_k = jnp.exp((C - 1.0 - idx)[None, :] * log_gamma[:, None])[:, :, None]
    d_chunk = jnp.exp(C * log_gamma)[:, None, None]
    qw, vw = H * dk, H * dv
    return pl.pallas_call(
        _ret_kernel,
        out_shape=jax.ShapeDtypeStruct((B, S, vw), BF16),
        grid=(B, N),
        in_specs=[
            pl.BlockSpec((1, C, qw), lambda b, n: (b, n, 0)),
            pl.BlockSpec((1, C, qw), lambda b, n: (b, n, 1)),
            pl.BlockSpec((1, C, vw), lambda b, n: (b, n, 1)),
            pl.BlockSpec((1, C, vw), lambda b, n: (b, n, 2)),
            pl.BlockSpec((C, half), lambda b, n: (n, 0)),
            pl.BlockSpec((C, half), lambda b, n: (n, 0)),
            pl.BlockSpec((H, C, C), lambda b, n: (0, 0, 0)),
            pl.BlockSpec((H, C, 1), lambda b, n: (0, 0, 0)),
            pl.BlockSpec((H, C, 1), lambda b, n: (0, 0, 0)),
            pl.BlockSpec((H, 1, 1), lambda b, n: (0, 0, 0)),
            pl.BlockSpec((H, 1, dv), lambda b, n: (0, 0, 0)),
            pl.BlockSpec((H, 1, dv), lambda b, n: (0, 0, 0)),
        ],
        out_specs=pl.BlockSpec((1, C, vw), lambda b, n: (b, n, 0)),
        scratch_shapes=[pltpu.VMEM((H, dk, dv), F32)],
        compiler_params=_cparams(("parallel", "arbitrary")),
        name="retention_scan",
    )(proj, proj, proj, proj, cos, sin, d_intra, d_q, d_k, d_chunk,
      gn_g.reshape(H, 1, dv), gn_b.reshape(H, 1, dv))


def _shift_rows(cur, prev8, s):
    rows = lax.broadcasted_iota(jnp.int32, cur.shape, 0)
    rolled = pltpu.roll(cur, s, axis=0)
    head = pltpu.roll(prev8, s, axis=0)
    head = jnp.concatenate([head, jnp.zeros((cur.shape[0] - SUBLANES, cur.shape[1]), cur.dtype)], axis=0)
    return jnp.where(rows < s, head, rolled)


def _cumsum_rows(x):
    rows = lax.broadcasted_iota(jnp.int32, x.shape, 0)
    s = 1
    while s < x.shape[0]:
        x = x + jnp.where(rows >= s, pltpu.roll(x, s, axis=0), 0.0)
        s *= 2
    return x


def _gdn_kernel(qkv_ref, gate_ref, ab_ref, cw_ref, alog_ref, dtb_ref, ng_ref, o_ref, state_ref, prev_ref):
    C, dk, H = GDN_CHUNK, GDN_DK, GDN_HEADS

    @pl.when(pl.program_id(1) == 0)
    def _():
        state_ref[...] = jnp.zeros_like(state_ref)
        prev_ref[...] = jnp.zeros_like(prev_ref)

    def conv_silu(col):
        cur = qkv_ref[0, :, col * dk:(col + 1) * dk]
        prev8 = prev_ref[:, col * dk:(col + 1) * dk]
        cw = cw_ref[:, col * dk:(col + 1) * dk]
        acc = cur * cw[GDN_CONV - 1:GDN_CONV]
        for s in range(1, GDN_CONV):
            acc = acc + _shift_rows(cur, prev8, s) * cw[GDN_CONV - 1 - s:GDN_CONV - s]
        prev_ref[:, col * dk:(col + 1) * dk] = cur[C - SUBLANES:]
        return _silu(acc)

    ab = ab_ref[0]
    g_all = -jnp.exp(alog_ref[...]) * _softplus(ab + dtb_ref[...])
    beta_all = _sigmoid(ab)
    ri = lax.broadcasted_iota(jnp.int32, (C, C), 0)
    ci = lax.broadcasted_iota(jnp.int32, (C, C), 1)
    incl = ri >= ci

    bdot_nt = lambda x, y: jnp.einsum('hid,hjd->hij', x.astype(BF16), y.astype(BF16), preferred_element_type=F32)
    bdot = lambda x, y: jnp.einsum('hij,hjk->hik', x, y, preferred_element_type=F32)
    stack = lambda xs: jnp.stack(xs, axis=0)

    def hdot(x, y):
        xh, yh = x.astype(BF16), y.astype(BF16)
        xl, yl = (x - xh.astype(F32)).astype(BF16), (y - yh.astype(F32)).astype(BF16)
        return bdot(xh, yh) + (bdot(xh, yl) + bdot(xl, yh))

    qs, ks, vs, betas, cums = [], [], [], [], []
    for h in range(H):
        q = conv_silu(h)
        k = conv_silu(H + h)
        qs.append(q * lax.rsqrt(jnp.sum(q * q, axis=-1, keepdims=True) + 1e-6) * (dk ** -0.5))
        ks.append(k * lax.rsqrt(jnp.sum(k * k, axis=-1, keepdims=True) + 1e-6))
        vs.append(conv_silu(2 * H + h))
        betas.append(beta_all[:, H + h:H + h + 1])
        cums.append(_cumsum_rows(jnp.broadcast_to(g_all[:, h:h + 1], (C, LANES))))
    q, k, v, beta, cum = stack(qs), stack(ks), stack(vs), stack(betas), stack(cums)
    cum_c = cum[:, :, :1]
    cum_last = cum[:, C - 1:C, :1]
    cum_r = stack([cums[h].T[:C, :] for h in range(H)])
    decay = jnp.where(incl, jnp.exp(jnp.where(incl, cum[:, :, :C] - cum_r, 0.0)), 0.0)

    L = jnp.where(ri > ci, bdot_nt(k, k) * decay, 0.0) * beta
    rhs = jnp.concatenate([k * (beta * jnp.exp(cum_c)), v * beta], axis=-1)
    eye = (ri == ci).astype(F32)
    p = -L
    inv = eye + p
    s = 2
    while s < C:
        p = hdot(p, p)
        inv = inv + hdot(inv, p)
        s *= 2
    sol = hdot(inv, rhs)
    a_qk = bdot_nt(q, k) * decay
    q_dec = q * jnp.exp(cum_c)
    k_dec = k * jnp.exp(cum_last - cum_c)
    e_last = jnp.exp(cum_last)

    for h in range(H):
        st = state_ref[h]
        u = sol[h, :, dk:] - _bdot(sol[h, :, :dk], st)
        o = _bdot(q_dec[h], st) + _bdot(a_qk[h], u)
        state_ref[h] = st * e_last[h] + _bdot_tn(k_dec[h], u)
        ms = jnp.mean(o * o, axis=-1, keepdims=True)
        o = o * lax.rsqrt(ms + NORM_EPS) * ng_ref[...]
        o_ref[0, :, h * dk:(h + 1) * dk] = (o * _silu(gate_ref[0, :, h * dk:(h + 1) * dk])).astype(o_ref.dtype)


def _gdn_scan(proj, proj_ab, conv_w, a_log, dt_bias, norm_g):
    B, S, _ = proj.shape
    H, dk, C = GDN_HEADS, GDN_DK, GDN_CHUNK
    N = S // C
    pad = lambda t: jnp.pad(t.astype(F32), (0, LANES - H)).reshape(1, LANES)
    row = pl.BlockSpec((1, LANES), lambda b, n: (0, 0))
    return pl.pallas_call(
        _gdn_kernel,
        out_shape=jax.ShapeDtypeStruct((B, S, H * dk), BF16),
        grid=(B, N),
        in_specs=[pl.BlockSpec((1, C, 3 * H * dk), lambda b, n: (b, n, 0)),
                  pl.BlockSpec((1, C, H * dk), lambda b, n: (b, n, 3)),
                  pl.BlockSpec((1, C, LANES), lambda b, n: (b, n, 0)),
                  pl.BlockSpec((GDN_CONV, 3 * H * dk), lambda b, n: (0, 0)),
                  row, row, row],
        out_specs=pl.BlockSpec((1, C, H * dk), lambda b, n: (b, n, 0)),
        scratch_shapes=[pltpu.VMEM((H, dk, dk), F32), pltpu.VMEM((SUBLANES, 3 * H * dk), F32)],
        compiler_params=_cparams(("parallel", "arbitrary")),
        name="gdn_scan",
    )(proj, proj, proj_ab, conv_w, pad(a_log), pad(dt_bias), norm_g.reshape(1, dk))


def _rwkv_proj_kernel(x_ref, xp_ref, g_ref, sc_ref, sh_ref, mu_ref, wr_ref, wk_ref, wv_ref, w1_ref, w2_ref,
                      a1_ref, a2_ref, g1_ref, g2_ref, w0_ref, a0_ref, kk_ref, ka_ref,
                      r_o, dec_o, k_o, v_o, kk_o, a_o, g_o, *, tiles_per_seq):
    h = _norm_mod(x_ref[...], g_ref[...], sc_ref[0], sh_ref[0])
    hp8 = _norm_mod(xp_ref[...], g_ref[...], sc_ref[0], sh_ref[0])
    seq_start = pl.program_id(0) % tiles_per_seq == 0
    first = jnp.where(seq_start, 0.0, hp8[SUBLANES - 1:SUBLANES, :])
    rows = lax.broadcasted_iota(jnp.int32, h.shape, 0)
    xx = jnp.where(rows == 0, first, pltpu.roll(h, 1, axis=0)) - h
    mix = lambda j: h + xx * mu_ref[j:j + 1, :]
    r = _bdot(mix(0), wr_ref[...])
    lw = w0_ref[...] + _bdot(jnp.tanh(_bdot(mix(1), w1_ref[...])), w2_ref[...])
    k = _bdot(mix(2), wk_ref[...])
    v = _bdot(mix(3), wv_ref[...])
    a = _sigmoid(a0_ref[...] + _bdot(_bdot(mix(4), a1_ref[...]), a2_ref[...]))
    g = _bdot(_sigmoid(_bdot(mix(5), g1_ref[...])), g2_ref[...])
    w = -_softplus(-lw) - 0.5
    r_o[...] = r
    dec_o[...] = jnp.exp(-jnp.exp(w))
    k_o[...] = k * (1.0 + (a - 1.0) * ka_ref[...])
    v_o[...] = v
    kk_o[...] = k * kk_ref[...]
    a_o[...] = a
    g_o[...] = g


def _pad_cols(w, n):
    return jnp.pad(w, ((0, 0), (0, n - w.shape[1])))


def _pad_rows(w, n):
    return jnp.pad(w, ((0, n - w.shape[0]), (0, 0)))


def _rwkv_proj(x2d, g, sc, sh, mu, w_r, w_k, w_v, w1, w2, a1, a2, g1, g2, w0, a0, k_k, k_a, *, rows_per_batch, tm=256):
    m, d = x2d.shape
    nb = sc.shape[0]
    tpb = rows_per_batch // tm
    lora_w = LANES * pl.cdiv(w1.shape[1], LANES)
    lora_g = LANES * pl.cdiv(g1.shape[1], LANES)
    bf = lambda t: t.astype(BF16)
    full = lambda a: pl.BlockSpec(a.shape, lambda i: (0,) * a.ndim)
    row = lambda t: t.reshape(1, d)
    ws = [bf(w_r), bf(w_k), bf(w_v), bf(_pad_cols(w1, lora_w)), bf(_pad_rows(w2, lora_w)),
          bf(_pad_cols(a1, lora_w)), bf(_pad_rows(a2, lora_w)), bf(_pad_cols(g1, lora_g)), bf(_pad_rows(g2, lora_g)),
          row(w0), row(a0), row(k_k), row(k_a)]
    tile = pl.BlockSpec((tm, d), lambda i: (i, 0))
    return pl.pallas_call(
        functools.partial(_rwkv_proj_kernel, tiles_per_seq=tpb),
        out_shape=[jax.ShapeDtypeStruct((m, d), F32)] * 7,
        grid=(m // tm,),
        in_specs=[
            tile,
            pl.BlockSpec((SUBLANES, d), lambda i: (jnp.maximum(i * (tm // SUBLANES) - 1, 0), 0)),
            pl.BlockSpec((1, d), lambda i: (0, 0)),
            pl.BlockSpec((1, 1, d), lambda i: (i // tpb, 0, 0)),
            pl.BlockSpec((1, 1, d), lambda i: (i // tpb, 0, 0)),
            full(mu),
        ] + [full(w) for w in ws],
        out_specs=[tile] * 7,
        compiler_params=_cparams(("parallel",)),
        name="rwkv_proj",
    )(x2d, x2d, g.reshape(1, d), sc.reshape(nb, 1, d), sh.reshape(nb, 1, d), mu, *ws)


RWKV_VUNROLL = 64


def _rwkv_scan_kernel(r_ref, w_ref, k_ref, v_ref, kk_ref, a_ref, rk_ref, lng_ref, lnb_ref, y_ref, state_ref, yrow_ref):
    n = RWKV_HEAD

    @pl.when(pl.program_id(0) == 0)
    def _():
        state_ref[...] = jnp.zeros_like(state_ref)

    def step(t, carry):
        r, w, k, kkr, a = r_ref[t], w_ref[t], k_ref[t], kk_ref[t], a_ref[t]
        kk = kkr * lax.rsqrt(jnp.sum(kkr * kkr, axis=0, keepdims=True) + 1e-6)
        nkk = -kk
        kka = kk * a

        def vloop(vb, c):
            for j in range(RWKV_VUNROLL):
                vi = vb * RWKV_VUNROLL + j
                sv = state_ref[vi]
                sa = jnp.sum(sv * nkk, axis=0, keepdims=True)
                vrow = v_ref[t, pl.ds(vi, 1), :]
                sn = sv * w + sa * kka + vrow * k
                state_ref[vi] = sn
                yrow_ref[pl.ds(vi, 1), :] = jnp.sum(sn * r, axis=0, keepdims=True)
            return c

        lax.fori_loop(0, n // RWKV_VUNROLL, vloop, 0)
        y = yrow_ref[...]
        mu = jnp.mean(y, axis=0, keepdims=True)
        var = jnp.mean(jnp.square(y - mu), axis=0, keepdims=True)
        yn = (y - mu) * lax.rsqrt(var + RWKV_GN_EPS) * lng_ref[...] + lnb_ref[...]
        bonus = jnp.sum(r * k * rk_ref[...], axis=0, keepdims=True)
        y_ref[t] = yn + bonus * v_ref[t]
        return carry

    lax.fori_loop(0, r_ref.shape[0], step, 0)


def _rwkv_scan(r, dec, k, v, kk, a, r_k, ln_g, ln_b, *, tc=32):
    B, S, D = r.shape
    H, n = RWKV_HEADS, RWKV_HEAD
    lanes = B * H
    to_scan = lambda t: jnp.transpose(t.reshape(B, S, H, n), (1, 3, 0, 2)).reshape(S, n, lanes)
    per_head = lambda p: jnp.tile(p.T, (1, B))
    blk = pl.BlockSpec((tc, n, lanes), lambda i: (i, 0, 0))
    cst = pl.BlockSpec((n, lanes), lambda i: (0, 0))
    y = pl.pallas_call(
        _rwkv_scan_kernel,
        out_shape=jax.ShapeDtypeStruct((S, n, lanes), F32),
        grid=(S // tc,),
        in_specs=[blk] * 6 + [cst] * 3,
        out_specs=blk,
        scratch_shapes=[pltpu.VMEM((n, n, lanes), F32), pltpu.VMEM((n, lanes), F32)],
        compiler_params=_cparams(("arbitrary",)),
        name="rwkv_scan",
    )(to_scan(r), to_scan(dec), to_scan(k), to_scan(v), to_scan(kk), to_scan(a),
      per_head(r_k), per_head(ln_g), per_head(ln_b))
    return jnp.transpose(y.reshape(S, n, B, H), (2, 0, 3, 1)).reshape(B, S, D)


def _rwkv_mixer(x, g, sc, sh, gt, mu, w_r, w_k, w_v, w_o, w0, w1, w2, a0, a1, a2, g1, g2, k_k, k_a, r_k, ln_g, ln_b):
    B, S, D = x.shape
    x2d = x.reshape(B * S, D)
    tm = min(256, S)
    r, dec, k, v, kk, a, gg = _rwkv_proj(x2d, g, sc, sh, mu, w_r, w_k, w_v, w1, w2, a1, a2, g1, g2, w0, a0, k_k, k_a,
                                         rows_per_batch=S, tm=tm)
    sh3 = lambda t: t.reshape(B, S, D)
    y = _rwkv_scan(sh3(r), sh3(dec), sh3(k), sh3(v), sh3(kk), sh3(a), r_k, ln_g, ln_b, tc=min(32, S))
    out = _mm_res(y.reshape(B * S, D), w_o.astype(BF16), x2d, gt, rows_per_batch=S, mul=gg, tm=min(512, S))
    return out.reshape(B, S, D)


def _rwkv_mixer_test(x, g, sc, sh, p):
    gt = jnp.ones_like(sc)
    return _rwkv_mixer(x, g, sc, sh, gt, p['mu'], p['w_r'], p['w_k'], p['w_v'], p['w_o'], p['w0'], p['w1'], p['w2'],
                       p['a0'], p['a1'], p['a2'], p['g1'], p['g2'], p['k_k'], p['k_a'], p['r_k'], p['ln_g'],
                       p['ln_b']) - x


def _topk_rows(s, k, rows=None):
    if rows is None:
        rows = lax.broadcasted_iota(jnp.int32, s.shape, 0)
    n = jnp.iinfo(jnp.int32).max
    vals, ids = [], []
    for _ in range(k):
        m = jnp.max(s, axis=0, keepdims=True)
        idx = jnp.min(jnp.where(s == m, rows, n), axis=0, keepdims=True)
        vals.append(m)
        ids.append(idx)
        s = jnp.where(rows == idx, -jnp.inf, s)
    return jnp.concatenate(vals, axis=0), jnp.concatenate(ids, axis=0)


def _take_rows(table, pos):
    out = jnp.zeros(pos.shape, table.dtype)
    for m in range(table.shape[0]):
        out = jnp.where(pos == m, table[m:m + 1, :], out)
    return out


_PEER_CAND = [(i, PEER_TOPK // (i + 1)) for i in range(PEER_TOPK)]
_PEER_NCAND = sum(n for _, n in _PEER_CAND)
_PEER_NCAND_PAD = SUBLANES * pl.cdiv(_PEER_NCAND, SUBLANES)


def _peer_cand_codes():
    codes = [i * PEER_TOPK + j for i, n in _PEER_CAND for j in range(n)]
    codes += [PEER_TOPK * PEER_TOPK + p for p in range(_PEER_NCAND_PAD - _PEER_NCAND)]
    return jnp.broadcast_to(jnp.asarray(codes, jnp.int32)[:, None], (_PEER_NCAND_PAD, LANES))


def _peer_route_kernel(q_ref, keys_ref, codes_ref, idx_o, gate_o, idx_tok_o, gate_tok_o):
    K, half = PEER_TOPK, PEER_DQ // 2

    tm = q_ref.shape[0]
    G = 2
    codes = jnp.concatenate([codes_ref[...]] * (G * tm // LANES), axis=1)
    pad = jnp.full((_PEER_NCAND_PAD - _PEER_NCAND, tm), -jnp.inf, F32)

    def group(hg, carry):
        ss = []
        for dh in range(G):
            for p in range(2):
                c = pl.multiple_of((2 * (G * hg + dh) + p) * half, half)
                ss.append(_bdot_nt(keys_ref[G * hg + dh, p], q_ref[:, pl.ds(c, half)]))
        vals, ids = _topk_rows(jnp.concatenate(ss, axis=1), K)
        part = lambda t, j: t[:, j * tm:(j + 1) * tm]
        cands = []
        for dh in range(G):
            va, vb = part(vals, 2 * dh), part(vals, 2 * dh + 1)
            cands.append(jnp.concatenate([va[i:i + 1, :] + vb[:n, :] for i, n in _PEER_CAND] + [pad], axis=0))
        best, pos = _topk_rows(jnp.concatenate(cands, axis=1), K, codes)
        for dh in range(G):
            ia, ib = part(ids, 2 * dh), part(ids, 2 * dh + 1)
            ps, bs = part(pos, dh), part(best, dh)
            expert = _take_rows(ia, ps // K) * PEER_KEYS + _take_rows(ib, ps % K)
            e = jnp.exp(bs - bs[0:1, :])
            r0 = pl.multiple_of((G * hg + dh) * K, K)
            idx_o[pl.ds(r0, K), :] = expert
            gate_o[pl.ds(r0, K), :] = e / jnp.sum(e, axis=0, keepdims=True)
        return carry

    lax.fori_loop(0, PEER_HEADS // G, group, 0)
    idx_tok_o[...] = idx_o[...].T
    gate_tok_o[...] = gate_o[...].T


def _peer_route(q, sub_keys, *, tm=256):
    t, n = q.shape
    blk = pl.BlockSpec((PEER_SEL, tm), lambda i: (0, i))
    tok = pl.BlockSpec((tm, PEER_SEL), lambda i: (i, 0))
    return pl.pallas_call(
        _peer_route_kernel,
        out_shape=[jax.ShapeDtypeStruct((PEER_SEL, t), jnp.int32), jax.ShapeDtypeStruct((PEER_SEL, t), F32),
                   jax.ShapeDtypeStruct((t, PEER_SEL), jnp.int32), jax.ShapeDtypeStruct((t, PEER_SEL), F32)],
        grid=(t // tm,),
        in_specs=[pl.BlockSpec((tm, n), lambda i: (i, 0)),
                  pl.BlockSpec(sub_keys.shape, lambda i: (0, 0, 0, 0)),
                  pl.BlockSpec((_PEER_NCAND_PAD, LANES), lambda i: (0, 0))],
        out_specs=[blk, blk, tok, tok],
        compiler_params=_cparams(("parallel",)),
        name="peer_route",
    )(q, sub_keys.astype(BF16), _peer_cand_codes())


def _pack_kernel(u_ref, v_ref, uv_o, u2_o, v2_o):
    half = u_ref.shape[1] // 2
    high = jnp.int32(-65536)
    bits = lambda x: lax.bitcast_convert_type(x.astype(BF16).astype(F32), jnp.int32)
    low = lambda b: lax.shift_right_logical(b, 16)
    ub, vb = bits(u_ref[...]), bits(v_ref[...])
    uv_o[...] = (vb & high) | low(ub)
    u2_o[...] = (ub[:, half:] & high) | low(ub[:, :half])
    v2_o[...] = (vb[:, half:] & high) | low(vb[:, :half])


def _pack_tables(u, v, *, tm=512):
    e, d = u.shape
    tile = lambda w: pl.BlockSpec((tm, w), lambda i: (i, 0))
    uv, u2, v2 = pl.pallas_call(
        _pack_kernel,
        out_shape=[jax.ShapeDtypeStruct((e, d), jnp.int32), jax.ShapeDtypeStruct((e, d // 2), jnp.int32),
                   jax.ShapeDtypeStruct((e, d // 2), jnp.int32)],
        grid=(e // tm,),
        in_specs=[tile(d), tile(d)],
        out_specs=[tile(d), tile(d // 2), tile(d // 2)],
        compiler_params=_cparams(("parallel",)),
        name="peer_pack",
    )(u, v)
    return uv.reshape(e, d // LANES, LANES), u2, v2


PEER_NBUF = 8


def _gelu(x):
    return 0.5 * x * (1.0 + lax.erf(x * (2.0 ** -0.5)))


def _peer_eval(packed, x, gate):
    u = lax.bitcast_convert_type(packed << 16, F32)
    v = lax.bitcast_convert_type(packed & jnp.int32(-65536), F32)
    act = jnp.sum(jnp.sum(u * x[None], axis=1), axis=1, keepdims=True)
    wgt = gate * _gelu(act)
    return jnp.sum(v * wgt[:, :, None], axis=0)


def _gate_column(gate_ref, t):
    lane = lax.broadcasted_iota(jnp.int32, gate_ref.shape, 1)
    return jnp.sum(jnp.where(lane == t, gate_ref[...], 0.0), axis=1, keepdims=True)


def _peer_expert_kernel(idx_hbm, gate_ref, h_ref, xres_ref, gt_ref, uv_hbm, *rest, tok0):
    o_ref, idx_smem, buf, sem_idx, sem = rest[-5:]
    tb = h_ref.shape[0]
    nsel = PEER_SEL
    base = (pl.program_id(0) * tb + tok0) * nsel
    cp = pltpu.make_async_copy(idx_hbm.at[pl.ds(base, tb * nsel)], idx_smem, sem_idx)
    cp.start()
    cp.wait()

    def issue(t, slot):
        for k in range(nsel):
            e = idx_smem[t * nsel + k]
            pltpu.make_async_copy(uv_hbm.at[e], buf.at[slot, k], sem.at[slot]).start()

    def wait(slot):
        pltpu.make_async_copy(uv_hbm.at[pl.ds(0, nsel)], buf.at[slot], sem.at[slot]).wait()

    for t0 in range(PEER_NBUF - 1):
        issue(t0, t0)

    def body(t, carry):
        slot = t % PEER_NBUF
        nxt = t + PEER_NBUF - 1

        @pl.when(nxt < tb)
        def _():
            issue(nxt, nxt % PEER_NBUF)

        wait(slot)
        out = _peer_eval(buf[slot], h_ref[t], _gate_column(gate_ref, t))
        o_ref[t] = xres_ref[t] + gt_ref[0] * out
        return carry

    lax.fori_loop(0, tb, body, 0)


def _peer_expert(idx_flat, gate_t, h3, xres3, gt3, uv, *, rows_per_batch, tok0, ntok, hx0, tb=128, after=None):
    _, c, _ = h3.shape
    tpb = rows_per_batch // tb
    b0 = tok0 // tb
    hb0 = (tok0 - hx0) // tb
    tok = pl.BlockSpec((tb, c, LANES), lambda i: (i + hb0, 0, 0))
    return pl.pallas_call(
        functools.partial(_peer_expert_kernel, tok0=tok0),
        out_shape=jax.ShapeDtypeStruct((ntok, c, LANES), F32),
        grid=(ntok // tb,),
        in_specs=[
            pl.BlockSpec(memory_space=pl.ANY),
            pl.BlockSpec((PEER_SEL, tb), lambda i: (0, i + b0)),
            tok, tok,
            pl.BlockSpec((1, c, LANES), lambda i: ((i + b0) // tpb, 0, 0)),
            pl.BlockSpec(memory_space=pl.ANY),
        ] + ([] if after is None else [pl.BlockSpec(memory_space=pl.ANY)]),
        out_specs=pl.BlockSpec((tb, c, LANES), lambda i: (i, 0, 0)),
        scratch_shapes=[
            pltpu.SMEM((tb * PEER_SEL,), jnp.int32),
            pltpu.VMEM((PEER_NBUF, PEER_SEL, c, LANES), jnp.int32),
            pltpu.SemaphoreType.DMA,
            pltpu.SemaphoreType.DMA((PEER_NBUF,)),
        ],
        compiler_params=_cparams(("arbitrary",)),
        name="peer_expert",
    )(idx_flat, gate_t, h3, xres3, gt3, uv, *(() if after is None else (after,)))


SC_CORES = 2
SC_SUBCORES = 16
SC_WORKERS = SC_CORES * SC_SUBCORES
SC_LANES = 16
SC_TOKENS = 8
SC_GATHER_ROWS = 32
SC_CHUNKS = 8
_SC_PARAMS = pltpu.CompilerParams(needs_layout_passes=False)


def _sc_unpack(w):
    return (lax.bitcast_convert_type(w << 16, F32), lax.bitcast_convert_type(w & jnp.int32(-65536), F32))


def _sc_mesh():
    return plsc.VectorSubcoreMesh(core_axis_name="core", subcore_axis_name="subcore")


def _sc_act(u2, idx_flat, h_flat, *, tok0, ntok, d):
    nsel, L, G, CG, TBK = PEER_SEL, SC_LANES, SC_GATHER_ROWS, SC_CHUNKS, SC_TOKENS
    half = d // 2
    tpw = ntok // SC_WORKERS
    ng = nsel // G
    ngt = TBK * ng

    @functools.partial(
        pl.kernel, out_type=jax.ShapeDtypeStruct((ntok * nsel,), F32), mesh=_sc_mesh(),
        scratch_types=[pltpu.VMEM((TBK * nsel,), jnp.int32), pltpu.VMEM((TBK * d,), F32),
                       pltpu.VMEM((2, G, half), jnp.int32), pltpu.VMEM((TBK * nsel,), F32),
                       pltpu.VMEM((G * L,), F32), pltpu.SemaphoreType.DMA((2,))],
        compiler_params=_SC_PARAMS, name="peer_sc_act")
    def k(u_hbm, i_hbm, x_hbm, o_hbm, idx_v, x_v, buf, act_v, acc_v, sem):
        wid = lax.axis_index("core") * SC_SUBCORES + lax.axis_index("subcore")
        lanes = lax.iota(jnp.int32, L)

        def gather(gi, b):
            return pltpu.make_async_copy(u_hbm.at[idx_v.at[pl.ds(gi * G, G)]], buf.at[b], sem.at[b])

        @pl.loop(0, tpw // TBK)
        def _(bi):
            tl = wid * tpw + bi * TBK
            pltpu.sync_copy(i_hbm.at[pl.ds((tok0 + tl) * nsel, TBK * nsel)], idx_v)
            pltpu.sync_copy(x_hbm.at[pl.ds((tok0 + tl) * d, TBK * d)], x_v)
            gather(0, 0).start()

            @pl.loop(0, ngt, step=2)
            def _(g0):
                for b in range(2):
                    gi = g0 + b

                    @pl.when(gi + 1 < ngt)
                    def _():
                        gather(gi + 1, 1 - b).start()

                    gather(gi, b).wait()
                    xbase = (gi // ng) * d
                    for jg in range(half // (CG * L)):
                        xl = [x_v[pl.ds(pl.multiple_of(xbase + (jg * CG + c) * L, L), L)] for c in range(CG)]
                        xh = [x_v[pl.ds(pl.multiple_of(xbase + half + (jg * CG + c) * L, L), L)] for c in range(CG)]

                        def row(r, carry):
                            ps = []
                            for c in range(CG):
                                lo, hi = _sc_unpack(buf[b, r, pl.ds((jg * CG + c) * L, L)])
                                ps.append(lo * xl[c] + hi * xh[c])
                            while len(ps) > 1:
                                ps = [ps[i] + ps[i + 1] for i in range(0, len(ps), 2)]
                            off = pl.multiple_of(r * L, L)
                            if jg == 0:
                                acc_v[pl.ds(off, L)] = ps[0]
                            else:
                                acc_v[pl.ds(off, L)] = acc_v[pl.ds(off, L)] + ps[0]
                            return carry

                        plsc.parallel_loop(0, G, carry=jnp.int32(0))(row)
                    for part in range(G // L):
                        vec = jnp.zeros((L,), F32)
                        for r in range(L):
                            vec = jnp.where(lanes == r, jnp.sum(acc_v[pl.ds((part * L + r) * L, L)]), vec)
                        act_v[pl.ds(pl.multiple_of(gi * G + part * L, L), L)] = vec

            pltpu.sync_copy(act_v, o_hbm.at[pl.ds(tl * nsel, TBK * nsel)])

    return k(u2, idx_flat, h_flat)


def _sc_out(v2, idx_flat, w_flat, xres_flat, gt_flat, *, tok0, ntok, d, rows_per_batch):
    nsel, L, G, CG, TBK = PEER_SEL, SC_LANES, SC_GATHER_ROWS, SC_CHUNKS, SC_TOKENS
    half = d // 2
    tpw = ntok // SC_WORKERS
    ng = nsel // G
    ngt = TBK * ng

    @functools.partial(
        pl.kernel, out_type=jax.ShapeDtypeStruct((ntok * d,), F32), mesh=_sc_mesh(),
        scratch_types=[pltpu.VMEM((TBK * nsel,), jnp.int32), pltpu.VMEM((TBK * nsel,), F32),
                       pltpu.VMEM((2, G, half), jnp.int32), pltpu.VMEM((TBK * d,), F32),
                       pltpu.VMEM((TBK * d,), F32), pltpu.VMEM((d,), F32), pltpu.SemaphoreType.DMA((2,))],
        compiler_params=_SC_PARAMS, name="peer_sc_out")
    def k(v_hbm, i_hbm, w_hbm, xr_hbm, gt_hbm, o_hbm, idx_v, w_v, buf, out_v, xr_v, gt_v, sem):
        wid = lax.axis_index("core") * SC_SUBCORES + lax.axis_index("subcore")

        def gather(gi, b):
            return pltpu.make_async_copy(v_hbm.at[idx_v.at[pl.ds(gi * G, G)]], buf.at[b], sem.at[b])

        @pl.loop(0, tpw // TBK)
        def _(bi):
            tl = wid * tpw + bi * TBK
            pltpu.sync_copy(i_hbm.at[pl.ds((tok0 + tl) * nsel, TBK * nsel)], idx_v)
            pltpu.sync_copy(w_hbm.at[pl.ds(tl * nsel, TBK * nsel)], w_v)
            gather(0, 0).start()
            pltpu.sync_copy(xr_hbm.at[pl.ds((tok0 + tl) * d, TBK * d)], xr_v)
            pltpu.sync_copy(gt_hbm.at[pl.ds(((tok0 + tl) // rows_per_batch) * d, d)], gt_v)

            @pl.loop(0, TBK * d // L)
            def _(i):
                out_v[pl.ds(pl.multiple_of(i * L, L), L)] = jnp.zeros((L,), F32)

            @pl.loop(0, ngt, step=2)
            def _(g0):
                for b in range(2):
                    gi = g0 + b

                    @pl.when(gi + 1 < ngt)
                    def _():
                        gather(gi + 1, 1 - b).start()

                    gather(gi, b).wait()
                    obase = (gi // ng) * d
                    for jg in range(half // (CG * L)):
                        lo0 = obase + jg * CG * L
                        hi0 = lo0 + half
                        accs = tuple(out_v[pl.ds(pl.multiple_of(lo0 + c * L, L), L)] for c in range(CG)) + \
                            tuple(out_v[pl.ds(pl.multiple_of(hi0 + c * L, L), L)] for c in range(CG))

                        def row(r, accs):
                            wk = plsc.load_gather(w_v, [jnp.full((L,), gi * G + r, jnp.int32)])
                            new = list(accs)
                            for c in range(CG):
                                lo, hi = _sc_unpack(buf[b, r, pl.ds((jg * CG + c) * L, L)])
                                new[c] = new[c] + wk * lo
                                new[CG + c] = new[CG + c] + wk * hi
                            return tuple(new)

                        accs = lax.fori_loop(0, G, row, accs)
                        for c in range(CG):
                            out_v[pl.ds(pl.multiple_of(lo0 + c * L, L), L)] = accs[c]
                            out_v[pl.ds(pl.multiple_of(hi0 + c * L, L), L)] = accs[CG + c]

            @pl.loop(0, TBK * d // L)
            def _(i):
                off = pl.multiple_of(i * L, L)
                goff = pl.multiple_of((i % (d // L)) * L, L)
                out_v[pl.ds(off, L)] = xr_v[pl.ds(off, L)] + gt_v[pl.ds(goff, L)] * out_v[pl.ds(off, L)]

            pltpu.sync_copy(out_v, o_hbm.at[pl.ds(tl * d, TBK * d)])

    return k(v2, idx_flat, w_flat, xres_flat, gt_flat)


def _peer_weight_kernel(act_ref, gate_ref, *rest):
    rest[-1][...] = gate_ref[...] * _gelu(act_ref[...])


def _peer_weight(act, gate_tok, *, tok0, tm=1024, after=None):
    n, k = act.shape
    tm = math.gcd(math.gcd(n, tok0), tm)
    assert tm % SUBLANES == 0
    b0 = tok0 // tm
    return pl.pallas_call(
        _peer_weight_kernel,
        out_shape=jax.ShapeDtypeStruct((n, k), F32),
        grid=(n // tm,),
        in_specs=[pl.BlockSpec((tm, k), lambda i: (i, 0)), pl.BlockSpec((tm, k), lambda i: (i + b0, 0))]
        + ([] if after is None else [pl.BlockSpec(memory_space=pl.ANY)]),
        out_specs=pl.BlockSpec((tm, k), lambda i: (i, 0)),
        compiler_params=_cparams(("parallel",)),
        name="peer_weight",
    )(act, gate_tok, *(() if after is None else (after,)))


PEER_SC_SHARE = (19, 32)
PEER_SC_CALLS = 2
PEER_TC_SPLIT = (1, 1, 2)


def _peer_experts(idx_flat, gate_t, gate_tok, h, xres, gt, uv, u2, v2, *, rows_per_batch):
    t, d = h.shape
    c = d // LANES
    tb = min(128, rows_per_batch)
    unit = SC_WORKERS * SC_TOKENS * PEER_SC_CALLS
    n_sc = t * PEER_SC_SHARE[0] // PEER_SC_SHARE[1] // unit * unit
    ch = n_sc // PEER_SC_CALLS
    h3, xres3 = h[n_sc:].reshape(t - n_sc, c, LANES), xres[n_sc:].reshape(t - n_sc, c, LANES)
    gt3 = gt.reshape(-1, c, LANES)
    blocks = (t - n_sc) // tb
    split = PEER_TC_SPLIT if n_sc and blocks >= sum(PEER_TC_SPLIT) else (1,) if blocks else ()
    cum = [sum(split[:j]) for j in range(len(split) + 1)]
    bounds = [n_sc + (blocks * cj // max(cum[-1], 1)) * tb for cj in cum]

    def gather_call(j, after):
        lo, hi = bounds[j], bounds[j + 1]
        return _peer_expert(idx_flat, gate_t, h3, xres3, gt3, uv, rows_per_batch=rows_per_batch, tok0=lo, ntok=hi - lo,
                            hx0=n_sc, tb=tb, after=after)

    sc_outs, tc_outs, last = [], [], None
    if n_sc:
        h_flat, xres_flat, gt_flat = h.reshape(-1), xres.reshape(-1), gt.reshape(-1)
        acts = [_sc_act(u2, idx_flat, h_flat, tok0=i * ch, ntok=ch, d=d) for i in range(PEER_SC_CALLS)]
        for i, a in enumerate(acts):
            if len(tc_outs) < len(split):
                tc_outs.append(gather_call(len(tc_outs), last))
                last = tc_outs[-1]
            last = _peer_weight(a.reshape(ch, PEER_SEL), gate_tok, tok0=i * ch, after=last)
            sc_outs.append(_sc_out(v2, idx_flat, last.reshape(-1), xres_flat, gt_flat, tok0=i * ch, ntok=ch, d=d,
                                   rows_per_batch=rows_per_batch).reshape(ch, d))
    while len(tc_outs) < len(split):
        tc_outs.append(gather_call(len(tc_outs), last))
        last = tc_outs[-1]
    return jnp.concatenate(sc_outs + [o.reshape(-1, d) for o in tc_outs], axis=0)


def _peer_ffn(x2d, g, sc, sh, gt, w_q, sub_keys, u, v, *, rows_per_batch):
    tm = min(512, rows_per_batch)
    q, h = _nm_matmul(x2d, g, sc, sh, w_q, rows_per_batch=rows_per_batch, tm=tm, emit_h=True)
    _, gate_t, idx_tok, gate_tok = _peer_route(q, sub_keys, tm=min(512, rows_per_batch))
    return _peer_experts(idx_tok.reshape(-1), gate_t, gate_tok, h, x2d, gt, *_pack_tables(u, v),
                         rows_per_batch=rows_per_batch)


def _peer_test(x, g, sc, sh, w_q, sub_keys, u, v):
    B, S, D = x.shape
    x2d = x.reshape(B * S, D)
    return (_peer_ffn(x2d, g, sc, sh, jnp.ones_like(sc), w_q.astype(BF16), sub_keys, u, v,
                      rows_per_batch=S) - x2d).reshape(B, S, D)


def _final_norm_kernel(x_ref, g_ref, o_ref):
    x = x_ref[...]
    ms = jnp.mean(x * x, axis=-1, keepdims=True)
    o_ref[...] = x * lax.rsqrt(ms + NORM_EPS) * g_ref[...]


def _final_norm(x2d, g, *, tm=512):
    m, d = x2d.shape
    tile = pl.BlockSpec((tm, d), lambda i: (i, 0))
    return pl.pallas_call(
        _final_norm_kernel,
        out_shape=jax.ShapeDtypeStruct((m, d), F32),
        grid=(m // tm,),
        in_specs=[tile, pl.BlockSpec((1, d), lambda i: (0, 0))],
        out_specs=tile,
        compiler_params=_cparams(("parallel",)),
        name="final_norm",
    )(x2d, g.reshape(1, d))


def kernel(x, c, ada_w, ada_b, norm_mix_g, norm_ffn_g, final_norm_g, ret_w_in, ret_w_out, ret_gn_g, ret_gn_b, gdn_w_in, gdn_conv_w, gdn_a_log, gdn_dt_bias, gdn_norm_g, gdn_w_out, rwkv_mu, rwkv_w_r, rwkv_w_k, rwkv_w_v, rwkv_w_o, rwkv_w0, rwkv_w1, rwkv_w2, rwkv_a0, rwkv_a1, rwkv_a2, rwkv_g1, rwkv_g2, rwkv_k_k, rwkv_k_a, rwkv_r_k, rwkv_ln_g, rwkv_ln_b, peer_w_q, peer_sub_keys, peer_u, peer_v):
    B, S, D = x.shape
    T = B * S
    depth = ada_w.shape[0]
    bf = lambda t: t.astype(BF16)
    mod = _adaln(c, ada_w, ada_b)
    x2d = x.reshape(T, D)
    for layer in range(depth):
        sh_m, sc_m, gt_m, sh_f, sc_f, gt_f = [mod[layer, :, i * D:(i + 1) * D] for i in range(6)]
        g_mix = norm_mix_g[layer]
        kind, j = layer % 3, layer // 3
        if kind == 0:
            proj = _nm_matmul(x2d, g_mix, sc_m, sh_m, bf(ret_w_in[j]), rows_per_batch=S, tn=2048)
            o = _retention_scan(proj.reshape(B, S, -1), ret_gn_g[j], ret_gn_b[j])
            x2d = _mm_res(o.reshape(T, -1), bf(ret_w_out[j]), x2d, gt_m, rows_per_batch=S)
        elif kind == 1:
            w = gdn_w_in[j]
            wide = GDN_QKV + GDN_HEADS * GDN_DV
            proj = _nm_matmul(x2d, g_mix, sc_m, sh_m, bf(w[:, :wide]), rows_per_batch=S, tn=2048)
            proj_ab = _nm_matmul(x2d, g_mix, sc_m, sh_m, bf(_pad_cols(w[:, wide:], LANES)), rows_per_batch=S)
            o = _gdn_scan(proj.reshape(B, S, -1), proj_ab.reshape(B, S, -1), gdn_conv_w[j], gdn_a_log[j],
                          gdn_dt_bias[j], gdn_norm_g[j])
            x2d = _mm_res(o.reshape(T, -1), bf(gdn_w_out[j]), x2d, gt_m, rows_per_batch=S)
        else:
            x2d = _rwkv_mixer(x2d.reshape(B, S, D), g_mix, sc_m, sh_m, gt_m, rwkv_mu[j], rwkv_w_r[j], rwkv_w_k[j],
                              rwkv_w_v[j], rwkv_w_o[j], rwkv_w0[j], rwkv_w1[j], rwkv_w2[j], rwkv_a0[j], rwkv_a1[j],
                              rwkv_a2[j], rwkv_g1[j], rwkv_g2[j], rwkv_k_k[j], rwkv_k_a[j], rwkv_r_k[j],
                              rwkv_ln_g[j], rwkv_ln_b[j]).reshape(T, D)
        x2d = _peer_ffn(x2d, norm_ffn_g[layer], sc_f, sh_f, gt_f, bf(peer_w_q[layer]), peer_sub_keys[layer],
                        peer_u[layer], peer_v[layer], rows_per_batch=S)
    return _final_norm(x2d, final_norm_g).reshape(B, S, D)
```

```python
import functools
import math

import jax
import jax.numpy as jnp
from jax import lax
from jax.experimental import pallas as pl
from jax.experimental.pallas import tpu as pltpu
from jax.experimental.pallas import tpu_sc as plsc

F32 = jnp.float32
BF16 = jnp.bfloat16

D_MODEL = 1024
NORM_EPS = 1e-6

RET_HEADS = 4
RET_DK = D_MODEL // RET_HEADS
RET_DV = 2 * D_MODEL // RET_HEADS
RET_CHUNK = 128
RET_ROPE_BASE = 10000.0
RET_GN_EPS = 1e-5

GDN_HEADS = 8
GDN_DK = D_MODEL // GDN_HEADS
GDN_DV = D_MODEL // GDN_HEADS
GDN_CONV = 4
GDN_CHUNK = 64
GDN_QKV = GDN_HEADS * (2 * GDN_DK + GDN_DV)

RWKV_HEAD = 64
RWKV_HEADS = D_MODEL // RWKV_HEAD
RWKV_GN_EPS = 64e-5

PEER_KEYS = 128
PEER_HEADS = 8
PEER_DQ = 256
PEER_TOPK = 16
PEER_SEL = PEER_HEADS * PEER_TOPK

LANES = 128
SUBLANES = 8
VMEM_LIMIT = 56 * 1024 * 1024


def _cparams(sem):
    return pltpu.CompilerParams(dimension_semantics=sem, vmem_limit_bytes=VMEM_LIMIT)


def _bdot(a, b):
    return jnp.dot(a.astype(BF16), b.astype(BF16), preferred_element_type=F32)


def _bdot_nt(a, b):
    return lax.dot_general(a.astype(BF16), b.astype(BF16), (((1,), (1,)), ((), ())),
                           preferred_element_type=F32)


def _bdot_tn(a, b):
    return lax.dot_general(a.astype(BF16), b.astype(BF16), (((0,), (0,)), ((), ())),
                           preferred_element_type=F32)


def _sigmoid(x):
    return 1.0 / (1.0 + jnp.exp(-x))


def _silu(x):
    return x * _sigmoid(x)


def _softplus(x):
    return jnp.maximum(x, 0.0) + jnp.log1p(jnp.exp(-jnp.abs(x)))


def _norm_mod(x, g, sc, sh):
    ms = jnp.mean(x * x, axis=-1, keepdims=True)
    return (x * lax.rsqrt(ms + NORM_EPS) * g) * (1.0 + sc) + sh


def _adaln_kernel(c_ref, w_ref, b_ref, o_ref):
    cond = _silu(c_ref[...])
    o_ref[0] = _bdot(cond, w_ref[0]) + b_ref[0]


def _adaln(c, ada_w, ada_b):
    depth, d, n = ada_w.shape
    b = c.shape[0]
    tn = 1024
    return pl.pallas_call(
        _adaln_kernel,
        out_shape=jax.ShapeDtypeStruct((depth, b, n), F32),
        grid=(depth, n // tn),
        in_specs=[
            pl.BlockSpec((b, d), lambda l, j: (0, 0)),
            pl.BlockSpec((1, d, tn), lambda l, j: (l, 0, j)),
            pl.BlockSpec((1, 1, tn), lambda l, j: (l, 0, j)),
        ],
        out_specs=pl.BlockSpec((1, b, tn), lambda l, j: (l, 0, j)),
        compiler_params=_cparams(("parallel", "parallel")),
        name="adaln",
    )(c, ada_w, ada_b.reshape(depth, 1, n))


def _nm_matmul_kernel(x_ref, g_ref, sc_ref, sh_ref, w_ref, o_ref, *h_ref):
    h = _norm_mod(x_ref[...], g_ref[...], sc_ref[0], sh_ref[0])
    o_ref[...] = jnp.dot(h.astype(BF16), w_ref[...], preferred_element_type=F32).astype(o_ref.dtype)
    if h_ref:
        h_ref[0][...] = h


def _nm_matmul(x2d, g, sc, sh, w, *, rows_per_batch, tm=512, tn=None, emit_h=False):
    m, d = x2d.shape
    n = w.shape[1]
    nb = sc.shape[0]
    if tn is None:
        tn = n
    tpb = rows_per_batch // tm
    out_shape = [jax.ShapeDtypeStruct((m, n), F32)]
    out_specs = [pl.BlockSpec((tm, tn), lambda j, i: (i, j))]
    if emit_h:
        assert tn == n
        out_shape.append(jax.ShapeDtypeStruct((m, d), F32))
        out_specs.append(pl.BlockSpec((tm, d), lambda j, i: (i, 0)))
    res = pl.pallas_call(
        _nm_matmul_kernel,
        out_shape=out_shape,
        grid=(n // tn, m // tm),
        in_specs=[
            pl.BlockSpec((tm, d), lambda j, i: (i, 0)),
            pl.BlockSpec((1, d), lambda j, i: (0, 0)),
            pl.BlockSpec((1, 1, d), lambda j, i: (i // tpb, 0, 0)),
            pl.BlockSpec((1, 1, d), lambda j, i: (i // tpb, 0, 0)),
            pl.BlockSpec((d, tn), lambda j, i: (0, j)),
        ],
        out_specs=out_specs,
        compiler_params=_cparams(("parallel", "parallel")),
        name="norm_mod_matmul",
    )(x2d, g.reshape(1, d), sc.reshape(nb, 1, d), sh.reshape(nb, 1, d), w)
    return res if emit_h else res[0]


def _mm_res_kernel(*refs, has_mul):
    if has_mul:
        a_ref, m_ref, w_ref, r_ref, gt_ref, o_ref = refs
        a = a_ref[...] * m_ref[...]
    else:
        a_ref, w_ref, r_ref, gt_ref, o_ref = refs
        a = a_ref[...]
    y = jnp.dot(a.astype(BF16), w_ref[...], preferred_element_type=F32)
    o_ref[...] = r_ref[...] + gt_ref[0] * y


def _mm_res(a, w, res, gt, *, rows_per_batch, mul=None, tm=512):
    m, k = a.shape
    n = w.shape[1]
    nb = gt.shape[0]
    tpb = rows_per_batch // tm
    ins = [a]
    specs = [pl.BlockSpec((tm, k), lambda i: (i, 0))]
    if mul is not None:
        ins.append(mul)
        specs.append(pl.BlockSpec((tm, k), lambda i: (i, 0)))
    ins += [w, res, gt.reshape(nb, 1, n)]
    specs += [
        pl.BlockSpec((k, n), lambda i: (0, 0)),
        pl.BlockSpec((tm, n), lambda i: (i, 0)),
        pl.BlockSpec((1, 1, n), lambda i: (i // tpb, 0, 0)),
    ]
    return pl.pallas_call(
        functools.partial(_mm_res_kernel, has_mul=mul is not None),
        out_shape=jax.ShapeDtypeStruct((m, n), F32),
        grid=(m // tm,),
        in_specs=specs,
        out_specs=pl.BlockSpec((tm, n), lambda i: (i, 0)),
        compiler_params=_cparams(("parallel",)),
        name="matmul_residual",
    )(*ins)


def _ret_kernel(q_ref, k_ref, v_ref, gate_ref, cos_ref, sin_ref, dintra_ref, dq_ref, dk_ref, dchunk_ref,
                gng_ref, gnb_ref, o_ref, state_ref):
    H, dk, dv = RET_HEADS, RET_DK, RET_DV
    half = dk // 2

    @pl.when(pl.program_id(1) == 0)
    def _():
        state_ref[...] = jnp.zeros_like(state_ref)

    cos = cos_ref[...]
    sin = sin_ref[...]

    def rot(ref, h):
        x1 = ref[0, :, h * dk:h * dk + half]
        x2 = ref[0, :, h * dk + half:(h + 1) * dk]
        return jnp.concatenate([x1 * cos - x2 * sin, x1 * sin + x2 * cos], axis=-1)

    for h in range(H):
        q = rot(q_ref, h)
        k = rot(k_ref, h) * (dk ** -0.5)
        v = v_ref[0, :, h * dv:(h + 1) * dv]
        scores = _bdot_nt(q, k) * dintra_ref[h]
        st = state_ref[h]
        o = _bdot(scores, v) + _bdot(q, st) * dq_ref[h]
        state_ref[h] = st * dchunk_ref[h] + _bdot_tn(k * dk_ref[h], v)
        mu = jnp.mean(o, axis=-1, keepdims=True)
        var = jnp.mean(jnp.square(o - mu), axis=-1, keepdims=True)
        on = (o - mu) * lax.rsqrt(var + RET_GN_EPS) * gng_ref[h] + gnb_ref[h]
        g = gate_ref[0, :, h * dv:(h + 1) * dv]
        o_ref[0, :, h * dv:(h + 1) * dv] = (_silu(g) * on).astype(o_ref.dtype)


def _retention_scan(proj, gn_g, gn_b):
    B, S, _ = proj.shape
    H, dk, dv, C = RET_HEADS, RET_DK, RET_DV, RET_CHUNK
    half = dk // 2
    N = S // C
    inv_freq = RET_ROPE_BASE ** (-jnp.arange(half, dtype=F32) / half)
    ang = jnp.arange(S, dtype=F32)[:, None] * inv_freq[None, :]
    cos, sin = jnp.cos(ang), jnp.sin(ang)
    log_gamma = jnp.log1p(-jnp.exp2(-5.0 - jnp.arange(H, dtype=F32)))
    idx = jnp.arange(C, dtype=F32)
    diff = idx[:, None] - idx[None, :]
    causal = diff >= 0
    d_intra = jnp.where(causal[None], jnp.exp(jnp.where(causal, diff, 0.0)[None] * log_gamma[:, None, None]), 0.0)
    d_q = jnp.exp((idx[None, :] + 1.0) * log_gamma[:, None])[:, :, None]
    d_k = jnp.exp((C - 1.0 - idx)[None, :] * log_gamma[:, None])[:, :, None]
    d_chunk = jnp.exp(C * log_gamma)[:, None, None]
    qw, vw = H * dk, H * dv
    return pl.pallas_call(
        _ret_kernel,
        out_shape=jax.ShapeDtypeStruct((B, S, vw), BF16),
        grid=(B, N),
        in_specs=[
            pl.BlockSpec((1, C, qw), lambda b, n: (b, n, 0)),
            pl.BlockSpec((1, C, qw), lambda b, n: (b, n, 1)),
            pl.BlockSpec((1, C, vw), lambda b, n: (b, n, 1)),
            pl.BlockSpec((1, C, vw), lambda b, n: (b, n, 2)),
            pl.BlockSpec((C, half), lambda b, n: (n, 0)),
            pl.BlockSpec((C, half), lambda b, n: (n, 0)),
            pl.BlockSpec((H, C, C), lambda b, n: (0, 0, 0)),
            pl.BlockSpec((H, C, 1), lambda b, n: (0, 0, 0)),
            pl.BlockSpec((H, C, 1), lambda b, n: (0, 0, 0)),
            pl.BlockSpec((H, 1, 1), lambda b, n: (0, 0, 0)),
            pl.BlockSpec((H, 1, dv), lambda b, n: (0, 0, 0)),
            pl.BlockSpec((H, 1, dv), lambda b, n: (0, 0, 0)),
        ],
        out_specs=pl.BlockSpec((1, C, vw), lambda b, n: (b, n, 0)),
        scratch_shapes=[pltpu.VMEM((H, dk, dv), F32)],
        compiler_params=_cparams(("parallel", "arbitrary")),
        name="retention_scan",
    )(proj, proj, proj, proj, cos, sin, d_intra, d_q, d_k, d_chunk,
      gn_g.reshape(H, 1, dv), gn_b.reshape(H, 1, dv))


def _shift_rows(cur, prev8, s):
    rows = lax.broadcasted_iota(jnp.int32, cur.shape, 0)
    rolled = pltpu.roll(cur, s, axis=0)
    head = pltpu.roll(prev8, s, axis=0)
    head = jnp.concatenate([head, jnp.zeros((cur.shape[0] - SUBLANES, cur.shape[1]), cur.dtype)], axis=0)
    return jnp.where(rows < s, head, rolled)


def _cumsum_rows(x):
    rows = lax.broadcasted_iota(jnp.int32, x.shape, 0)
    s = 1
    while s < x.shape[0]:
        x = x + jnp.where(rows >= s, pltpu.roll(x, s, axis=0), 0.0)
        s *= 2
    return x


def _gdn_kernel(qkv_ref, gate_ref, ab_ref, cw_ref, alog_ref, dtb_ref, ng_ref, o_ref, state_ref, prev_ref):
    C, dk, H = GDN_CHUNK, GDN_DK, GDN_HEADS

    @pl.when(pl.program_id(1) == 0)
    def _():
        state_ref[...] = jnp.zeros_like(state_ref)
        prev_ref[...] = jnp.zeros_like(prev_ref)

    def conv_silu(col):
        cur = qkv_ref[0, :, col * dk:(col + 1) * dk]
        prev8 = prev_ref[:, col * dk:(col + 1) * dk]
        cw = cw_ref[:, col * dk:(col + 1) * dk]
        acc = cur * cw[GDN_CONV - 1:GDN_CONV]
        for s in range(1, GDN_CONV):
            acc = acc + _shift_rows(cur, prev8, s) * cw[GDN_CONV - 1 - s:GDN_CONV - s]
        prev_ref[:, col * dk:(col + 1) * dk] = cur[C - SUBLANES:]
        return _silu(acc)

    ab = ab_ref[0]
    g_all = -jnp.exp(alog_ref[...]) * _softplus(ab + dtb_ref[...])
    beta_all = _sigmoid(ab)
    ri = lax.broadcasted_iota(jnp.int32, (C, C), 0)
    ci = lax.broadcasted_iota(jnp.int32, (C, C), 1)
    incl = ri >= ci

    bdot_nt = lambda x, y: jnp.einsum('hid,hjd->hij', x.astype(BF16), y.astype(BF16), preferred_element_type=F32)
    bdot = lambda x, y: jnp.einsum('hij,hjk->hik', x, y, preferred_element_type=F32)
    stack = lambda xs: jnp.stack(xs, axis=0)

    def hdot(x, y):
        xh, yh = x.astype(BF16), y.astype(BF16)
        xl, yl = (x - xh.astype(F32)).astype(BF16), (y - yh.astype(F32)).astype(BF16)
        return bdot(xh, yh) + (bdot(xh, yl) + bdot(xl, yh))

    qs, ks, vs, betas, cums = [], [], [], [], []
    for h in range(H):
        q = conv_silu(h)
        k = conv_silu(H + h)
        qs.append(q * lax.rsqrt(jnp.sum(q * q, axis=-1, keepdims=True) + 1e-6) * (dk ** -0.5))
        ks.append(k * lax.rsqrt(jnp.sum(k * k, axis=-1, keepdims=True) + 1e-6))
        vs.append(conv_silu(2 * H + h))
        betas.append(beta_all[:, H + h:H + h + 1])
        cums.append(_cumsum_rows(jnp.broadcast_to(g_all[:, h:h + 1], (C, LANES))))
    q, k, v, beta, cum = stack(qs), stack(ks), stack(vs), stack(betas), stack(cums)
    cum_c = cum[:, :, :1]
    cum_last = cum[:, C - 1:C, :1]
    cum_r = stack([cums[h].T[:C, :] for h in range(H)])
    decay = jnp.where(incl, jnp.exp(jnp.where(incl, cum[:, :, :C] - cum_r, 0.0)), 0.0)

    L = jnp.where(ri > ci, bdot_nt(k, k) * decay, 0.0) * beta
    rhs = jnp.concatenate([k * (beta * jnp.exp(cum_c)), v * beta], axis=-1)
    eye = (ri == ci).astype(F32)
    p = -L
    inv = eye + p
    s = 2
    while s < C:
        p = hdot(p, p)
        inv = inv + hdot(inv, p)
        s *= 2
    sol = hdot(inv, rhs)
    a_qk = bdot_nt(q, k) * decay
    q_dec = q * jnp.exp(cum_c)
    k_dec = k * jnp.exp(cum_last - cum_c)
    e_last = jnp.exp(cum_last)

    for h in range(H):
        st = state_ref[h]
        u = sol[h, :, dk:] - _bdot(sol[h, :, :dk], st)
        o = _bdot(q_dec[h], st) + _bdot(a_qk[h], u)
        state_ref[h] = st * e_last[h] + _bdot_tn(k_dec[h], u)
        ms = jnp.mean(o * o, axis=-1, keepdims=True)
        o = o * lax.rsqrt(ms + NORM_EPS) * ng_ref[...]
        o_ref[0, :, h * dk:(h + 1) * dk] = (o * _silu(gate_ref[0, :, h * dk:(h + 1) * dk])).astype(o_ref.dtype)


def _gdn_scan(proj, proj_ab, conv_w, a_log, dt_bias, norm_g):
    B, S, _ = proj.shape
    H, dk, C = GDN_HEADS, GDN_DK, GDN_CHUNK
    N = S // C
    pad = lambda t: jnp.pad(t.astype(F32), (0, LANES - H)).reshape(1, LANES)
    row = pl.BlockSpec((1, LANES), lambda b, n: (0, 0))
    return pl.pallas_call(
        _gdn_kernel,
        out_shape=jax.ShapeDtypeStruct((B, S, H * dk), BF16),
        grid=(B, N),
        in_specs=[pl.BlockSpec((1, C, 3 * H * dk), lambda b, n: (b, n, 0)),
                  pl.BlockSpec((1, C, H * dk), lambda b, n: (b, n, 3)),
                  pl.BlockSpec((1, C, LANES), lambda b, n: (b, n, 0)),
                  pl.BlockSpec((GDN_CONV, 3 * H * dk), lambda b, n: (0, 0)),
                  row, row, row],
        out_specs=pl.BlockSpec((1, C, H * dk), lambda b, n: (b, n, 0)),
        scratch_shapes=[pltpu.VMEM((H, dk, dk), F32), pltpu.VMEM((SUBLANES, 3 * H * dk), F32)],
        compiler_params=_cparams(("parallel", "arbitrary")),
        name="gdn_scan",
    )(proj, proj, proj_ab, conv_w, pad(a_log), pad(dt_bias), norm_g.reshape(1, dk))


def _rwkv_proj_kernel(x_ref, xp_ref, g_ref, sc_ref, sh_ref, mu_ref, wr_ref, wk_ref, wv_ref, w1_ref, w2_ref,
                      a1_ref, a2_ref, g1_ref, g2_ref, w0_ref, a0_ref, kk_ref, ka_ref,
                      r_o, dec_o, k_o, v_o, kk_o, a_o, g_o, *, tiles_per_seq):
    h = _norm_mod(x_ref[...], g_ref[...], sc_ref[0], sh_ref[0])
    hp8 = _norm_mod(xp_ref[...], g_ref[...], sc_ref[0], sh_ref[0])
    seq_start = pl.program_id(0) % tiles_per_seq == 0
    first = jnp.where(seq_start, 0.0, hp8[SUBLANES - 1:SUBLANES, :])
    rows = lax.broadcasted_iota(jnp.int32, h.shape, 0)
    xx = jnp.where(rows == 0, first, pltpu.roll(h, 1, axis=0)) - h
    mix = lambda j: h + xx * mu_ref[j:j + 1, :]
    r = _bdot(mix(0), wr_ref[...])
    lw = w0_ref[...] + _bdot(jnp.tanh(_bdot(mix(1), w1_ref[...])), w2_ref[...])
    k = _bdot(mix(2), wk_ref[...])
    v = _bdot(mix(3), wv_ref[...])
    a = _sigmoid(a0_ref[...] + _bdot(_bdot(mix(4), a1_ref[...]), a2_ref[...]))
    g = _bdot(_sigmoid(_bdot(mix(5), g1_ref[...])), g2_ref[...])
    w = -_softplus(-lw) - 0.5
    r_o[...] = r
    dec_o[...] = jnp.exp(-jnp.exp(w))
    k_o[...] = k * (1.0 + (a - 1.0) * ka_ref[...])
    v_o[...] = v
    kk_o[...] = k * kk_ref[...]
    a_o[...] = a
    g_o[...] = g


def _pad_cols(w, n):
    return jnp.pad(w, ((0, 0), (0, n - w.shape[1])))


def _pad_rows(w, n):
    return jnp.pad(w, ((0, n - w.shape[0]), (0, 0)))


def _rwkv_proj(x2d, g, sc, sh, mu, w_r, w_k, w_v, w1, w2, a1, a2, g1, g2, w0, a0, k_k, k_a, *, rows_per_batch, tm=256):
    m, d = x2d.shape
    nb = sc.shape[0]
    tpb = rows_per_batch // tm
    lora_w = LANES * pl.cdiv(w1.shape[1], LANES)
    lora_g = LANES * pl.cdiv(g1.shape[1], LANES)
    bf = lambda t: t.astype(BF16)
    full = lambda a: pl.BlockSpec(a.shape, lambda i: (0,) * a.ndim)
    row = lambda t: t.reshape(1, d)
    ws = [bf(w_r), bf(w_k), bf(w_v), bf(_pad_cols(w1, lora_w)), bf(_pad_rows(w2, lora_w)),
          bf(_pad_cols(a1, lora_w)), bf(_pad_rows(a2, lora_w)), bf(_pad_cols(g1, lora_g)), bf(_pad_rows(g2, lora_g)),
          row(w0), row(a0), row(k_k), row(k_a)]
    tile = pl.BlockSpec((tm, d), lambda i: (i, 0))
    return pl.pallas_call(
        functools.partial(_rwkv_proj_kernel, tiles_per_seq=tpb),
        out_shape=[jax.ShapeDtypeStruct((m, d), F32)] * 7,
        grid=(m // tm,),
        in_specs=[
            tile,
            pl.BlockSpec((SUBLANES, d), lambda i: (jnp.maximum(i * (tm // SUBLANES) - 1, 0), 0)),
            pl.BlockSpec((1, d), lambda i: (0, 0)),
            pl.BlockSpec((1, 1, d), lambda i: (i // tpb, 0, 0)),
            pl.BlockSpec((1, 1, d), lambda i: (i // tpb, 0, 0)),
            full(mu),
        ] + [full(w) for w in ws],
        out_specs=[tile] * 7,
        compiler_params=_cparams(("parallel",)),
        name="rwkv_proj",
    )(x2d, x2d, g.reshape(1, d), sc.reshape(nb, 1, d), sh.reshape(nb, 1, d), mu, *ws)


RWKV_VUNROLL = 64


def _rwkv_scan_kernel(r_ref, w_ref, k_ref, v_ref, kk_ref, a_ref, rk_ref, lng_ref, lnb_ref, y_ref, state_ref, yrow_ref):
    n = RWKV_HEAD

    @pl.when(pl.program_id(0) == 0)
    def _():
        state_ref[...] = jnp.zeros_like(state_ref)

    def step(t, carry):
        r, w, k, kkr, a = r_ref[t], w_ref[t], k_ref[t], kk_ref[t], a_ref[t]
        kk = kkr * lax.rsqrt(jnp.sum(kkr * kkr, axis=0, keepdims=True) + 1e-6)
        nkk = -kk
        kka = kk * a

        def vloop(vb, c):
            for j in range(RWKV_VUNROLL):
                vi = vb * RWKV_VUNROLL + j
                sv = state_ref[vi]
                sa = jnp.sum(sv * nkk, axis=0, keepdims=True)
                vrow = v_ref[t, pl.ds(vi, 1), :]
                sn = sv * w + sa * kka + vrow * k
                state_ref[vi] = sn
                yrow_ref[pl.ds(vi, 1), :] = jnp.sum(sn * r, axis=0, keepdims=True)
            return c

        lax.fori_loop(0, n // RWKV_VUNROLL, vloop, 0)
        y = yrow_ref[...]
        mu = jnp.mean(y, axis=0, keepdims=True)
        var = jnp.mean(jnp.square(y - mu), axis=0, keepdims=True)
        yn = (y - mu) * lax.rsqrt(var + RWKV_GN_EPS) * lng_ref[...] + lnb_ref[...]
        bonus = jnp.sum(r * k * rk_ref[...], axis=0, keepdims=True)
        y_ref[t] = yn + bonus * v_ref[t]
        return carry

    lax.fori_loop(0, r_ref.shape[0], step, 0)


def _rwkv_scan(r, dec, k, v, kk, a, r_k, ln_g, ln_b, *, tc=32):
    B, S, D = r.shape
    H, n = RWKV_HEADS, RWKV_HEAD
    lanes = B * H
    to_scan = lambda t: jnp.transpose(t.reshape(B, S, H, n), (1, 3, 0, 2)).reshape(S, n, lanes)
    per_head = lambda p: jnp.tile(p.T, (1, B))
    blk = pl.BlockSpec((tc, n, lanes), lambda i: (i, 0, 0))
    cst = pl.BlockSpec((n, lanes), lambda i: (0, 0))
    y = pl.pallas_call(
        _rwkv_scan_kernel,
        out_shape=jax.ShapeDtypeStruct((S, n, lanes), F32),
        grid=(S // tc,),
        in_specs=[blk] * 6 + [cst] * 3,
        out_specs=blk,
        scratch_shapes=[pltpu.VMEM((n, n, lanes), F32), pltpu.VMEM((n, lanes), F32)],
        compiler_params=_cparams(("arbitrary",)),
        name="rwkv_scan",
    )(to_scan(r), to_scan(dec), to_scan(k), to_scan(v), to_scan(kk), to_scan(a),
      per_head(r_k), per_head(ln_g), per_head(ln_b))
    return jnp.transpose(y.reshape(S, n, B, H), (2, 0, 3, 1)).reshape(B, S, D)


def _rwkv_mixer(x, g, sc, sh, gt, mu, w_r, w_k, w_v, w_o, w0, w1, w2, a0, a1, a2, g1, g2, k_k, k_a, r_k, ln_g, ln_b):
    B, S, D = x.shape
    x2d = x.reshape(B * S, D)
    tm = min(256, S)
    r, dec, k, v, kk, a, gg = _rwkv_proj(x2d, g, sc, sh, mu, w_r, w_k, w_v, w1, w2, a1, a2, g1, g2, w0, a0, k_k, k_a,
                                         rows_per_batch=S, tm=tm)
    sh3 = lambda t: t.reshape(B, S, D)
    y = _rwkv_scan(sh3(r), sh3(dec), sh3(k), sh3(v), sh3(kk), sh3(a), r_k, ln_g, ln_b, tc=min(32, S))
    out = _mm_res(y.reshape(B * S, D), w_o.astype(BF16), x2d, gt, rows_per_batch=S, mul=gg, tm=min(512, S))
    return out.reshape(B, S, D)


def _rwkv_mixer_test(x, g, sc, sh, p):
    gt = jnp.ones_like(sc)
    return _rwkv_mixer(x, g, sc, sh, gt, p['mu'], p['w_r'], p['w_k'], p['w_v'], p['w_o'], p['w0'], p['w1'], p['w2'],
                       p['a0'], p['a1'], p['a2'], p['g1'], p['g2'], p['k_k'], p['k_a'], p['r_k'], p['ln_g'],
                       p['ln_b']) - x


def _topk_rows(s, k, rows=None):
    if rows is None:
        rows = lax.broadcasted_iota(jnp.int32, s.shape, 0)
    n = jnp.iinfo(jnp.int32).max
    vals, ids = [], []
    for _ in range(k):
        m = jnp.max(s, axis=0, keepdims=True)
        idx = jnp.min(jnp.where(s == m, rows, n), axis=0, keepdims=True)
        vals.append(m)
        ids.append(idx)
        s = jnp.where(rows == idx, -jnp.inf, s)
    return jnp.concatenate(vals, axis=0), jnp.concatenate(ids, axis=0)


def _take_rows(table, pos):
    out = jnp.zeros(pos.shape, table.dtype)
    for m in range(table.shape[0]):
        out = jnp.where(pos == m, table[m:m + 1, :], out)
    return out


_PEER_CAND = [(i, PEER_TOPK // (i + 1)) for i in range(PEER_TOPK)]
_PEER_NCAND = sum(n for _, n in _PEER_CAND)
_PEER_NCAND_PAD = SUBLANES * pl.cdiv(_PEER_NCAND, SUBLANES)


def _peer_cand_codes():
    codes = [i * PEER_TOPK + j for i, n in _PEER_CAND for j in range(n)]
    codes += [PEER_TOPK * PEER_TOPK + p for p in range(_PEER_NCAND_PAD - _PEER_NCAND)]
    return jnp.broadcast_to(jnp.asarray(codes, jnp.int32)[:, None], (_PEER_NCAND_PAD, LANES))


def _peer_route_kernel(q_ref, keys_ref, codes_ref, idx_o, gate_o, idx_tok_o, gate_tok_o):
    K, half = PEER_TOPK, PEER_DQ // 2

    tm = q_ref.shape[0]
    G = 2
    codes = jnp.concatenate([codes_ref[...]] * (G * tm // LANES), axis=1)
    pad = jnp.full((_PEER_NCAND_PAD - _PEER_NCAND, tm), -jnp.inf, F32)

    def group(hg, carry):
        ss = []
        for dh in range(G):
            for p in range(2):
                c = pl.multiple_of((2 * (G * hg + dh) + p) * half, half)
                ss.append(_bdot_nt(keys_ref[G * hg + dh, p], q_ref[:, pl.ds(c, half)]))
        vals, ids = _topk_rows(jnp.concatenate(ss, axis=1), K)
        part = lambda t, j: t[:, j * tm:(j + 1) * tm]
        cands = []
        for dh in range(G):
            va, vb = part(vals, 2 * dh), part(vals, 2 * dh + 1)
            cands.append(jnp.concatenate([va[i:i + 1, :] + vb[:n, :] for i, n in _PEER_CAND] + [pad], axis=0))
        best, pos = _topk_rows(jnp.concatenate(cands, axis=1), K, codes)
        for dh in range(G):
            ia, ib = part(ids, 2 * dh), part(ids, 2 * dh + 1)
            ps, bs = part(pos, dh), part(best, dh)
            expert = _take_rows(ia, ps // K) * PEER_KEYS + _take_rows(ib, ps % K)
            e = jnp.exp(bs - bs[0:1, :])
            r0 = pl.multiple_of((G * hg + dh) * K, K)
            idx_o[pl.ds(r0, K), :] = expert
            gate_o[pl.ds(r0, K), :] = e / jnp.sum(e, axis=0, keepdims=True)
        return carry

    lax.fori_loop(0, PEER_HEADS // G, group, 0)
    idx_tok_o[...] = idx_o[...].T
    gate_tok_o[...] = gate_o[...].T


def _peer_route(q, sub_keys, *, tm=256):
    t, n = q.shape
    blk = pl.BlockSpec((PEER_SEL, tm), lambda i: (0, i))
    tok = pl.BlockSpec((tm, PEER_SEL), lambda i: (i, 0))
    return pl.pallas_call(
        _peer_route_kernel,
        out_shape=[jax.ShapeDtypeStruct((PEER_SEL, t), jnp.int32), jax.ShapeDtypeStruct((PEER_SEL, t), F32),
                   jax.ShapeDtypeStruct((t, PEER_SEL), jnp.int32), jax.ShapeDtypeStruct((t, PEER_SEL), F32)],
        grid=(t // tm,),
        in_specs=[pl.BlockSpec((tm, n), lambda i: (i, 0)),
                  pl.BlockSpec(sub_keys.shape, lambda i: (0, 0, 0, 0)),
                  pl.BlockSpec((_PEER_NCAND_PAD, LANES), lambda i: (0, 0))],
        out_specs=[blk, blk, tok, tok],
        compiler_params=_cparams(("parallel",)),
        name="peer_route",
    )(q, sub_keys.astype(BF16), _peer_cand_codes())


def _pack_kernel(u_ref, v_ref, uv_o, u2_o, v2_o):
    half = u_ref.shape[1] // 2
    high = jnp.int32(-65536)
    bits = lambda x: lax.bitcast_convert_type(x.astype(BF16).astype(F32), jnp.int32)
    low = lambda b: lax.shift_right_logical(b, 16)
    ub, vb = bits(u_ref[...]), bits(v_ref[...])
    uv_o[...] = (vb & high) | low(ub)
    u2_o[...] = (ub[:, half:] & high) | low(ub[:, :half])
    v2_o[...] = (vb[:, half:] & high) | low(vb[:, :half])


def _pack_tables(u, v, *, tm=512):
    e, d = u.shape
    tile = lambda w: pl.BlockSpec((tm, w), lambda i: (i, 0))
    uv, u2, v2 = pl.pallas_call(
        _pack_kernel,
        out_shape=[jax.ShapeDtypeStruct((e, d), jnp.int32), jax.ShapeDtypeStruct((e, d // 2), jnp.int32),
                   jax.ShapeDtypeStruct((e, d // 2), jnp.int32)],
        grid=(e // tm,),
        in_specs=[tile(d), tile(d)],
        out_specs=[tile(d), tile(d // 2), tile(d // 2)],
        compiler_params=_cparams(("parallel",)),
        name="peer_pack",
    )(u, v)
    return uv.reshape(e, d // LANES, LANES), u2, v2


PEER_NBUF = 8


def _gelu(x):
    return 0.5 * x * (1.0 + lax.erf(x * (2.0 ** -0.5)))


def _peer_eval(packed, x, gate):
    u = lax.bitcast_convert_type(packed << 16, F32)
    v = lax.bitcast_convert_type(packed & jnp.int32(-65536), F32)
    act = jnp.sum(jnp.sum(u * x[None], axis=1), axis=1, keepdims=True)
    wgt = gate * _gelu(act)
    return jnp.sum(v * wgt[:, :, None], axis=0)


def _gate_column(gate_ref, t):
    lane = lax.broadcasted_iota(jnp.int32, gate_ref.shape, 1)
    return jnp.sum(jnp.where(lane == t, gate_ref[...], 0.0), axis=1, keepdims=True)


def _peer_expert_kernel(idx_hbm, gate_ref, h_ref, xres_ref, gt_ref, uv_hbm, *rest, tok0):
    o_ref, idx_smem, buf, sem_idx, sem = rest[-5:]
    tb = h_ref.shape[0]
    nsel = PEER_SEL
    base = (pl.program_id(0) * tb + tok0) * nsel
    cp = pltpu.make_async_copy(idx_hbm.at[pl.ds(base, tb * nsel)], idx_smem, sem_idx)
    cp.start()
    cp.wait()

    def issue(t, slot):
        for k in range(nsel):
            e = idx_smem[t * nsel + k]
            pltpu.make_async_copy(uv_hbm.at[e], buf.at[slot, k], sem.at[slot]).start(priority=k % 2)

    def wait(slot):
        pltpu.make_async_copy(uv_hbm.at[pl.ds(0, nsel)], buf.at[slot], sem.at[slot]).wait()

    for t0 in range(PEER_NBUF - 1):
        issue(t0, t0)

    def body(t, carry):
        slot = t % PEER_NBUF
        nxt = t + PEER_NBUF - 1

        @pl.when(nxt < tb)
        def _():
            issue(nxt, nxt % PEER_NBUF)

        wait(slot)
        out = _peer_eval(buf[slot], h_ref[t], _gate_column(gate_ref, t))
        o_ref[t] = xres_ref[t] + gt_ref[0] * out
        return carry

    lax.fori_loop(0, tb, body, 0)


def _peer_expert(idx_flat, gate_t, h3, xres3, gt3, uv, *, rows_per_batch, tok0, ntok, hx0, tb=128, after=None):
    _, c, _ = h3.shape
    tpb = rows_per_batch // tb
    b0 = tok0 // tb
    hb0 = (tok0 - hx0) // tb
    tok = pl.BlockSpec((tb, c, LANES), lambda i: (i + hb0, 0, 0))
    return pl.pallas_call(
        functools.partial(_peer_expert_kernel, tok0=tok0),
        out_shape=jax.ShapeDtypeStruct((ntok, c, LANES), F32),
        grid=(ntok // tb,),
        in_specs=[
            pl.BlockSpec(memory_space=pl.ANY),
            pl.BlockSpec((PEER_SEL, tb), lambda i: (0, i + b0)),
            tok, tok,
            pl.BlockSpec((1, c, LANES), lambda i: ((i + b0) // tpb, 0, 0)),
            pl.BlockSpec(memory_space=pl.ANY),
        ] + ([] if after is None else [pl.BlockSpec(memory_space=pl.ANY)]),
        out_specs=pl.BlockSpec((tb, c, LANES), lambda i: (i, 0, 0)),
        scratch_shapes=[
            pltpu.SMEM((tb * PEER_SEL,), jnp.int32),
            pltpu.VMEM((PEER_NBUF, PEER_SEL, c, LANES), jnp.int32),
            pltpu.SemaphoreType.DMA,
            pltpu.SemaphoreType.DMA((PEER_NBUF,)),
        ],
        compiler_params=_cparams(("arbitrary",)),
        name="peer_expert",
    )(idx_flat, gate_t, h3, xres3, gt3, uv, *(() if after is None else (after,)))


SC_CORES = 2
SC_SUBCORES = 16
SC_WORKERS = SC_CORES * SC_SUBCORES
SC_LANES = 16
SC_TOKENS = 8
SC_GATHER_ROWS = 32
SC_CHUNKS = 8
_SC_PARAMS = pltpu.CompilerParams(needs_layout_passes=False)


def _sc_unpack(w):
    return (lax.bitcast_convert_type(w << 16, F32), lax.bitcast_convert_type(w & jnp.int32(-65536), F32))


def _sc_mesh():
    return plsc.VectorSubcoreMesh(core_axis_name="core", subcore_axis_name="subcore")


def _sc_act(u2, idx_flat, h_flat, *, tok0, ntok, d):
    nsel, L, G, CG, TBK = PEER_SEL, SC_LANES, SC_GATHER_ROWS, SC_CHUNKS, SC_TOKENS
    half = d // 2
    tpw = ntok // SC_WORKERS
    ng = nsel // G
    ngt = TBK * ng

    @functools.partial(
        pl.kernel, out_type=jax.ShapeDtypeStruct((ntok * nsel,), F32), mesh=_sc_mesh(),
        scratch_types=[pltpu.VMEM((TBK * nsel,), jnp.int32), pltpu.VMEM((TBK * d,), F32),
                       pltpu.VMEM((2, G, half), jnp.int32), pltpu.VMEM((TBK * nsel,), F32),
                       pltpu.VMEM((G * L,), F32), pltpu.SemaphoreType.DMA((2,))],
        compiler_params=_SC_PARAMS, name="peer_sc_act")
    def k(u_hbm, i_hbm, x_hbm, o_hbm, idx_v, x_v, buf, act_v, acc_v, sem):
        wid = lax.axis_index("core") * SC_SUBCORES + lax.axis_index("subcore")
        lanes = lax.iota(jnp.int32, L)

        def gather(gi, b):
            return pltpu.make_async_copy(u_hbm.at[idx_v.at[pl.ds(gi * G, G)]], buf.at[b], sem.at[b])

        @pl.loop(0, tpw // TBK)
        def _(bi):
            tl = wid * tpw + bi * TBK
            pltpu.sync_copy(i_hbm.at[pl.ds((tok0 + tl) * nsel, TBK * nsel)], idx_v)
            pltpu.sync_copy(x_hbm.at[pl.ds((tok0 + tl) * d, TBK * d)], x_v)
            gather(0, 0).start()

            @pl.loop(0, ngt, step=2)
            def _(g0):
                for b in range(2):
                    gi = g0 + b

                    @pl.when(gi + 1 < ngt)
                    def _():
                        gather(gi + 1, 1 - b).start()

                    gather(gi, b).wait()
                    xbase = (gi // ng) * d
                    for jg in range(half // (CG * L)):
                        xl = [x_v[pl.ds(pl.multiple_of(xbase + (jg * CG + c) * L, L), L)] for c in range(CG)]
                        xh = [x_v[pl.ds(pl.multiple_of(xbase + half + (jg * CG + c) * L, L), L)] for c in range(CG)]

                        def row(r, carry):
                            ps = []
                            for c in range(CG):
                                lo, hi = _sc_unpack(buf[b, r, pl.ds((jg * CG + c) * L, L)])
                                ps.append(lo * xl[c] + hi * xh[c])
                            while len(ps) > 1:
                                ps = [ps[i] + ps[i + 1] for i in range(0, len(ps), 2)]
                            off = pl.multiple_of(r * L, L)
                            if jg == 0:
                                acc_v[pl.ds(off, L)] = ps[0]
                            else:
                                acc_v[pl.ds(off, L)] = acc_v[pl.ds(off, L)] + ps[0]
                            return carry

                        plsc.parallel_loop(0, G, carry=jnp.int32(0))(row)
                    for part in range(G // L):
                        vec = jnp.zeros((L,), F32)
                        for r in range(L):
                            vec = jnp.where(lanes == r, jnp.sum(acc_v[pl.ds((part * L + r) * L, L)]), vec)
                        act_v[pl.ds(pl.multiple_of(gi * G + part * L, L), L)] = vec

            pltpu.sync_copy(act_v, o_hbm.at[pl.ds(tl * nsel, TBK * nsel)])

    return k(u2, idx_flat, h_flat)


def _sc_out(v2, idx_flat, w_flat, xres_flat, gt_flat, *, tok0, ntok, d, rows_per_batch):
    nsel, L, G, CG, TBK = PEER_SEL, SC_LANES, SC_GATHER_ROWS, SC_CHUNKS, SC_TOKENS
    half = d // 2
    tpw = ntok // SC_WORKERS
    ng = nsel // G
    ngt = TBK * ng

    @functools.partial(
        pl.kernel, out_type=jax.ShapeDtypeStruct((ntok * d,), F32), mesh=_sc_mesh(),
        scratch_types=[pltpu.VMEM((TBK * nsel,), jnp.int32), pltpu.VMEM((TBK * nsel,), F32),
                       pltpu.VMEM((2, G, half), jnp.int32), pltpu.VMEM((TBK * d,), F32),
                       pltpu.VMEM((TBK * d,), F32), pltpu.VMEM((d,), F32), pltpu.SemaphoreType.DMA((2,))],
        compiler_params=_SC_PARAMS, name="peer_sc_out")
    def k(v_hbm, i_hbm, w_hbm, xr_hbm, gt_hbm, o_hbm, idx_v, w_v, buf, out_v, xr_v, gt_v, sem):
        wid = lax.axis_index("core") * SC_SUBCORES + lax.axis_index("subcore")

        def gather(gi, b):
            return pltpu.make_async_copy(v_hbm.at[idx_v.at[pl.ds(gi * G, G)]], buf.at[b], sem.at[b])

        @pl.loop(0, tpw // TBK)
        def _(bi):
            tl = wid * tpw + bi * TBK
            pltpu.sync_copy(i_hbm.at[pl.ds((tok0 + tl) * nsel, TBK * nsel)], idx_v)
            pltpu.sync_copy(w_hbm.at[pl.ds(tl * nsel, TBK * nsel)], w_v)
            gather(0, 0).start()
            pltpu.sync_copy(xr_hbm.at[pl.ds((tok0 + tl) * d, TBK * d)], xr_v)
            pltpu.sync_copy(gt_hbm.at[pl.ds(((tok0 + tl) // rows_per_batch) * d, d)], gt_v)

            @pl.loop(0, TBK * d // L)
            def _(i):
                out_v[pl.ds(pl.multiple_of(i * L, L), L)] = jnp.zeros((L,), F32)

            @pl.loop(0, ngt, step=2)
            def _(g0):
                for b in range(2):
                    gi = g0 + b

                    @pl.when(gi + 1 < ngt)
                    def _():
                        gather(gi + 1, 1 - b).start()

                    gather(gi, b).wait()
                    obase = (gi // ng) * d
                    for jg in range(half // (CG * L)):
                        lo0 = obase + jg * CG * L
                        hi0 = lo0 + half
                        accs = tuple(out_v[pl.ds(pl.multiple_of(lo0 + c * L, L), L)] for c in range(CG)) + \
                            tuple(out_v[pl.ds(pl.multiple_of(hi0 + c * L, L), L)] for c in range(CG))

                        def row(r, accs):
                            wk = plsc.load_gather(w_v, [jnp.full((L,), gi * G + r, jnp.int32)])
                            new = list(accs)
                            for c in range(CG):
                                lo, hi = _sc_unpack(buf[b, r, pl.ds((jg * CG + c) * L, L)])
                                new[c] = new[c] + wk * lo
                                new[CG + c] = new[CG + c] + wk * hi
                            return tuple(new)

                        accs = lax.fori_loop(0, G, row, accs)
                        for c in range(CG):
                            out_v[pl.ds(pl.multiple_of(lo0 + c * L, L), L)] = accs[c]
                            out_v[pl.ds(pl.multiple_of(hi0 + c * L, L), L)] = accs[CG + c]

            @pl.loop(0, TBK * d // L)
            def _(i):
                off = pl.multiple_of(i * L, L)
                goff = pl.multiple_of((i % (d // L)) * L, L)
                out_v[pl.ds(off, L)] = xr_v[pl.ds(off, L)] + gt_v[pl.ds(goff, L)] * out_v[pl.ds(off, L)]

            pltpu.sync_copy(out_v, o_hbm.at[pl.ds(tl * d, TBK * d)])

    return k(v2, idx_flat, w_flat, xres_flat, gt_flat)


def _peer_weight_kernel(act_ref, gate_ref, *rest):
    rest[-1][...] = gate_ref[...] * _gelu(act_ref[...])


def _peer_weight(act, gate_tok, *, tok0, tm=1024, after=None):
    n, k = act.shape
    tm = math.gcd(math.gcd(n, tok0), tm)
    assert tm % SUBLANES == 0
    b0 = tok0 // tm
    return pl.pallas_call(
        _peer_weight_kernel,
        out_shape=jax.ShapeDtypeStruct((n, k), F32),
        grid=(n // tm,),
        in_specs=[pl.BlockSpec((tm, k), lambda i: (i, 0)), pl.BlockSpec((tm, k), lambda i: (i + b0, 0))]
        + ([] if after is None else [pl.BlockSpec(memory_space=pl.ANY)]),
        out_specs=pl.BlockSpec((tm, k), lambda i: (i, 0)),
        compiler_params=_cparams(("parallel",)),
        name="peer_weight",
    )(act, gate_tok, *(() if after is None else (after,)))


PEER_SC_SHARE = (19, 32)
PEER_SC_CALLS = 2
PEER_TC_SPLIT = (1, 1, 2)


def _peer_experts(idx_flat, gate_t, gate_tok, h, xres, gt, uv, u2, v2, *, rows_per_batch):
    t, d = h.shape
    c = d // LANES
    tb = min(128, rows_per_batch)
    unit = SC_WORKERS * SC_TOKENS * PEER_SC_CALLS
    n_sc = t * PEER_SC_SHARE[0] // PEER_SC_SHARE[1] // unit * unit
    ch = n_sc // PEER_SC_CALLS
    h3, xres3 = h[n_sc:].reshape(t - n_sc, c, LANES), xres[n_sc:].reshape(t - n_sc, c, LANES)
    gt3 = gt.reshape(-1, c, LANES)
    blocks = (t - n_sc) // tb
    split = PEER_TC_SPLIT if n_sc and blocks >= sum(PEER_TC_SPLIT) else (1,) if blocks else ()
    cum = [sum(split[:j]) for j in range(len(split) + 1)]
    bounds = [n_sc + (blocks * cj // max(cum[-1], 1)) * tb for cj in cum]

    def gather_call(j, after):
        lo, hi = bounds[j], bounds[j + 1]
        return _peer_expert(idx_flat, gate_t, h3, xres3, gt3, uv, rows_per_batch=rows_per_batch, tok0=lo, ntok=hi - lo,
                            hx0=n_sc, tb=tb, after=after)

    sc_outs, tc_outs, last = [], [], None
    if n_sc:
        h_flat, xres_flat, gt_flat = h.reshape(-1), xres.reshape(-1), gt.reshape(-1)
        acts = [_sc_act(u2, idx_flat, h_flat, tok0=i * ch, ntok=ch, d=d) for i in range(PEER_SC_CALLS)]
        for i, a in enumerate(acts):
            if len(tc_outs) < len(split):
                tc_outs.append(gather_call(len(tc_outs), last))
                last = tc_outs[-1]
            last = _peer_weight(a.reshape(ch, PEER_SEL), gate_tok, tok0=i * ch, after=last)
            sc_outs.append(_sc_out(v2, idx_flat, last.reshape(-1), xres_flat, gt_flat, tok0=i * ch, ntok=ch, d=d,
                                   rows_per_batch=rows_per_batch).reshape(ch, d))
    while len(tc_outs) < len(split):
        tc_outs.append(gather_call(len(tc_outs), last))
        last = tc_outs[-1]
    return jnp.concatenate(sc_outs + [o.reshape(-1, d) for o in tc_outs], axis=0)


def _peer_ffn(x2d, g, sc, sh, gt, w_q, sub_keys, u, v, *, rows_per_batch):
    tm = min(512, rows_per_batch)
    q, h = _nm_matmul(x2d, g, sc, sh, w_q, rows_per_batch=rows_per_batch, tm=tm, emit_h=True)
    _, gate_t, idx_tok, gate_tok = _peer_route(q, sub_keys, tm=min(512, rows_per_batch))
    return _peer_experts(idx_tok.reshape(-1), gate_t, gate_tok, h, x2d, gt, *_pack_tables(u, v),
                         rows_per_batch=rows_per_batch)


def _peer_test(x, g, sc, sh, w_q, sub_keys, u, v):
    B, S, D = x.shape
    x2d = x.reshape(B * S, D)
    return (_peer_ffn(x2d, g, sc, sh, jnp.ones_like(sc), w_q.astype(BF16), sub_keys, u, v,
                      rows_per_batch=S) - x2d).reshape(B, S, D)


def _final_norm_kernel(x_ref, g_ref, o_ref):
    x = x_ref[...]
    ms = jnp.mean(x * x, axis=-1, keepdims=True)
    o_ref[...] = x * lax.rsqrt(ms + NORM_EPS) * g_ref[...]


def _final_norm(x2d, g, *, tm=512):
    m, d = x2d.shape
    tile = pl.BlockSpec((tm, d), lambda i: (i, 0))
    return pl.pallas_call(
        _final_norm_kernel,
        out_shape=jax.ShapeDtypeStruct((m, d), F32),
        grid=(m // tm,),
        in_specs=[tile, pl.BlockSpec((1, d), lambda i: (0, 0))],
        out_specs=tile,
        compiler_params=_cparams(("parallel",)),
        name="final_norm",
    )(x2d, g.reshape(1, d))


def kernel(x, c, ada_w, ada_b, norm_mix_g, norm_ffn_g, final_norm_g, ret_w_in, ret_w_out, ret_gn_g, ret_gn_b, gdn_w_in, gdn_conv_w, gdn_a_log, gdn_dt_bias, gdn_norm_g, gdn_w_out, rwkv_mu, rwkv_w_r, rwkv_w_k, rwkv_w_v, rwkv_w_o, rwkv_w0, rwkv_w1, rwkv_w2, rwkv_a0, rwkv_a1, rwkv_a2, rwkv_g1, rwkv_g2, rwkv_k_k, rwkv_k_a, rwkv_r_k, rwkv_ln_g, rwkv_ln_b, peer_w_q, peer_sub_keys, peer_u, peer_v):
    B, S, D = x.shape
    T = B * S
    depth = ada_w.shape[0]
    bf = lambda t: t.astype(BF16)
    mod = _adaln(c, ada_w, ada_b)
    x2d = x.reshape(T, D)
    for layer in range(depth):
        sh_m, sc_m, gt_m, sh_f, sc_f, gt_f = [mod[layer, :, i * D:(i + 1) * D] for i in range(6)]
        g_mix = norm_mix_g[layer]
        kind, j = layer % 3, layer // 3
        if kind == 0:
            proj = _nm_matmul(x2d, g_mix, sc_m, sh_m, bf(ret_w_in[j]), rows_per_batch=S, tn=2048)
            o = _retention_scan(proj.reshape(B, S, -1), ret_gn_g[j], ret_gn_b[j])
            x2d = _mm_res(o.reshape(T, -1), bf(ret_w_out[j]), x2d, gt_m, rows_per_batch=S)
        elif kind == 1:
            w = gdn_w_in[j]
            wide = GDN_QKV + GDN_HEADS * GDN_DV
            proj = _nm_matmul(x2d, g_mix, sc_m, sh_m, bf(w[:, :wide]), rows_per_batch=S, tn=2048)
            proj_ab = _nm_matmul(x2d, g_mix, sc_m, sh_m, bf(_pad_cols(w[:, wide:], LANES)), rows_per_batch=S)
            o = _gdn_scan(proj.reshape(B, S, -1), proj_ab.reshape(B, S, -1), gdn_conv_w[j], gdn_a_log[j],
                          gdn_dt_bias[j], gdn_norm_g[j])
            x2d = _mm_res(o.reshape(T, -1), bf(gdn_w_out[j]), x2d, gt_m, rows_per_batch=S)
        else:
            x2d = _rwkv_mixer(x2d.reshape(B, S, D), g_mix, sc_m, sh_m, gt_m, rwkv_mu[j], rwkv_w_r[j], rwkv_w_k[j],
                              rwkv_w_v[j], rwkv_w_o[j], rwkv_w0[j], rwkv_w1[j], rwkv_w2[j], rwkv_a0[j], rwkv_a1[j],
                              rwkv_a2[j], rwkv_g1[j], rwkv_g2[j], rwkv_k_k[j], rwkv_k_a[j], rwkv_r_k[j],
                              rwkv_ln_g[j], rwkv_ln_b[j]).reshape(T, D)
        x2d = _peer_ffn(x2d, norm_ffn_g[layer], sc_f, sh_f, gt_f, bf(peer_w_q[layer]), peer_sub_keys[layer],
                        peer_u[layer], peer_v[layer], rows_per_batch=S)
    return _final_norm(x2d, final_norm_g).reshape(B, S, D)
```
